```python
import math, functools
import jax, jax.numpy as jnp
from jax import lax
import numpy as np

D_MODEL = 2048
BATCH = 2
SEQ = 4096
DEPTH = 4
DEC_BATCH = 2
DEC_SEQ = 8192
PAST_LEN = 128

N_MIXERS = 3
MIX_WIDTH = 3 * D_MODEL // 2
XA_HEADS = 4
XA_HEAD_DIM = D_MODEL // 8
XA_WIDTH = XA_HEADS * XA_HEAD_DIM
BRANCH = MIX_WIDTH + XA_WIDTH
MEM_LEN = 256
CONV_W = 4
CONV_LEFT = 2
LRU_BLOCKS = 12
LRU_BLOCK = MIX_WIDTH // LRU_BLOCKS
LRU_C = 8.0
QK_NOPE = 128
QK_ROPE = 64
V_DIM = 128
MLA_HEADS = MIX_WIDTH // V_DIM
Q_LORA = D_MODEL // 4
KV_LORA = D_MODEL // 8
ROPE_THETA = 10000.0
Q_BLOCK = 128
POOL_WINDOWS = (2, 4, 8, 16)
POOL_GROUPS = len(POOL_WINDOWS)
POOL_GROUP = MIX_WIDTH // POOL_GROUPS
IN_A = MIX_WIDTH + XA_WIDTH + BRANCH
IN_B = Q_LORA + KV_LORA + QK_ROPE + XA_WIDTH + BRANCH
IN_C = MIX_WIDTH + XA_WIDTH + BRANCH
N_A = (DEPTH + 2) // 3
N_B = (DEPTH + 1) // 3
N_C = DEPTH // 3
EPS = 1e-6

kernel_name = 'hybrid_bidir_lru_mla_pool_memxattn'


def rmsnorm(x, g):
    xf = x.astype(jnp.float32)
    y = xf * lax.rsqrt(jnp.mean(xf * xf, axis=-1, keepdims=True) + EPS)
    return (y * g.astype(jnp.float32)).astype(x.dtype)


def centred_dwconv(x, w, b):
    S = x.shape[1]
    xp = jnp.pad(x, ((0, 0), (CONV_LEFT, CONV_W - 1 - CONV_LEFT), (0, 0)))
    out = b
    for k in range(CONV_W):
        out = out + xp[:, k:k + S] * w[k]
    return out


def rglru_scan(u, gate_w, gate_b, lam, reverse):
    Bsz, S, C = u.shape
    ub = u.reshape(Bsz, S, LRU_BLOCKS, LRU_BLOCK)
    g = jnp.einsum('bshi,ghij->gbshj', ub, gate_w.astype(jnp.float32)).reshape(2, Bsz, S, C)
    g = g + gate_b.astype(jnp.float32)[:, None, None, :]
    r = jax.nn.sigmoid(g[0])
    i = jax.nn.sigmoid(g[1])
    log_a = -LRU_C * r * jax.nn.softplus(-lam.astype(jnp.float32))
    a = jnp.exp(log_a)
    b = jnp.sqrt(-jnp.expm1(2.0 * log_a)) * (i * u)

    def combine(left, right):
        a1, b1 = left
        a2, b2 = right
        return a1 * a2, a2 * b1 + b2

    _, h = lax.associative_scan(combine, (a, b), reverse=reverse, axis=1)
    return h


def rglru_mixer(xs, conv_w, conv_b, gate_w, gate_b, lam):
    u = centred_dwconv(xs, conv_w, conv_b).astype(jnp.float32)
    h = rglru_scan(u, gate_w[0], gate_b[0], lam[0], False) + rglru_scan(u, gate_w[1], gate_b[1], lam[1], True)
    return h.astype(xs.dtype)


def rope_tables(S):
    inv_freq = 1.0 / (ROPE_THETA ** (jnp.arange(0, QK_ROPE, 2, dtype=jnp.float32) / QK_ROPE))
    ang = jnp.arange(S, dtype=jnp.float32)[:, None] * inv_freq[None, :]
    return jnp.cos(ang), jnp.sin(ang)


def rotary(x, cos, sin):
    extra = x.ndim - 3
    c = cos.reshape(cos.shape[0], *([1] * extra), cos.shape[1])
    s = sin.reshape(sin.shape[0], *([1] * extra), sin.shape[1])
    xf = x.astype(jnp.float32)
    x1, x2 = jnp.split(xf, 2, axis=-1)
    return jnp.concatenate([x1 * c - x2 * s, x2 * c + x1 * s], axis=-1).astype(x.dtype)


def mla_mixer(c_q, c_kv, k_rope, q_norm, kv_norm, w_q_up, w_kv_up):
    Bsz, S, _ = c_q.shape
    q = (rmsnorm(c_q, q_norm) @ w_q_up).reshape(Bsz, S, MLA_HEADS, QK_NOPE + QK_ROPE)
    q_nope, q_rope = q[..., :QK_NOPE], q[..., QK_NOPE:]
    kv = (rmsnorm(c_kv, kv_norm) @ w_kv_up).reshape(Bsz, S, MLA_HEADS, QK_NOPE + V_DIM)
    k_nope, v = kv[..., :QK_NOPE], kv[..., QK_NOPE:]
    cos, sin = rope_tables(S)
    q_rope = rotary(q_rope, cos, sin)
    k_rope = rotary(k_rope, cos, sin)
    scale = (QK_NOPE + QK_ROPE) ** -0.5
    nb = S // Q_BLOCK

    def to_blocks(t):
        return jnp.moveaxis(t.reshape(Bsz, nb, Q_BLOCK, *t.shape[2:]), 1, 0)

    def one_block(args):
        qn, qr = args
        s = (jnp.einsum('bqhd,bkhd->bhqk', qn, k_nope, preferred_element_type=jnp.float32)
             + jnp.einsum('bqhr,bkr->bhqk', qr, k_rope, preferred_element_type=jnp.float32)) * scale
        p = jax.nn.softmax(s, axis=-1).astype(v.dtype)
        return jnp.einsum('bhqk,bkhd->bqhd', p, v)

    o = lax.map(one_block, (to_blocks(q_nope), to_blocks(q_rope)))
    return jnp.moveaxis(o, 0, 1).reshape(Bsz, S, MLA_HEADS * V_DIM)


def pool_mixer(xs, w_group, scale):
    Bsz, S, C = xs.shape
    xg = xs.astype(jnp.float32).reshape(Bsz, S, POOL_GROUPS, POOL_GROUP)
    cs = jnp.pad(jnp.cumsum(xg, axis=1), ((0, 0), (1, 0), (0, 0), (0, 0)))
    t = jnp.arange(S)
    outs = []
    for g, w in enumerate(POOL_WINDOWS):
        start = jnp.clip(t - w // 2, 0, S)
        end = jnp.clip(t + w - w // 2, 0, S)
        mean = (cs[:, end, g] - cs[:, start, g]) / (end - start).astype(jnp.float32)[None, :, None]
        outs.append(mean - xg[:, :, g])
    pooled = jnp.stack(outs, axis=2)
    y = jnp.einsum('bsgi,gij->bsgj', pooled, w_group.astype(jnp.float32)).reshape(Bsz, S, C)
    return (y * scale.astype(jnp.float32)).astype(xs.dtype)


def memory_cross_attention(xq, mem, norm_g, w_kv):
    Bsz, S, _ = xq.shape
    kv = rmsnorm(mem, norm_g) @ w_kv
    k, v = jnp.split(kv, 2, axis=-1)
    k = k.reshape(Bsz, MEM_LEN, XA_HEADS, XA_HEAD_DIM)
    v = v.reshape(Bsz, MEM_LEN, XA_HEADS, XA_HEAD_DIM)
    q = xq.reshape(Bsz, S, XA_HEADS, XA_HEAD_DIM)
    s = jnp.einsum('bshd,bmhd->bhsm', q, k, preferred_element_type=jnp.float32) * (XA_HEAD_DIM ** -0.5)
    p = jax.nn.softmax(s, axis=-1).astype(v.dtype)
    return jnp.einsum('bhsm,bmhd->bshd', p, v).reshape(Bsz, S, XA_WIDTH)


def trunk(x, mem, norm_pre, norm_post, norm_mem, w_mem_kv, w_out,
          a_w_in, a_conv_w, a_conv_b, a_gate_w, a_gate_b, a_lambda,
          b_w_in, b_q_norm, b_kv_norm, b_w_q_up, b_w_kv_up,
          c_w_in, c_w_group, c_scale):
    for i in range(DEPTH):
        kind, j = i % N_MIXERS, i // N_MIXERS
        h = rmsnorm(x, norm_pre[i])
        if kind == 0:
            u = h @ a_w_in[j]
            xs, xq, gate = jnp.split(u, [MIX_WIDTH, MIX_WIDTH + XA_WIDTH], axis=-1)
            mix = rglru_mixer(xs, a_conv_w[j], a_conv_b[j], a_gate_w[j], a_gate_b[j], a_lambda[j])
        elif kind == 1:
            u = h @ b_w_in[j]
            o1 = Q_LORA
            o2 = o1 + KV_LORA
            o3 = o2 + QK_ROPE
            o4 = o3 + XA_WIDTH
            c_q, c_kv, k_r, xq, gate = jnp.split(u, [o1, o2, o3, o4], axis=-1)
            mix = mla_mixer(c_q, c_kv, k_r, b_q_norm[j], b_kv_norm[j], b_w_q_up[j], b_w_kv_up[j])
        else:
            u = h @ c_w_in[j]
            xs, xq, gate = jnp.split(u, [MIX_WIDTH, MIX_WIDTH + XA_WIDTH], axis=-1)
            mix = pool_mixer(xs, c_w_group[j], c_scale[j])
        xa = memory_cross_attention(xq, mem, norm_mem[i], w_mem_kv[i])
        y = jnp.concatenate([mix.astype(x.dtype), xa.astype(x.dtype)], axis=-1) * jax.nn.silu(gate)
        x = x + rmsnorm(y @ w_out[i], norm_post[i])
    return x


def setup_inputs(seed: int = 0) -> dict:
    key = jax.random.key(seed)
    ks = iter(jax.random.split(key, 32))

    def nrm(shape, s):
        return jax.random.normal(next(ks), shape, jnp.float32) * s

    def gain(shape):
        return 1.0 + nrm(shape, 0.02)

    p = jax.random.uniform(next(ks), (N_A, 2, MIX_WIDTH), jnp.float32, minval=0.9, maxval=0.999) ** (1.0 / LRU_C)
    a_lambda = jnp.log(p) - jnp.log1p(-p)
    return {
        'x_prompt': nrm((BATCH, SEQ, D_MODEL), 1.0),
        'x_sample': nrm((DEC_BATCH, DEC_SEQ, D_MODEL), 1.0),
        'mem_prompt': nrm((BATCH, MEM_LEN, D_MODEL), 1.0),
        'mem_sample': nrm((DEC_BATCH, MEM_LEN, D_MODEL), 1.0),
        'norm_pre': gain((DEPTH, D_MODEL)),
        'norm_post': gain((DEPTH, D_MODEL)),
        'norm_mem': gain((DEPTH, D_MODEL)),
        'w_mem_kv': nrm((DEPTH, D_MODEL, 2 * XA_WIDTH), D_MODEL ** -0.5),
        'w_out': nrm((DEPTH, BRANCH, D_MODEL), BRANCH ** -0.5),
        'a_w_in': nrm((N_A, D_MODEL, IN_A), D_MODEL ** -0.5),
        'a_conv_w': nrm((N_A, CONV_W, MIX_WIDTH), CONV_W ** -0.5),
        'a_conv_b': nrm((N_A, MIX_WIDTH), 0.01),
        'a_gate_w': nrm((N_A, 2, 2, LRU_BLOCKS, LRU_BLOCK, LRU_BLOCK), LRU_BLOCK ** -0.5),
        'a_gate_b': nrm((N_A, 2, 2, MIX_WIDTH), 0.01),
        'a_lambda': a_lambda,
        'b_w_in': nrm((N_B, D_MODEL, IN_B), D_MODEL ** -0.5),
        'b_q_norm': gain((N_B, Q_LORA)),
        'b_kv_norm': gain((N_B, KV_LORA)),
        'b_w_q_up': nrm((N_B, Q_LORA, MLA_HEADS * (QK_NOPE + QK_ROPE)), Q_LORA ** -0.5),
        'b_w_kv_up': nrm((N_B, KV_LORA, MLA_HEADS * (QK_NOPE + V_DIM)), KV_LORA ** -0.5),
        'c_w_in': nrm((N_C, D_MODEL, IN_C), D_MODEL ** -0.5),
        'c_w_group': nrm((N_C, POOL_GROUPS, POOL_GROUP, POOL_GROUP), POOL_GROUP ** -0.5),
        'c_scale': gain((N_C, MIX_WIDTH)),
    }


def reference(x_prompt, x_sample, mem_prompt, mem_sample, norm_pre, norm_post, norm_mem, w_mem_kv, w_out,
              a_w_in, a_conv_w, a_conv_b, a_gate_w, a_gate_b, a_lambda,
              b_w_in, b_q_norm, b_kv_norm, b_w_q_up, b_w_kv_up,
              c_w_in, c_w_group, c_scale):
    y_prompt = trunk(x_prompt, mem_prompt, norm_pre, norm_post, norm_mem, w_mem_kv, w_out,
                     a_w_in, a_conv_w, a_conv_b, a_gate_w, a_gate_b, a_lambda,
                     b_w_in, b_q_norm, b_kv_norm, b_w_q_up, b_w_kv_up,
                     c_w_in, c_w_group, c_scale)
    y_sample = trunk(x_sample, mem_sample, norm_pre, norm_post, norm_mem, w_mem_kv, w_out,
                     a_w_in, a_conv_w, a_conv_b, a_gate_w, a_gate_b, a_lambda,
                     b_w_in, b_q_norm, b_kv_norm, b_w_q_up, b_w_kv_up,
                     c_w_in, c_w_group, c_scale)
    return (y_prompt, y_sample)
```

```python
import functools
import math

import jax
import jax.numpy as jnp
from jax import lax
from jax.experimental import pallas as pl
from jax.experimental.pallas import tpu as pltpu

D_MODEL = 2048
MIX_WIDTH = 3 * D_MODEL // 2
XA_HEADS = 4
XA_HEAD_DIM = D_MODEL // 8
XA_WIDTH = XA_HEADS * XA_HEAD_DIM
BRANCH = MIX_WIDTH + XA_WIDTH
MEM_LEN = 256
N_MIXERS = 3
CONV_W = 4
CONV_LEFT = 2
LRU_BLOCKS = 12
LRU_BLOCK = MIX_WIDTH // LRU_BLOCKS
LRU_C = 8.0
QK_NOPE = 128
QK_ROPE = 64
V_DIM = 128
MLA_HEADS = MIX_WIDTH // V_DIM
Q_LORA = D_MODEL // 4
KV_LORA = D_MODEL // 8
ROPE_THETA = 10000.0
POOL_WINDOWS = (2, 4, 8, 16)
POOL_GROUP = MIX_WIDTH // len(POOL_WINDOWS)
EPS = 1e-6

LANES = 128
SUBLANES = 8
BF16_ROWS = 16
QK_PAD = 256
VMEM_LIMIT = 56 * 1024 * 1024

B_GATE_OFF = 0
B_XQ_OFF = BRANCH
B_CQ_OFF = B_XQ_OFF + XA_WIDTH
B_CKV_OFF = B_CQ_OFF + Q_LORA
B_KR_OFF = B_CKV_OFF + KV_LORA
B_IN_WIDTH = B_KR_OFF + 2 * QK_ROPE + LANES

LRU_CH = 1024
LRU_LANE_GROUPS = LRU_CH // LANES
LRU_HALO = BF16_ROWS

POOL_HALO = BF16_ROWS


def _params(*sem):
    return pltpu.CompilerParams(dimension_semantics=sem, vmem_limit_bytes=VMEM_LIMIT)


def _resident(block_shape, index_map):
    return pl.BlockSpec(block_shape, index_map, pipeline_mode=pl.Buffered(1))


def _sigmoid(x):
    return 1.0 / (1.0 + jnp.exp(-x))


def _norm_matmul_kernel(x_ref, g_ref, w_ref, o_ref, h_ref):
    @pl.when(pl.program_id(1) == 0)
    def _():
        x = x_ref[...].astype(jnp.float32)
        ms = jnp.mean(x * x, axis=-1, keepdims=True)
        h_ref[...] = ((x * lax.rsqrt(ms + EPS)) * g_ref[...]).astype(h_ref.dtype)

    o_ref[...] = jnp.dot(h_ref[...], w_ref[...],
                         preferred_element_type=jnp.float32).astype(o_ref.dtype)


def _norm_matmul(x, g, w, *, tm, tn):
    T, K = x.shape
    N = w.shape[1]
    tm = min(tm, T)
    return pl.pallas_call(
        _norm_matmul_kernel,
        grid=(T // tm, N // tn),
        in_specs=[
            pl.BlockSpec((tm, K), lambda i, j: (i, 0)),
            pl.BlockSpec((1, K), lambda i, j: (0, 0)),
            pl.BlockSpec((K, tn), lambda i, j: (0, j)),
        ],
        out_specs=pl.BlockSpec((tm, tn), lambda i, j: (i, j)),
        out_shape=jax.ShapeDtypeStruct((T, N), jnp.bfloat16),
        scratch_shapes=[pltpu.VMEM((tm, K), jnp.bfloat16)],
        compiler_params=_params("parallel", "arbitrary"),
        name="norm_matmul",
    )(x, g.reshape(1, K), w)


def _out_kernel(*refs, n_mix):
    mix_refs = refs[:n_mix]
    xq_ref, gate_ref, kv_ref, w_ref, x_ref, g_ref, o_ref = refs[n_mix:]

    mix = mix_refs[0][...].astype(jnp.float32)
    for r in mix_refs[1:]:
        mix = mix + r[...].astype(jnp.float32)

    xq = xq_ref[...]
    scale = XA_HEAD_DIM ** -0.5
    xa = []
    for h in range(XA_HEADS):
        lo = h * XA_HEAD_DIM
        q = xq[:, lo:lo + XA_HEAD_DIM]
        k = kv_ref[:, lo:lo + XA_HEAD_DIM]
        v = kv_ref[:, XA_WIDTH + lo:XA_WIDTH + lo + XA_HEAD_DIM]
        s = lax.dot_general(q, k, (((1,), (1,)), ((), ())),
                            preferred_element_type=jnp.float32) * scale
        m = jnp.max(s, axis=-1, keepdims=True)
        p = jnp.exp(s - m)
        l = jnp.sum(p, axis=-1, keepdims=True)
        p = (p / l).astype(jnp.bfloat16)
        xa.append(jnp.dot(p, v, preferred_element_type=jnp.float32))

    gate = gate_ref[...].astype(jnp.float32)
    act = gate * _sigmoid(gate)
    y_mix = (mix * act[:, :MIX_WIDTH]).astype(jnp.bfloat16)
    acc = jnp.dot(y_mix, w_ref[:MIX_WIDTH, :], preferred_element_type=jnp.float32)
    for h in range(XA_HEADS):
        lo = MIX_WIDTH + h * XA_HEAD_DIM
        y_h = (xa[h] * act[:, lo:lo + XA_HEAD_DIM]).astype(jnp.bfloat16)
        acc = acc + jnp.dot(y_h, w_ref[lo:lo + XA_HEAD_DIM, :],
                            preferred_element_type=jnp.float32)

    ms = jnp.mean(acc * acc, axis=-1, keepdims=True)
    o_ref[...] = x_ref[...] + (acc * lax.rsqrt(ms + EPS)) * g_ref[...]


def _out_block(mixes, u, xq_off, gate_off, kv, w_out, x, g_post, *, seq, tm):
    T = x.shape[0]
    tm = min(tm, seq)
    per_seq = seq // tm
    xq_blk = xq_off // XA_WIDTH
    gate_blk = gate_off // BRANCH
    mix_spec = pl.BlockSpec((tm, MIX_WIDTH), lambda i: (i, 0))
    return pl.pallas_call(
        functools.partial(_out_kernel, n_mix=len(mixes)),
        grid=(T // tm,),
        in_specs=[mix_spec] * len(mixes) + [
            pl.BlockSpec((tm, XA_WIDTH), lambda i: (i, xq_blk)),
            pl.BlockSpec((tm, BRANCH), lambda i: (i, gate_blk)),
            pl.BlockSpec((MEM_LEN, 2 * XA_WIDTH), lambda i: (i // per_seq, 0)),
            _resident((BRANCH, D_MODEL), lambda i: (0, 0)),
            pl.BlockSpec((tm, D_MODEL), lambda i: (i, 0)),
            pl.BlockSpec((1, D_MODEL), lambda i: (0, 0)),
        ],
        out_specs=pl.BlockSpec((tm, D_MODEL), lambda i: (i, 0)),
        out_shape=jax.ShapeDtypeStruct((T, D_MODEL), jnp.float32),
        compiler_params=_params("parallel"),
        name="xattn_gate_outproj",
    )(*mixes, u, u, kv, w_out, x, g_post.reshape(1, D_MODEL))


def _lru_kernel(cur_f, prev_f, next_f, cur_r, prev_r, next_r, cw_ref, cb_ref, wg_ref, gb_ref,
                lam_ref, hf_ref, hr_ref, a_f, b_f, a_r, b_r, carry, *, rows, pitch):
    s = pl.program_id(2)
    n = pl.num_programs(2)
    win = rows + 2 * LRU_HALO

    @pl.when(s == 0)
    def _():
        carry[...] = jnp.zeros_like(carry)

    def gates(cur, prev, nxt, chunk, d, a_s, b_s):
        p = prev[...].astype(jnp.float32) * (chunk > 0).astype(jnp.float32)
        q = nxt[...].astype(jnp.float32) * (chunk < n - 1).astype(jnp.float32)
        w = jnp.concatenate([p, cur[...].astype(jnp.float32), q], axis=0)
        taps = (pltpu.roll(w, 2, 0), pltpu.roll(w, 1, 0), w, pltpu.roll(w, win - 1, 0))
        u = cb_ref[...]
        for k in range(CONV_W):
            u = u + taps[k][LRU_HALO:LRU_HALO + rows] * cw_ref[k:k + 1, :]
        lam = lam_ref[d]
        softplus = jnp.maximum(-lam, 0.0) + jnp.log(1.0 + jnp.exp(-jnp.abs(lam)))
        decay = -LRU_C * softplus
        for blk in range(LRU_CH // LRU_BLOCK):
            lo = blk * LRU_BLOCK
            ub = u[:, lo:lo + LRU_BLOCK]
            g = jnp.dot(ub.astype(jnp.bfloat16), wg_ref[d, blk],
                        preferred_element_type=jnp.float32) + gb_ref[d, blk]
            r = _sigmoid(g[:, :LRU_BLOCK])
            i = _sigmoid(g[:, LRU_BLOCK:])
            a = jnp.exp(decay[:, lo:lo + LRU_BLOCK] * r)
            b = jnp.sqrt(1.0 - a * a) * (i * ub)
            for half in range(LRU_BLOCK // LANES):
                base = (blk * (LRU_BLOCK // LANES) + half) * pitch
                a_s[base:base + rows, :] = a[:, half * LANES:(half + 1) * LANES]
                b_s[base:base + rows, :] = b[:, half * LANES:(half + 1) * LANES]

    gates(cur_f, prev_f, next_f, s, 0, a_f, b_f)
    gates(cur_r, prev_r, next_r, n - 1 - s, 1, a_r, b_r)

    unroll = 8

    def body(grp, hs):
        hf, hr = hs
        for j in range(unroll):
            t = grp * unroll + j
            f_idx = pl.ds(t, LRU_LANE_GROUPS, stride=pitch)
            hf = a_f[f_idx, :] * hf + b_f[f_idx, :]
            b_f[f_idx, :] = hf
            r_idx = pl.ds(rows - 1 - t, LRU_LANE_GROUPS, stride=pitch)
            hr = a_r[r_idx, :] * hr + b_r[r_idx, :]
            b_r[r_idx, :] = hr
        return hf, hr

    hf, hr = lax.fori_loop(0, rows // unroll, body, (carry[0], carry[1]))
    carry[0] = hf
    carry[1] = hr

    for grp in range(LRU_LANE_GROUPS):
        base = grp * pitch
        hf_ref[:, grp * LANES:(grp + 1) * LANES] = b_f[base:base + rows, :].astype(hf_ref.dtype)
        hr_ref[:, grp * LANES:(grp + 1) * LANES] = b_r[base:base + rows, :].astype(hr_ref.dtype)


def _lru_mixer(u, conv_w, conv_b, wg, gb, lam, *, batch, seq, rows):
    T = u.shape[0]
    rows = min(rows, seq)
    n = seq // rows
    pitch = rows + SUBLANES
    hb = rows // LRU_HALO
    last_halo = T // LRU_HALO - 1

    def cur_map(rev):
        def f(b, c, s):
            chunk = (n - 1 - s) if rev else s
            return (b * n + chunk, c)
        return f

    def prev_map(rev):
        def f(b, c, s):
            chunk = (n - 1 - s) if rev else s
            return (jnp.maximum((b * n + chunk) * hb - 1, 0), c)
        return f

    def next_map(rev):
        def f(b, c, s):
            chunk = (n - 1 - s) if rev else s
            return (jnp.minimum((b * n + chunk + 1) * hb, last_halo), c)
        return f

    blocks = LRU_CH // LRU_BLOCK
    in_specs = []
    for rev in (False, True):
        in_specs += [
            pl.BlockSpec((rows, LRU_CH), cur_map(rev)),
            pl.BlockSpec((LRU_HALO, LRU_CH), prev_map(rev)),
            pl.BlockSpec((LRU_HALO, LRU_CH), next_map(rev)),
        ]
    in_specs += [
        pl.BlockSpec((CONV_W, LRU_CH), lambda b, c, s: (0, c)),
        pl.BlockSpec((1, LRU_CH), lambda b, c, s: (0, c)),
        pl.BlockSpec((2, blocks, LRU_BLOCK, 2 * LRU_BLOCK), lambda b, c, s: (0, c, 0, 0)),
        pl.BlockSpec((2, blocks, 1, 2 * LRU_BLOCK), lambda b, c, s: (0, c, 0, 0)),
        pl.BlockSpec((2, 1, LRU_CH), lambda b, c, s: (0, 0, c)),
    ]
    slab = pltpu.VMEM((LRU_LANE_GROUPS * pitch, LANES), jnp.float32)
    out = jax.ShapeDtypeStruct((T, MIX_WIDTH), jnp.bfloat16)
    return pl.pallas_call(
        functools.partial(_lru_kernel, rows=rows, pitch=pitch),
        grid=(batch, MIX_WIDTH // LRU_CH, n),
        in_specs=in_specs,
        out_specs=[pl.BlockSpec((rows, LRU_CH), cur_map(False)),
                   pl.BlockSpec((rows, LRU_CH), cur_map(True))],
        out_shape=[out, out],
        scratch_shapes=[slab, slab, slab, slab,
                        pltpu.VMEM((2, LRU_LANE_GROUPS, LANES), jnp.float32)],
        compiler_params=_params("parallel", "parallel", "arbitrary"),
        name="rglru",
    )(u, u, u, u, u, u, conv_w, conv_b.reshape(1, MIX_WIDTH), wg, gb,
      lam.reshape(2, 1, MIX_WIDTH))


def _mla_q_kernel(cq_ref, g_ref, wa_ref, wb_ref, ct_ref, st_ref, o_ref, h_ref, *, heads):
    @pl.when(pl.program_id(1) == 0)
    def _():
        x = cq_ref[...].astype(jnp.float32)
        ms = jnp.mean(x * x, axis=-1, keepdims=True)
        h_ref[...] = ((x * lax.rsqrt(ms + EPS)) * g_ref[...]).astype(h_ref.dtype)

    h = h_ref[...]
    qa = jnp.dot(h, wa_ref[...], preferred_element_type=jnp.float32)
    qb = jnp.dot(h, wb_ref[...], preferred_element_type=jnp.float32)
    scale = (QK_NOPE + QK_ROPE) ** -0.5
    ct = ct_ref[...] * scale
    st = st_ref[...] * scale
    for hh in range(heads):
        lo = hh * QK_PAD
        o_ref[:, lo:lo + LANES] = (qa[:, lo:lo + LANES] * scale).astype(o_ref.dtype)
        rope = qa[:, lo + LANES:lo + 2 * LANES] * ct + qb[:, hh * LANES:(hh + 1) * LANES] * st
        o_ref[:, lo + LANES:lo + 2 * LANES] = rope.astype(o_ref.dtype)


def _mla_q(u, q_norm, wa, wb, ct, st, *, seq, tm, heads):
    T = u.shape[0]
    tm = min(tm, seq)
    per_seq = seq // tm
    cq_blk = B_CQ_OFF // Q_LORA
    return pl.pallas_call(
        functools.partial(_mla_q_kernel, heads=heads),
        grid=(T // tm, MLA_HEADS // heads),
        in_specs=[
            pl.BlockSpec((tm, Q_LORA), lambda i, j: (i, cq_blk)),
            pl.BlockSpec((1, Q_LORA), lambda i, j: (0, 0)),
            pl.BlockSpec((Q_LORA, heads * QK_PAD), lambda i, j: (0, j)),
            pl.BlockSpec((Q_LORA, heads * LANES), lambda i, j: (0, j)),
            pl.BlockSpec((tm, LANES), lambda i, j: (i % per_seq, 0)),
            pl.BlockSpec((tm, LANES), lambda i, j: (i % per_seq, 0)),
        ],
        out_specs=pl.BlockSpec((tm, heads * QK_PAD), lambda i, j: (i, j)),
        out_shape=jax.ShapeDtypeStruct((T, MLA_HEADS * QK_PAD), jnp.bfloat16),
        scratch_shapes=[pltpu.VMEM((tm, Q_LORA), jnp.bfloat16)],
        compiler_params=_params("parallel", "arbitrary"),
        name="mla_q_proj",
    )(u, q_norm.reshape(1, Q_LORA), wa, wb, ct, st)


def _mla_kv_kernel(ckv_ref, g_ref, wk_ref, wv_ref, kr_ref, tab_ref, k_ref, v_ref):
    x = ckv_ref[...].astype(jnp.float32)
    ms = jnp.mean(x * x, axis=-1, keepdims=True)
    h = ((x * lax.rsqrt(ms + EPS)) * g_ref[...]).astype(jnp.bfloat16)
    kn = jnp.dot(h, wk_ref[...], preferred_element_type=jnp.float32)
    v_ref[...] = jnp.dot(h, wv_ref[...], preferred_element_type=jnp.float32).astype(v_ref.dtype)
    t = kr_ref[...].astype(jnp.float32) * tab_ref[...]
    rot = t + pltpu.roll(t, QK_ROPE, 1)
    lane = lax.broadcasted_iota(jnp.int32, rot.shape, 1)
    rot = jnp.where(lane < QK_ROPE, rot, 0.0).astype(k_ref.dtype)
    for hh in range(MLA_HEADS):
        k_ref[:, hh * QK_PAD:hh * QK_PAD + LANES] = kn[:, hh * LANES:(hh + 1) * LANES].astype(k_ref.dtype)
        k_ref[:, hh * QK_PAD + LANES:(hh + 1) * QK_PAD] = rot


def _mla_kv(u, kv_norm, wk, wv, tab, *, seq, tm):
    T = u.shape[0]
    tm = min(tm, seq)
    per_seq = seq // tm
    ckv_blk = B_CKV_OFF // KV_LORA
    kr_blk = B_KR_OFF // LANES
    return pl.pallas_call(
        _mla_kv_kernel,
        grid=(T // tm,),
        in_specs=[
            pl.BlockSpec((tm, KV_LORA), lambda i: (i, ckv_blk)),
            pl.BlockSpec((1, KV_LORA), lambda i: (0, 0)),
            pl.BlockSpec((KV_LORA, MLA_HEADS * QK_NOPE), lambda i: (0, 0)),
            pl.BlockSpec((KV_LORA, MLA_HEADS * V_DIM), lambda i: (0, 0)),
            pl.BlockSpec((tm, LANES), lambda i: (i, kr_blk)),
            pl.BlockSpec((tm, LANES), lambda i: (i % per_seq, 0)),
        ],
        out_specs=[pl.BlockSpec((tm, MLA_HEADS * QK_PAD), lambda i: (i, 0)),
                   pl.BlockSpec((tm, MLA_HEADS * V_DIM), lambda i: (i, 0))],
        out_shape=[jax.ShapeDtypeStruct((T, MLA_HEADS * QK_PAD), jnp.bfloat16),
                   jax.ShapeDtypeStruct((T, MLA_HEADS * V_DIM), jnp.bfloat16)],
        compiler_params=_params("parallel"),
        name="mla_kv_proj",
    )(u, kv_norm.reshape(1, KV_LORA), wk, wv, u, tab)


def _flash_kernel(q_ref, k_ref, v_ref, o_ref, m_ref, l_ref, acc_ref, *, tk, n_kv):
    m_ref[...] = jnp.full_like(m_ref, -jnp.inf)
    l_ref[...] = jnp.zeros_like(l_ref)
    acc_ref[...] = jnp.zeros_like(acc_ref)
    q = q_ref[...]

    def body(j, carry):
        start = pl.multiple_of(j * tk, tk)
        k = k_ref[pl.ds(start, tk), :]
        v = v_ref[pl.ds(start, tk), :]
        s = lax.dot_general(q, k, (((1,), (1,)), ((), ())), preferred_element_type=jnp.float32)
        m_prev = m_ref[...]
        m_new = jnp.maximum(m_prev, jnp.max(s, axis=-1, keepdims=True))
        alpha = jnp.exp(m_prev - m_new)
        p = jnp.exp(s - m_new)
        l_ref[...] = alpha * l_ref[...] + jnp.sum(p, axis=-1, keepdims=True)
        acc_ref[...] = alpha * acc_ref[...] + jnp.dot(p.astype(jnp.bfloat16), v,
                                                      preferred_element_type=jnp.float32)
        m_ref[...] = m_new
        return carry

    lax.fori_loop(0, n_kv, body, 0)
    o_ref[...] = (acc_ref[...] / l_ref[...]).astype(o_ref.dtype)


def _flash_attention(q, k, v, *, batch, seq, tq, tk):
    T = q.shape[0]
    tq = min(tq, seq)
    tk = min(tk, seq)
    nq = seq // tq
    return pl.pallas_call(
        functools.partial(_flash_kernel, tk=tk, n_kv=seq // tk),
        grid=(batch, MLA_HEADS, nq),
        in_specs=[
            pl.BlockSpec((tq, QK_PAD), lambda b, h, i: (b * nq + i, h)),
            pl.BlockSpec((seq, QK_PAD), lambda b, h, i: (b, h)),
            pl.BlockSpec((seq, V_DIM), lambda b, h, i: (b, h)),
        ],
        out_specs=pl.BlockSpec((tq, V_DIM), lambda b, h, i: (b * nq + i, h)),
        out_shape=jax.ShapeDtypeStruct((T, MLA_HEADS * V_DIM), jnp.bfloat16),
        scratch_shapes=[pltpu.VMEM((tq, 1), jnp.float32),
                        pltpu.VMEM((tq, 1), jnp.float32),
                        pltpu.VMEM((tq, V_DIM), jnp.float32)],
        compiler_params=_params("parallel", "parallel", "arbitrary"),
        name="mla_flash_attention",
    )(q, k, v)


def _pool_kernel(cur_ref, prev_ref, next_ref, w_ref, sc_ref, o_ref, *, seq, tm):
    i = pl.program_id(1)
    n = pl.num_programs(1)
    win = tm + 2 * POOL_HALO
    keep_prev = (i > 0).astype(jnp.float32)
    keep_next = (i < n - 1).astype(jnp.float32)
    t = i * tm + lax.broadcasted_iota(jnp.int32, (tm, 1), 0)
    for g, w in enumerate(POOL_WINDOWS):
        lo = g * POOL_GROUP
        x = cur_ref[:, lo:lo + POOL_GROUP].astype(jnp.float32)
        p = prev_ref[:, lo:lo + POOL_GROUP].astype(jnp.float32) * keep_prev
        q = next_ref[:, lo:lo + POOL_GROUP].astype(jnp.float32) * keep_next
        run = jnp.concatenate([p, x, q], axis=0)
        span = 1
        while span < w:
            run = run + pltpu.roll(run, span, 0)
            span *= 2
        ahead = w - w // 2 - 1
        if ahead:
            run = pltpu.roll(run, win - ahead, 0)
        total = run[POOL_HALO:POOL_HALO + tm]
        count = jnp.minimum(t + (w - w // 2), seq) - jnp.maximum(t - w // 2, 0)
        pooled = total / count.astype(jnp.float32) - x
        y = jnp.dot(pooled.astype(jnp.bfloat16), w_ref[g], preferred_element_type=jnp.float32)
        o_ref[:, lo:lo + POOL_GROUP] = (y * sc_ref[:, lo:lo + POOL_GROUP]).astype(o_ref.dtype)


def _pool_mixer(u, w_group, scale, *, batch, seq, tm):
    T = u.shape[0]
    tm = min(tm, seq)
    n = seq // tm
    hb = tm // POOL_HALO
    last_halo = T // POOL_HALO - 1
    return pl.pallas_call(
        functools.partial(_pool_kernel, seq=seq, tm=tm),
        grid=(batch, n),
        in_specs=[
            pl.BlockSpec((tm, MIX_WIDTH), lambda b, i: (b * n + i, 0)),
            pl.BlockSpec((POOL_HALO, MIX_WIDTH),
                         lambda b, i: (jnp.maximum((b * n + i) * hb - 1, 0), 0)),
            pl.BlockSpec((POOL_HALO, MIX_WIDTH),
                         lambda b, i: (jnp.minimum((b * n + i + 1) * hb, last_halo), 0)),
            pl.BlockSpec((len(POOL_WINDOWS), POOL_GROUP, POOL_GROUP), lambda b, i: (0, 0, 0)),
            pl.BlockSpec((1, MIX_WIDTH), lambda b, i: (0, 0)),
        ],
        out_specs=pl.BlockSpec((tm, MIX_WIDTH), lambda b, i: (b * n + i, 0)),
        out_shape=jax.ShapeDtypeStruct((T, MIX_WIDTH), jnp.bfloat16),
        compiler_params=_params("parallel", "parallel"),
        name="pool_mixer",
    )(u, u, u, w_group, scale.reshape(1, MIX_WIDTH))


def _rope_tables(seq):
    inv_freq = 1.0 / (ROPE_THETA ** (jnp.arange(0, QK_ROPE, 2, dtype=jnp.float32) / QK_ROPE))
    ang = jnp.arange(seq, dtype=jnp.float32)[:, None] * inv_freq[None, :]
    cos, sin = jnp.cos(ang), jnp.sin(ang)
    zeros = jnp.zeros((seq, LANES - QK_ROPE), jnp.float32)
    ct = jnp.concatenate([cos, cos, zeros], axis=-1)
    st = jnp.concatenate([-sin, sin, zeros], axis=-1)
    tab = jnp.concatenate([cos, cos, -sin, sin], axis=-1)
    return ct, st, tab


def _swap_halves(w):
    half = QK_ROPE // 2
    return jnp.concatenate([w[..., half:], w[..., :half]], axis=-1)


def _prep_b(b_w_in, b_w_q_up, b_w_kv_up):
    bf = jnp.bfloat16
    K = b_w_in.shape[0]
    o1 = Q_LORA
    o2 = o1 + KV_LORA
    o3 = o2 + QK_ROPE
    o4 = o3 + XA_WIDTH
    w_cq, w_ckv, w_kr, w_xq, w_gate = (b_w_in[:, :o1], b_w_in[:, o1:o2], b_w_in[:, o2:o3],
                                       b_w_in[:, o3:o4], b_w_in[:, o4:])
    w_in = jnp.concatenate([w_gate, w_xq, w_cq, w_ckv, w_kr, _swap_halves(w_kr),
                            jnp.zeros((K, LANES), b_w_in.dtype)], axis=-1).astype(bf)
    wq = b_w_q_up.reshape(Q_LORA, MLA_HEADS, QK_NOPE + QK_ROPE)
    q_nope, q_rope = wq[..., :QK_NOPE], wq[..., QK_NOPE:]
    pad = jnp.zeros((Q_LORA, MLA_HEADS, LANES - QK_ROPE), wq.dtype)
    wa = jnp.concatenate([q_nope, q_rope, pad], axis=-1).reshape(Q_LORA, MLA_HEADS * QK_PAD).astype(bf)
    wb = jnp.concatenate([_swap_halves(q_rope), pad], axis=-1).reshape(Q_LORA, MLA_HEADS * LANES).astype(bf)
    wkv = b_w_kv_up.reshape(KV_LORA, MLA_HEADS, QK_NOPE + V_DIM)
    wk = wkv[..., :QK_NOPE].reshape(KV_LORA, MLA_HEADS * QK_NOPE).astype(bf)
    wv = wkv[..., QK_NOPE:].reshape(KV_LORA, MLA_HEADS * V_DIM).astype(bf)
    return w_in, wa, wb, wk, wv


def _prep_lru(gate_w, gate_b):
    wg = jnp.concatenate([gate_w[:, 0], gate_w[:, 1]], axis=-1).astype(jnp.bfloat16)
    gb = gate_b.reshape(2, 2, LRU_BLOCKS, 1, LRU_BLOCK)
    gb = jnp.concatenate([gb[:, 0], gb[:, 1]], axis=-1)
    return wg, gb


def _trunk(x, mem, w):
    batch, seq, _ = x.shape
    T = batch * seq
    x = x.reshape(T, D_MODEL)
    mem = mem.reshape(batch * MEM_LEN, D_MODEL)
    ct, st, tab = _rope_tables(seq)
    depth = w["norm_pre"].shape[0]
    for i in range(depth):
        kind, j = i % N_MIXERS, i // N_MIXERS
        kv = _norm_matmul(mem, w["norm_mem"][i], w["w_mem_kv"][i], tm=512, tn=1024)
        if kind == 0:
            u = _norm_matmul(x, w["norm_pre"][i], w["a_w_in"][j], tm=1024, tn=1024)
            mixes = _lru_mixer(u, w["a_conv_w"][j], w["a_conv_b"][j], w["a_wg"][j], w["a_gb"][j],
                               w["a_lambda"][j], batch=batch, seq=seq, rows=256)
            xq_off, gate_off = MIX_WIDTH, BRANCH
        elif kind == 1:
            u = _norm_matmul(x, w["norm_pre"][i], w["b_w_in"][j], tm=1024, tn=1024)
            q = _mla_q(u, w["b_q_norm"][j], w["b_wa"][j], w["b_wb"][j], ct, st,
                       seq=seq, tm=512, heads=4)
            k, v = _mla_kv(u, w["b_kv_norm"][j], w["b_wk"][j], w["b_wv"][j], tab, seq=seq, tm=512)
            mixes = [_flash_attention(q, k, v, batch=batch, seq=seq, tq=512, tk=1024)]
            xq_off, gate_off = B_XQ_OFF, B_GATE_OFF
        else:
            u = _norm_matmul(x, w["norm_pre"][i], w["c_w_in"][j], tm=1024, tn=1024)
            mixes = [_pool_mixer(u, w["c_w_group"][j], w["c_scale"][j], batch=batch, seq=seq, tm=512)]
            xq_off, gate_off = MIX_WIDTH, BRANCH
        x = _out_block(mixes, u, xq_off, gate_off, kv, w["w_out"][i], x, w["norm_post"][i],
                       seq=seq, tm=256)
    return x.reshape(batch, seq, D_MODEL)


def kernel(x_prompt, x_sample, mem_prompt, mem_sample, norm_pre, norm_post, norm_mem, w_mem_kv, w_out,
           a_w_in, a_conv_w, a_conv_b, a_gate_w, a_gate_b, a_lambda,
           b_w_in, b_q_norm, b_kv_norm, b_w_q_up, b_w_kv_up,
           c_w_in, c_w_group, c_scale):
    bf = jnp.bfloat16
    b_parts = [_prep_b(b_w_in[j], b_w_q_up[j], b_w_kv_up[j]) for j in range(b_w_in.shape[0])]
    lru_parts = [_prep_lru(a_gate_w[j], a_gate_b[j]) for j in range(a_gate_w.shape[0])]
    w = dict(
        norm_pre=norm_pre, norm_post=norm_post, norm_mem=norm_mem,
        w_mem_kv=w_mem_kv.astype(bf), w_out=w_out.astype(bf),
        a_w_in=a_w_in.astype(bf), a_conv_w=a_conv_w, a_conv_b=a_conv_b,
        a_wg=[p[0] for p in lru_parts], a_gb=[p[1] for p in lru_parts], a_lambda=a_lambda,
        b_w_in=[p[0] for p in b_parts], b_q_norm=b_q_norm, b_kv_norm=b_kv_norm,
        b_wa=[p[1] for p in b_parts], b_wb=[p[2] for p in b_parts],
        b_wk=[p[3] for p in b_parts], b_wv=[p[4] for p in b_parts],
        c_w_in=c_w_in.astype(bf), c_w_group=c_w_group.astype(bf), c_scale=c_scale,
    )
    return (_trunk(x_prompt, mem_prompt, w), _trunk(x_sample, mem_sample, w))
```

```python
import functools
import math

import jax
import jax.numpy as jnp
from jax import lax
from jax.experimental import pallas as pl
from jax.experimental.pallas import tpu as pltpu

D_MODEL = 2048
MIX_WIDTH = 3 * D_MODEL // 2
XA_HEADS = 4
XA_HEAD_DIM = D_MODEL // 8
XA_WIDTH = XA_HEADS * XA_HEAD_DIM
BRANCH = MIX_WIDTH + XA_WIDTH
MEM_LEN = 256
N_MIXERS = 3
CONV_W = 4
CONV_LEFT = 2
LRU_BLOCKS = 12
LRU_BLOCK = MIX_WIDTH // LRU_BLOCKS
LRU_C = 8.0
QK_NOPE = 128
QK_ROPE = 64
V_DIM = 128
MLA_HEADS = MIX_WIDTH // V_DIM
Q_LORA = D_MODEL // 4
KV_LORA = D_MODEL // 8
ROPE_THETA = 10000.0
POOL_WINDOWS = (2, 4, 8, 16)
POOL_GROUP = MIX_WIDTH // len(POOL_WINDOWS)
EPS = 1e-6

LANES = 128
SUBLANES = 8
BF16_ROWS = 16
QK_PAD = 256
VMEM_LIMIT = 56 * 1024 * 1024

B_GATE_OFF = 0
B_XQ_OFF = BRANCH
B_CQ_OFF = B_XQ_OFF + XA_WIDTH
B_CKV_OFF = B_CQ_OFF + Q_LORA
B_KR_OFF = B_CKV_OFF + KV_LORA
B_IN_WIDTH = B_KR_OFF + 2 * QK_ROPE + LANES

LRU_CH = 1024
LRU_LANE_GROUPS = LRU_CH // LANES
LRU_HALO = BF16_ROWS

POOL_HALO = BF16_ROWS


def _params(*sem):
    return pltpu.CompilerParams(dimension_semantics=sem, vmem_limit_bytes=VMEM_LIMIT)


def _resident(block_shape, index_map):
    return pl.BlockSpec(block_shape, index_map, pipeline_mode=pl.Buffered(1))


def _sigmoid(x):
    return 1.0 / (1.0 + jnp.exp(-x))


def _norm_matmul_kernel(x_ref, g_ref, w_ref, o_ref, h_ref):
    @pl.when(pl.program_id(1) == 0)
    def _():
        x = x_ref[...].astype(jnp.float32)
        ms = jnp.mean(x * x, axis=-1, keepdims=True)
        h_ref[...] = ((x * lax.rsqrt(ms + EPS)) * g_ref[...]).astype(h_ref.dtype)

    o_ref[...] = jnp.dot(h_ref[...], w_ref[...],
                         preferred_element_type=jnp.float32).astype(o_ref.dtype)


def _norm_matmul(x, g, w, *, tm, tn):
    T, K = x.shape
    N = w.shape[1]
    tm = min(tm, T)
    return pl.pallas_call(
        _norm_matmul_kernel,
        grid=(T // tm, N // tn),
        in_specs=[
            pl.BlockSpec((tm, K), lambda i, j: (i, 0)),
            pl.BlockSpec((1, K), lambda i, j: (0, 0)),
            pl.BlockSpec((K, tn), lambda i, j: (0, j)),
        ],
        out_specs=pl.BlockSpec((tm, tn), lambda i, j: (i, j)),
        out_shape=jax.ShapeDtypeStruct((T, N), jnp.bfloat16),
        scratch_shapes=[pltpu.VMEM((tm, K), jnp.bfloat16)],
        compiler_params=_params("parallel", "arbitrary"),
        name="norm_matmul",
    )(x, g.reshape(1, K), w)


def _out_kernel(*refs, n_mix):
    mix_refs = refs[:n_mix]
    xq_ref, gate_ref, kv_ref, w_ref, x_ref, g_ref, o_ref = refs[n_mix:]

    mix = mix_refs[0][...].astype(jnp.float32)
    for r in mix_refs[1:]:
        mix = mix + r[...].astype(jnp.float32)

    xq = xq_ref[...]
    scale = XA_HEAD_DIM ** -0.5
    xa = []
    for h in range(XA_HEADS):
        lo = h * XA_HEAD_DIM
        q = xq[:, lo:lo + XA_HEAD_DIM]
        k = kv_ref[:, lo:lo + XA_HEAD_DIM]
        v = kv_ref[:, XA_WIDTH + lo:XA_WIDTH + lo + XA_HEAD_DIM]
        s = lax.dot_general(q, k, (((1,), (1,)), ((), ())),
                            preferred_element_type=jnp.float32) * scale
        m = jnp.max(s, axis=-1, keepdims=True)
        p = jnp.exp(s - m)
        l = jnp.sum(p, axis=-1, keepdims=True)
        p = (p / l).astype(jnp.bfloat16)
        xa.append(jnp.dot(p, v, preferred_element_type=jnp.float32))

    gate = gate_ref[...].astype(jnp.float32)
    act = gate * _sigmoid(gate)
    y_mix = (mix * act[:, :MIX_WIDTH]).astype(jnp.bfloat16)
    acc = jnp.dot(y_mix, w_ref[:MIX_WIDTH, :], preferred_element_type=jnp.float32)
    for h in range(XA_HEADS):
        lo = MIX_WIDTH + h * XA_HEAD_DIM
        y_h = (xa[h] * act[:, lo:lo + XA_HEAD_DIM]).astype(jnp.bfloat16)
        acc = acc + jnp.dot(y_h, w_ref[lo:lo + XA_HEAD_DIM, :],
                            preferred_element_type=jnp.float32)

    ms = jnp.mean(acc * acc, axis=-1, keepdims=True)
    o_ref[...] = x_ref[...] + (acc * lax.rsqrt(ms + EPS)) * g_ref[...]


def _out_block(mixes, u, xq_off, gate_off, kv, w_out, x, g_post, *, seq, tm):
    T = x.shape[0]
    tm = min(tm, seq)
    per_seq = seq // tm
    xq_blk = xq_off // XA_WIDTH
    gate_blk = gate_off // BRANCH
    mix_spec = pl.BlockSpec((tm, MIX_WIDTH), lambda i: (i, 0))
    return pl.pallas_call(
        functools.partial(_out_kernel, n_mix=len(mixes)),
        grid=(T // tm,),
        in_specs=[mix_spec] * len(mixes) + [
            pl.BlockSpec((tm, XA_WIDTH), lambda i: (i, xq_blk)),
            pl.BlockSpec((tm, BRANCH), lambda i: (i, gate_blk)),
            pl.BlockSpec((MEM_LEN, 2 * XA_WIDTH), lambda i: (i // per_seq, 0)),
            _resident((BRANCH, D_MODEL), lambda i: (0, 0)),
            pl.BlockSpec((tm, D_MODEL), lambda i: (i, 0)),
            pl.BlockSpec((1, D_MODEL), lambda i: (0, 0)),
        ],
        out_specs=pl.BlockSpec((tm, D_MODEL), lambda i: (i, 0)),
        out_shape=jax.ShapeDtypeStruct((T, D_MODEL), jnp.float32),
        compiler_params=_params("parallel"),
        name="xattn_gate_outproj",
    )(*mixes, u, u, kv, w_out, x, g_post.reshape(1, D_MODEL))


def _lru_kernel(cur_f, prev_f, next_f, cur_r, prev_r, next_r, cw_ref, cb_ref, wg_ref, gb_ref,
                lam_ref, hf_ref, hr_ref, a_f, b_f, a_r, b_r, carry, *, rows, pitch):
    s = pl.program_id(2)
    n = pl.num_programs(2)
    win = rows + 2 * LRU_HALO

    @pl.when(s == 0)
    def _():
        carry[...] = jnp.zeros_like(carry)

    def gates(cur, prev, nxt, chunk, d, a_s, b_s):
        p = prev[...].astype(jnp.float32) * (chunk > 0).astype(jnp.float32)
        q = nxt[...].astype(jnp.float32) * (chunk < n - 1).astype(jnp.float32)
        w = jnp.concatenate([p, cur[...].astype(jnp.float32), q], axis=0)
        taps = (pltpu.roll(w, 2, 0), pltpu.roll(w, 1, 0), w, pltpu.roll(w, win - 1, 0))
        u = cb_ref[...]
        for k in range(CONV_W):
            u = u + taps[k][LRU_HALO:LRU_HALO + rows] * cw_ref[k:k + 1, :]
        lam = lam_ref[d]
        softplus = jnp.maximum(-lam, 0.0) + jnp.log(1.0 + jnp.exp(-jnp.abs(lam)))
        decay = -LRU_C * softplus
        for blk in range(LRU_CH // LRU_BLOCK):
            lo = blk * LRU_BLOCK
            ub = u[:, lo:lo + LRU_BLOCK]
            g = jnp.dot(ub.astype(jnp.bfloat16), wg_ref[d, blk],
                        preferred_element_type=jnp.float32) + gb_ref[d, blk]
            r = _sigmoid(g[:, :LRU_BLOCK])
            i = _sigmoid(g[:, LRU_BLOCK:])
            a = jnp.exp(decay[:, lo:lo + LRU_BLOCK] * r)
            b = jnp.sqrt(1.0 - a * a) * (i * ub)
            for half in range(LRU_BLOCK // LANES):
                base = (blk * (LRU_BLOCK // LANES) + half) * pitch
                a_s[base:base + rows, :] = a[:, half * LANES:(half + 1) * LANES]
                b_s[base:base + rows, :] = b[:, half * LANES:(half + 1) * LANES]

    gates(cur_f, prev_f, next_f, s, 0, a_f, b_f)
    gates(cur_r, prev_r, next_r, n - 1 - s, 1, a_r, b_r)

    unroll = 8

    def body(grp, hs):
        hf, hr = hs
        for j in range(unroll):
            t = grp * unroll + j
            f_idx = pl.ds(t, LRU_LANE_GROUPS, stride=pitch)
            hf = a_f[f_idx, :] * hf + b_f[f_idx, :]
            b_f[f_idx, :] = hf
            r_idx = pl.ds(rows - 1 - t, LRU_LANE_GROUPS, stride=pitch)
            hr = a_r[r_idx, :] * hr + b_r[r_idx, :]
            b_r[r_idx, :] = hr
        return hf, hr

    hf, hr = lax.fori_loop(0, rows // unroll, body, (carry[0], carry[1]))
    carry[0] = hf
    carry[1] = hr

    for grp in range(LRU_LANE_GROUPS):
        base = grp * pitch
        hf_ref[:, grp * LANES:(grp + 1) * LANES] = b_f[base:base + rows, :].astype(hf_ref.dtype)
        hr_ref[:, grp * LANES:(grp + 1) * LANES] = b_r[base:base + rows, :].astype(hr_ref.dtype)


def _lru_mixer(u, conv_w, conv_b, wg, gb, lam, *, batch, seq, rows):
    T = u.shape[0]
    rows = min(rows, seq)
    n = seq // rows
    pitch = rows + SUBLANES
    hb = rows // LRU_HALO
    last_halo = T // LRU_HALO - 1

    def cur_map(rev):
        def f(b, c, s):
            chunk = (n - 1 - s) if rev else s
            return (b * n + chunk, c)
        return f

    def prev_map(rev):
        def f(b, c, s):
            chunk = (n - 1 - s) if rev else s
            return (jnp.maximum((b * n + chunk) * hb - 1, 0), c)
        return f

    def next_map(rev):
        def f(b, c, s):
            chunk = (n - 1 - s) if rev else s
            return (jnp.minimum((b * n + chunk + 1) * hb, last_halo), c)
        return f

    blocks = LRU_CH // LRU_BLOCK
    in_specs = []
    for rev in (False, True):
        in_specs += [
            pl.BlockSpec((rows, LRU_CH), cur_map(rev)),
            pl.BlockSpec((LRU_HALO, LRU_CH), prev_map(rev)),
            pl.BlockSpec((LRU_HALO, LRU_CH), next_map(rev)),
        ]
    in_specs += [
        pl.BlockSpec((CONV_W, LRU_CH), lambda b, c, s: (0, c)),
        pl.BlockSpec((1, LRU_CH), lambda b, c, s: (0, c)),
        pl.BlockSpec((2, blocks, LRU_BLOCK, 2 * LRU_BLOCK), lambda b, c, s: (0, c, 0, 0)),
        pl.BlockSpec((2, blocks, 1, 2 * LRU_BLOCK), lambda b, c, s: (0, c, 0, 0)),
        pl.BlockSpec((2, 1, LRU_CH), lambda b, c, s: (0, 0, c)),
    ]
    slab = pltpu.VMEM((LRU_LANE_GROUPS * pitch, LANES), jnp.float32)
    out = jax.ShapeDtypeStruct((T, MIX_WIDTH), jnp.bfloat16)
    return pl.pallas_call(
        functools.partial(_lru_kernel, rows=rows, pitch=pitch),
        grid=(batch, MIX_WIDTH // LRU_CH, n),
        in_specs=in_specs,
        out_specs=[pl.BlockSpec((rows, LRU_CH), cur_map(False)),
                   pl.BlockSpec((rows, LRU_CH), cur_map(True))],
        out_shape=[out, out],
        scratch_shapes=[slab, slab, slab, slab,
                        pltpu.VMEM((2, LRU_LANE_GROUPS, LANES), jnp.float32)],
        compiler_params=_params("parallel", "parallel", "arbitrary"),
        name="rglru",
    )(u, u, u, u, u, u, conv_w, conv_b.reshape(1, MIX_WIDTH), wg, gb,
      lam.reshape(2, 1, MIX_WIDTH))


def _mla_q_kernel(cq_ref, g_ref, wa_ref, wb_ref, ct_ref, st_ref, o_ref, h_ref, *, heads):
    @pl.when(pl.program_id(1) == 0)
    def _():
        x = cq_ref[...].astype(jnp.float32)
        ms = jnp.mean(x * x, axis=-1, keepdims=True)
        h_ref[...] = ((x * lax.rsqrt(ms + EPS)) * g_ref[...]).astype(h_ref.dtype)

    h = h_ref[...]
    qa = jnp.dot(h, wa_ref[...], preferred_element_type=jnp.float32)
    qb = jnp.dot(h, wb_ref[...], preferred_element_type=jnp.float32)
    scale = (QK_NOPE + QK_ROPE) ** -0.5 * math.log2(math.e)
    ct = ct_ref[...] * scale
    st = st_ref[...] * scale
    for hh in range(heads):
        lo = hh * QK_PAD
        o_ref[:, lo:lo + LANES] = (qa[:, lo:lo + LANES] * scale).astype(o_ref.dtype)
        rope = qa[:, lo + LANES:lo + 2 * LANES] * ct + qb[:, hh * LANES:(hh + 1) * LANES] * st
        o_ref[:, lo + LANES:lo + 2 * LANES] = rope.astype(o_ref.dtype)


def _mla_q(u, q_norm, wa, wb, ct, st, *, seq, tm, heads):
    T = u.shape[0]
    tm = min(tm, seq)
    per_seq = seq // tm
    cq_blk = B_CQ_OFF // Q_LORA
    return pl.pallas_call(
        functools.partial(_mla_q_kernel, heads=heads),
        grid=(T // tm, MLA_HEADS // heads),
        in_specs=[
            pl.BlockSpec((tm, Q_LORA), lambda i, j: (i, cq_blk)),
            pl.BlockSpec((1, Q_LORA), lambda i, j: (0, 0)),
            pl.BlockSpec((Q_LORA, heads * QK_PAD), lambda i, j: (0, j)),
            pl.BlockSpec((Q_LORA, heads * LANES), lambda i, j: (0, j)),
            pl.BlockSpec((tm, LANES), lambda i, j: (i % per_seq, 0)),
            pl.BlockSpec((tm, LANES), lambda i, j: (i % per_seq, 0)),
        ],
        out_specs=pl.BlockSpec((tm, heads * QK_PAD), lambda i, j: (i, j)),
        out_shape=jax.ShapeDtypeStruct((T, MLA_HEADS * QK_PAD), jnp.bfloat16),
        scratch_shapes=[pltpu.VMEM((tm, Q_LORA), jnp.bfloat16)],
        compiler_params=_params("parallel", "arbitrary"),
        name="mla_q_proj",
    )(u, q_norm.reshape(1, Q_LORA), wa, wb, ct, st)


def _mla_kv_kernel(ckv_ref, g_ref, wk_ref, wvt_ref, kr_ref, tab_ref, k_ref, vt_ref):
    x = ckv_ref[...].astype(jnp.float32)
    ms = jnp.mean(x * x, axis=-1, keepdims=True)
    h = ((x * lax.rsqrt(ms + EPS)) * g_ref[...]).astype(jnp.bfloat16)
    kn = jnp.dot(h, wk_ref[...], preferred_element_type=jnp.float32)
    vt_ref[...] = lax.dot_general(wvt_ref[...], h, (((1,), (1,)), ((), ())),
                                  preferred_element_type=jnp.float32).astype(vt_ref.dtype)
    t = kr_ref[...].astype(jnp.float32) * tab_ref[...]
    rot = t + pltpu.roll(t, QK_ROPE, 1)
    lane = lax.broadcasted_iota(jnp.int32, rot.shape, 1)
    rot = jnp.where(lane < QK_ROPE, rot, 0.0).astype(k_ref.dtype)
    for hh in range(MLA_HEADS):
        k_ref[:, hh * QK_PAD:hh * QK_PAD + LANES] = kn[:, hh * LANES:(hh + 1) * LANES].astype(k_ref.dtype)
        k_ref[:, hh * QK_PAD + LANES:(hh + 1) * QK_PAD] = rot


def _mla_kv(u, kv_norm, wk, wvt, tab, *, seq, tm):
    T = u.shape[0]
    tm = min(tm, seq)
    per_seq = seq // tm
    ckv_blk = B_CKV_OFF // KV_LORA
    kr_blk = B_KR_OFF // LANES
    return pl.pallas_call(
        _mla_kv_kernel,
        grid=(T // tm,),
        in_specs=[
            pl.BlockSpec((tm, KV_LORA), lambda i: (i, ckv_blk)),
            pl.BlockSpec((1, KV_LORA), lambda i: (0, 0)),
            pl.BlockSpec((KV_LORA, MLA_HEADS * QK_NOPE), lambda i: (0, 0)),
            pl.BlockSpec((MLA_HEADS * V_DIM, KV_LORA), lambda i: (0, 0)),
            pl.BlockSpec((tm, LANES), lambda i: (i, kr_blk)),
            pl.BlockSpec((tm, LANES), lambda i: (i % per_seq, 0)),
        ],
        out_specs=[pl.BlockSpec((tm, MLA_HEADS * QK_PAD), lambda i: (i, 0)),
                   pl.BlockSpec((MLA_HEADS * V_DIM, tm), lambda i: (0, i))],
        out_shape=[jax.ShapeDtypeStruct((T, MLA_HEADS * QK_PAD), jnp.bfloat16),
                   jax.ShapeDtypeStruct((MLA_HEADS * V_DIM, T), jnp.bfloat16)],
        compiler_params=_params("parallel"),
        name="mla_kv_proj",
    )(u, kv_norm.reshape(1, KV_LORA), wk, wvt, u, tab)


def _flash_kernel(q_ref, k_ref, vt_ref, o_ref, s_buf, acc_ref, *, tk, n_kv):
    q = q_ref[...]

    def scores(j, slot):
        k = k_ref[j * tk:(j + 1) * tk, :]
        s = lax.dot_general(k, q, (((1,), (1,)), ((), ())), preferred_element_type=jnp.float32)
        s_buf[slot] = s
        return jnp.max(s, axis=0, keepdims=True)

    chunk_max = scores(0, 0)
    m_run = l = None
    for j in range(n_kv):
        slot = j % 2
        next_max = scores(j + 1, 1 - slot) if j + 1 < n_kv else None
        m_new = chunk_max if j == 0 else jnp.maximum(m_run, chunk_max)
        p = jnp.exp2(s_buf[slot] - m_new)
        p_sum = jnp.sum(p, axis=0, keepdims=True)
        pv = jnp.dot(vt_ref[:, j * tk:(j + 1) * tk], p.astype(jnp.bfloat16),
                     preferred_element_type=jnp.float32)
        if j == 0:
            l = p_sum
            acc_ref[...] = pv
        else:
            alpha = jnp.exp2(m_run - m_new)
            l = alpha * l + p_sum
            acc_ref[...] = alpha * acc_ref[...] + pv
        m_run, chunk_max = m_new, next_max
    o_ref[...] = (acc_ref[...] / l).T.astype(o_ref.dtype)


def _flash_attention(q, k, vt, *, batch, seq, tq, tk):
    T = q.shape[0]
    tq = min(tq, seq)
    tk = min(tk, seq)
    nq = seq // tq
    return pl.pallas_call(
        functools.partial(_flash_kernel, tk=tk, n_kv=seq // tk),
        grid=(batch, MLA_HEADS, nq),
        in_specs=[
            pl.BlockSpec((tq, QK_PAD), lambda b, h, i: (b * nq + i, h)),
            pl.BlockSpec((seq, QK_PAD), lambda b, h, i: (b, h)),
            pl.BlockSpec((V_DIM, seq), lambda b, h, i: (h, b)),
        ],
        out_specs=pl.BlockSpec((tq, V_DIM), lambda b, h, i: (b * nq + i, h)),
        out_shape=jax.ShapeDtypeStruct((T, MLA_HEADS * V_DIM), jnp.bfloat16),
        scratch_shapes=[pltpu.VMEM((2, tk, tq), jnp.float32),
                        pltpu.VMEM((V_DIM, tq), jnp.float32)],
        compiler_params=_params("parallel", "parallel", "arbitrary"),
        name="mla_flash_attention",
    )(q, k, vt)


def _pool_kernel(cur_ref, prev_ref, next_ref, w_ref, sc_ref, o_ref, *, seq, tm):
    i = pl.program_id(1)
    n = pl.num_programs(1)
    win = tm + 2 * POOL_HALO
    keep_prev = (i > 0).astype(jnp.float32)
    keep_next = (i < n - 1).astype(jnp.float32)
    t = i * tm + lax.broadcasted_iota(jnp.int32, (tm, 1), 0)
    for g, w in enumerate(POOL_WINDOWS):
        lo = g * POOL_GROUP
        x = cur_ref[:, lo:lo + POOL_GROUP].astype(jnp.float32)
        p = prev_ref[:, lo:lo + POOL_GROUP].astype(jnp.float32) * keep_prev
        q = next_ref[:, lo:lo + POOL_GROUP].astype(jnp.float32) * keep_next
        run = jnp.concatenate([p, x, q], axis=0)
        span = 1
        while span < w:
            run = run + pltpu.roll(run, span, 0)
            span *= 2
        ahead = w - w // 2 - 1
        if ahead:
            run = pltpu.roll(run, win - ahead, 0)
        total = run[POOL_HALO:POOL_HALO + tm]
        count = jnp.minimum(t + (w - w // 2), seq) - jnp.maximum(t - w // 2, 0)
        pooled = total / count.astype(jnp.float32) - x
        y = jnp.dot(pooled.astype(jnp.bfloat16), w_ref[g], preferred_element_type=jnp.float32)
        o_ref[:, lo:lo + POOL_GROUP] = (y * sc_ref[:, lo:lo + POOL_GROUP]).astype(o_ref.dtype)


def _pool_mixer(u, w_group, scale, *, batch, seq, tm):
    T = u.shape[0]
    tm = min(tm, seq)
    n = seq // tm
    hb = tm // POOL_HALO
    last_halo = T // POOL_HALO - 1
    return pl.pallas_call(
        functools.partial(_pool_kernel, seq=seq, tm=tm),
        grid=(batch, n),
        in_specs=[
            pl.BlockSpec((tm, MIX_WIDTH), lambda b, i: (b * n + i, 0)),
            pl.BlockSpec((POOL_HALO, MIX_WIDTH),
                         lambda b, i: (jnp.maximum((b * n + i) * hb - 1, 0), 0)),
            pl.BlockSpec((POOL_HALO, MIX_WIDTH),
                         lambda b, i: (jnp.minimum((b * n + i + 1) * hb, last_halo), 0)),
            pl.BlockSpec((len(POOL_WINDOWS), POOL_GROUP, POOL_GROUP), lambda b, i: (0, 0, 0)),
            pl.BlockSpec((1, MIX_WIDTH), lambda b, i: (0, 0)),
        ],
        out_specs=pl.BlockSpec((tm, MIX_WIDTH), lambda b, i: (b * n + i, 0)),
        out_shape=jax.ShapeDtypeStruct((T, MIX_WIDTH), jnp.bfloat16),
        compiler_params=_params("parallel", "parallel"),
        name="pool_mixer",
    )(u, u, u, w_group, scale.reshape(1, MIX_WIDTH))


def _rope_tables(seq):
    inv_freq = 1.0 / (ROPE_THETA ** (jnp.arange(0, QK_ROPE, 2, dtype=jnp.float32) / QK_ROPE))
    ang = jnp.arange(seq, dtype=jnp.float32)[:, None] * inv_freq[None, :]
    cos, sin = jnp.cos(ang), jnp.sin(ang)
    zeros = jnp.zeros((seq, LANES - QK_ROPE), jnp.float32)
    ct = jnp.concatenate([cos, cos, zeros], axis=-1)
    st = jnp.concatenate([-sin, sin, zeros], axis=-1)
    tab = jnp.concatenate([cos, cos, -sin, sin], axis=-1)
    return ct, st, tab


def _swap_halves(w):
    half = QK_ROPE // 2
    return jnp.concatenate([w[..., half:], w[..., :half]], axis=-1)


def _prep_b(b_w_in, b_w_q_up, b_w_kv_up):
    bf = jnp.bfloat16
    K = b_w_in.shape[0]
    o1 = Q_LORA
    o2 = o1 + KV_LORA
    o3 = o2 + QK_ROPE
    o4 = o3 + XA_WIDTH
    w_cq, w_ckv, w_kr, w_xq, w_gate = (b_w_in[:, :o1], b_w_in[:, o1:o2], b_w_in[:, o2:o3],
                                       b_w_in[:, o3:o4], b_w_in[:, o4:])
    w_in = jnp.concatenate([w_gate, w_xq, w_cq, w_ckv, w_kr, _swap_halves(w_kr),
                            jnp.zeros((K, LANES), b_w_in.dtype)], axis=-1).astype(bf)
    wq = b_w_q_up.reshape(Q_LORA, MLA_HEADS, QK_NOPE + QK_ROPE)
    q_nope, q_rope = wq[..., :QK_NOPE], wq[..., QK_NOPE:]
    pad = jnp.zeros((Q_LORA, MLA_HEADS, LANES - QK_ROPE), wq.dtype)
    wa = jnp.concatenate([q_nope, q_rope, pad], axis=-1).reshape(Q_LORA, MLA_HEADS * QK_PAD).astype(bf)
    wb = jnp.concatenate([_swap_halves(q_rope), pad], axis=-1).reshape(Q_LORA, MLA_HEADS * LANES).astype(bf)
    wkv = b_w_kv_up.reshape(KV_LORA, MLA_HEADS, QK_NOPE + V_DIM)
    wk = wkv[..., :QK_NOPE].reshape(KV_LORA, MLA_HEADS * QK_NOPE).astype(bf)
    wvt = wkv[..., QK_NOPE:].reshape(KV_LORA, MLA_HEADS * V_DIM).T.astype(bf)
    return w_in, wa, wb, wk, wvt


def _prep_lru(gate_w, gate_b):
    wg = jnp.concatenate([gate_w[:, 0], gate_w[:, 1]], axis=-1).astype(jnp.bfloat16)
    gb = gate_b.reshape(2, 2, LRU_BLOCKS, 1, LRU_BLOCK)
    gb = jnp.concatenate([gb[:, 0], gb[:, 1]], axis=-1)
    return wg, gb


def _trunk(x, mem, w):
    batch, seq, _ = x.shape
    T = batch * seq
    x = x.reshape(T, D_MODEL)
    mem = mem.reshape(batch * MEM_LEN, D_MODEL)
    ct, st, tab = _rope_tables(seq)
    depth = w["norm_pre"].shape[0]
    for i in range(depth):
        kind, j = i % N_MIXERS, i // N_MIXERS
        kv = _norm_matmul(mem, w["norm_mem"][i], w["w_mem_kv"][i], tm=512, tn=1024)
        if kind == 0:
            u = _norm_matmul(x, w["norm_pre"][i], w["a_w_in"][j], tm=1024, tn=1024)
            mixes = _lru_mixer(u, w["a_conv_w"][j], w["a_conv_b"][j], w["a_wg"][j], w["a_gb"][j],
                               w["a_lambda"][j], batch=batch, seq=seq, rows=256)
            xq_off, gate_off = MIX_WIDTH, BRANCH
        elif kind == 1:
            u = _norm_matmul(x, w["norm_pre"][i], w["b_w_in"][j], tm=1024, tn=1024)
            q = _mla_q(u, w["b_q_norm"][j], w["b_wa"][j], w["b_wb"][j], ct, st,
                       seq=seq, tm=512, heads=4)
            k, vt = _mla_kv(u, w["b_kv_norm"][j], w["b_wk"][j], w["b_wvt"][j], tab, seq=seq, tm=512)
            mixes = [_flash_attention(q, k, vt, batch=batch, seq=seq, tq=512, tk=512)]
            xq_off, gate_off = B_XQ_OFF, B_GATE_OFF
        else:
            u = _norm_matmul(x, w["norm_pre"][i], w["c_w_in"][j], tm=1024, tn=1024)
            mixes = [_pool_mixer(u, w["c_w_group"][j], w["c_scale"][j], batch=batch, seq=seq, tm=512)]
            xq_off, gate_off = MIX_WIDTH, BRANCH
        x = _out_block(mixes, u, xq_off, gate_off, kv, w["w_out"][i], x, w["norm_post"][i],
                       seq=seq, tm=256)
    return x.reshape(batch, seq, D_MODEL)


def kernel(x_prompt, x_sample, mem_prompt, mem_sample, norm_pre, norm_post, norm_mem, w_mem_kv, w_out,
           a_w_in, a_conv_w, a_conv_b, a_gate_w, a_gate_b, a_lambda,
           b_w_in, b_q_norm, b_kv_norm, b_w_q_up, b_w_kv_up,
           c_w_in, c_w_group, c_scale):
    bf = jnp.bfloat16
    b_parts = [_prep_b(b_w_in[j], b_w_q_up[j], b_w_kv_up[j]) for j in range(b_w_in.shape[0])]
    lru_parts = [_prep_lru(a_gate_w[j], a_gate_b[j]) for j in range(a_gate_w.shape[0])]
    w = dict(
        norm_pre=norm_pre, norm_post=norm_post, norm_mem=norm_mem,
        w_mem_kv=w_mem_kv.astype(bf), w_out=w_out.astype(bf),
        a_w_in=a_w_in.astype(bf), a_conv_w=a_conv_w, a_conv_b=a_conv_b,
        a_wg=[p[0] for p in lru_parts], a_gb=[p[1] for p in lru_parts], a_lambda=a_lambda,
        b_w_in=[p[0] for p in b_parts], b_q_norm=b_q_norm, b_kv_norm=b_kv_norm,
        b_wa=[p[1] for p in b_parts], b_wb=[p[2] for p in b_parts],
        b_wk=[p[3] for p in b_parts], b_wvt=[p[4] for p in b_parts],
        c_w_in=c_w_in.astype(bf), c_w_group=c_w_group.astype(bf), c_scale=c_scale,
    )
    return (_trunk(x_prompt, mem_prompt, w), _trunk(x_sample, mem_sample, w))
```

```python
import functools
import math

import jax
import jax.numpy as jnp
from jax import lax
from jax.experimental import pallas as pl
from jax.experimental.pallas import tpu as pltpu

D_MODEL = 2048
MIX_WIDTH = 3 * D_MODEL // 2
XA_HEADS = 4
XA_HEAD_DIM = D_MODEL // 8
XA_WIDTH = XA_HEADS * XA_HEAD_DIM
BRANCH = MIX_WIDTH + XA_WIDTH
MEM_LEN = 256
N_MIXERS = 3
CONV_W = 4
CONV_LEFT = 2
LRU_BLOCKS = 12
LRU_BLOCK = MIX_WIDTH // LRU_BLOCKS
LRU_C = 8.0
QK_NOPE = 128
QK_ROPE = 64
V_DIM = 128
MLA_HEADS = MIX_WIDTH // V_DIM
Q_LORA = D_MODEL // 4
KV_LORA = D_MODEL // 8
ROPE_THETA = 10000.0
POOL_WINDOWS = (2, 4, 8, 16)
POOL_GROUP = MIX_WIDTH // len(POOL_WINDOWS)
EPS = 1e-6

LANES = 128
SUBLANES = 8
BF16_ROWS = 16
QK_PAD = 256
AUG_ROW = QK_NOPE + QK_ROPE
FAST_PATH_LIMIT = 64.0
VT_ROWS = V_DIM + BF16_ROWS
VMEM_LIMIT = 56 * 1024 * 1024

B_GATE_OFF = 0
B_XQ_OFF = BRANCH
B_CQ_OFF = B_XQ_OFF + XA_WIDTH
B_CKV_OFF = B_CQ_OFF + Q_LORA
B_KR_OFF = B_CKV_OFF + KV_LORA
B_IN_WIDTH = B_KR_OFF + 2 * QK_ROPE + LANES

LRU_CH = 1024
LRU_LANE_GROUPS = LRU_CH // LANES
LRU_HALO = BF16_ROWS
LRU_STEP = 4
RSQRT_FLOOR = 1e-30

POOL_HALO = BF16_ROWS


def _params(*sem, flags=None):
    return pltpu.CompilerParams(dimension_semantics=sem, vmem_limit_bytes=VMEM_LIMIT, flags=flags)


def _resident(block_shape, index_map):
    return pl.BlockSpec(block_shape, index_map, pipeline_mode=pl.Buffered(1))


def _sigmoid(x):
    return 1.0 / (1.0 + jnp.exp(-x))


def _norm_matmul_kernel(x_ref, g_ref, w_ref, o_ref, h_ref):
    @pl.when(pl.program_id(1) == 0)
    def _():
        x = x_ref[...].astype(jnp.float32)
        ms = jnp.mean(x * x, axis=-1, keepdims=True)
        h_ref[...] = ((x * lax.rsqrt(ms + EPS)) * g_ref[...]).astype(h_ref.dtype)

    o_ref[...] = jnp.dot(h_ref[...], w_ref[...],
                         preferred_element_type=jnp.float32).astype(o_ref.dtype)


def _norm_matmul(x, g, w, *, tm, tn):
    T, K = x.shape
    N = w.shape[1]
    tm = min(tm, T)
    return pl.pallas_call(
        _norm_matmul_kernel,
        grid=(T // tm, N // tn),
        in_specs=[
            pl.BlockSpec((tm, K), lambda i, j: (i, 0)),
            pl.BlockSpec((1, K), lambda i, j: (0, 0)),
            pl.BlockSpec((K, tn), lambda i, j: (0, j)),
        ],
        out_specs=pl.BlockSpec((tm, tn), lambda i, j: (i, j)),
        out_shape=jax.ShapeDtypeStruct((T, N), jnp.bfloat16),
        scratch_shapes=[pltpu.VMEM((tm, K), jnp.bfloat16)],
        compiler_params=_params("parallel", "arbitrary"),
        name="norm_matmul",
    )(x, g.reshape(1, K), w)


def _out_kernel(*refs, n_mix):
    mix_refs = refs[:n_mix]
    xq_ref, gate_ref, kv_ref, w_ref, x_ref, g_ref, o_ref = refs[n_mix:]

    mix = mix_refs[0][...].astype(jnp.float32)
    for r in mix_refs[1:]:
        mix = mix + r[...].astype(jnp.float32)

    xq = xq_ref[...]
    scale = XA_HEAD_DIM ** -0.5
    xa = []
    for h in range(XA_HEADS):
        lo = h * XA_HEAD_DIM
        q = xq[:, lo:lo + XA_HEAD_DIM]
        k = kv_ref[:, lo:lo + XA_HEAD_DIM]
        v = kv_ref[:, XA_WIDTH + lo:XA_WIDTH + lo + XA_HEAD_DIM]
        s = lax.dot_general(q, k, (((1,), (1,)), ((), ())),
                            preferred_element_type=jnp.float32) * scale
        m = jnp.max(s, axis=-1, keepdims=True)
        p = jnp.exp(s - m)
        l = jnp.sum(p, axis=-1, keepdims=True)
        p = (p / l).astype(jnp.bfloat16)
        xa.append(jnp.dot(p, v, preferred_element_type=jnp.float32))

    gate = gate_ref[...].astype(jnp.float32)
    act = gate * _sigmoid(gate)
    y_mix = (mix * act[:, :MIX_WIDTH]).astype(jnp.bfloat16)
    acc = jnp.dot(y_mix, w_ref[:MIX_WIDTH, :], preferred_element_type=jnp.float32)
    for h in range(XA_HEADS):
        lo = MIX_WIDTH + h * XA_HEAD_DIM
        y_h = (xa[h] * act[:, lo:lo + XA_HEAD_DIM]).astype(jnp.bfloat16)
        acc = acc + jnp.dot(y_h, w_ref[lo:lo + XA_HEAD_DIM, :],
                            preferred_element_type=jnp.float32)

    ms = jnp.mean(acc * acc, axis=-1, keepdims=True)
    o_ref[...] = x_ref[...] + (acc * lax.rsqrt(ms + EPS)) * g_ref[...]


def _out_block(mixes, u, xq_off, gate_off, kv, w_out, x, g_post, *, seq, tm):
    T = x.shape[0]
    tm = min(tm, seq)
    per_seq = seq // tm
    xq_blk = xq_off // XA_WIDTH
    gate_blk = gate_off // BRANCH
    mix_spec = pl.BlockSpec((tm, MIX_WIDTH), lambda i: (i, 0))
    return pl.pallas_call(
        functools.partial(_out_kernel, n_mix=len(mixes)),
        grid=(T // tm,),
        in_specs=[mix_spec] * len(mixes) + [
            pl.BlockSpec((tm, XA_WIDTH), lambda i: (i, xq_blk)),
            pl.BlockSpec((tm, BRANCH), lambda i: (i, gate_blk)),
            pl.BlockSpec((MEM_LEN, 2 * XA_WIDTH), lambda i: (i // per_seq, 0)),
            _resident((BRANCH, D_MODEL), lambda i: (0, 0)),
            pl.BlockSpec((tm, D_MODEL), lambda i: (i, 0)),
            pl.BlockSpec((1, D_MODEL), lambda i: (0, 0)),
        ],
        out_specs=pl.BlockSpec((tm, D_MODEL), lambda i: (i, 0)),
        out_shape=jax.ShapeDtypeStruct((T, D_MODEL), jnp.float32),
        compiler_params=_params("parallel"),
        name="xattn_gate_outproj",
    )(*mixes, u, u, kv, w_out, x, g_post.reshape(1, D_MODEL))


def _lru_kernel(cur_f, prev_f, next_f, cur_r, prev_r, next_r, cw_ref, cb_ref, wg_ref, gb_ref,
                lam_ref, hf_ref, hr_ref, a_f, b_f, h_f, a_r, b_r, h_r, carry, *, rows, pitch):
    s = pl.program_id(2)
    n = pl.num_programs(2)

    @pl.when(s == 0)
    def _():
        carry[...] = jnp.zeros_like(carry)

    win = rows + 2 * LRU_HALO

    def gates(cur, prev, nxt, chunk, d, a_s, b_s):
        p = prev[...].astype(jnp.float32) * (chunk > 0).astype(jnp.float32)
        q = nxt[...].astype(jnp.float32) * (chunk < n - 1).astype(jnp.float32)
        w = jnp.concatenate([p, cur[...].astype(jnp.float32), q], axis=0)
        taps = (pltpu.roll(w, 2, 0), pltpu.roll(w, 1, 0), w, pltpu.roll(w, win - 1, 0))
        u = cb_ref[...]
        for k in range(CONV_W):
            u = u + taps[k][LRU_HALO:LRU_HALO + rows] * cw_ref[k:k + 1, :]
        lam = lam_ref[d]
        softplus = jnp.maximum(-lam, 0.0) + jnp.log(1.0 + jnp.exp(-jnp.abs(lam)))
        half_decay = (-0.5 * LRU_C * math.log2(math.e)) * softplus
        for blk in range(LRU_CH // LRU_BLOCK):
            lo = blk * LRU_BLOCK
            ub = u[:, lo:lo + LRU_BLOCK]
            g = jnp.dot(ub.astype(jnp.bfloat16), wg_ref[d, blk],
                        preferred_element_type=jnp.float32) + gb_ref[d, blk]
            hd = half_decay[:, lo:lo + LRU_BLOCK]
            a = jnp.exp2(hd * jnp.tanh(g[:, :LRU_BLOCK]) + hd)
            half_u = 0.5 * ub
            iu = half_u * jnp.tanh(g[:, LRU_BLOCK:]) + half_u
            y = 1.0 - a * a
            b = (y * lax.rsqrt(jnp.maximum(y, RSQRT_FLOOR))) * iu
            for half in range(LRU_BLOCK // LANES):
                base = (blk * (LRU_BLOCK // LANES) + half) * pitch
                a_s[base:base + rows, :] = a[:, half * LANES:(half + 1) * LANES]
                b_s[base:base + rows, :] = b[:, half * LANES:(half + 1) * LANES]

    gates(cur_f, prev_f, next_f, s, 0, a_f, b_f)
    gates(cur_r, prev_r, next_r, n - 1 - s, 1, a_r, b_r)

    def block(a_s, b_s, h_s, h, t0, sign):
        idx = [pl.ds(t0 + sign * j, LRU_LANE_GROUPS, stride=pitch) for j in range(LRU_STEP)]
        a = [a_s[i, :] for i in idx]
        b = [b_s[i, :] for i in idx]
        a01, b01 = a[1] * a[0], a[1] * b[0] + b[1]
        a23, b23 = a[3] * a[2], a[3] * b[2] + b[3]
        a03, b03 = a23 * a01, a23 * b01 + b23
        h0 = a[0] * h + b[0]
        h1 = a01 * h + b01
        h3 = a03 * h + b03
        h2 = a[2] * h1 + b[2]
        for i, v in zip(idx, (h0, h1, h2, h3)):
            h_s[i, :] = v
        return h3

    blocks_per_iter = 8

    def body(it, hs):
        hf, hr = hs
        for j in range(blocks_per_iter):
            t = (it * blocks_per_iter + j) * LRU_STEP
            hf = block(a_f, b_f, h_f, hf, t, 1)
            hr = block(a_r, b_r, h_r, hr, rows - 1 - t, -1)
        return hf, hr

    hf, hr = lax.fori_loop(0, rows // (LRU_STEP * blocks_per_iter), body, (carry[0], carry[1]))
    carry[0] = hf
    carry[1] = hr

    for grp in range(LRU_LANE_GROUPS):
        base = grp * pitch
        hf_ref[:, grp * LANES:(grp + 1) * LANES] = h_f[base:base + rows, :].astype(hf_ref.dtype)
        hr_ref[:, grp * LANES:(grp + 1) * LANES] = h_r[base:base + rows, :].astype(hr_ref.dtype)


def _lru_mixer(u, conv_w, conv_b, wg, gb, lam, *, batch, seq, rows):
    T = u.shape[0]
    rows = min(rows, seq)
    n = seq // rows
    pitch = rows + SUBLANES
    hb = rows // LRU_HALO
    last_halo = T // LRU_HALO - 1

    def cur_map(rev):
        def f(b, c, s):
            chunk = (n - 1 - s) if rev else s
            return (b * n + chunk, c)
        return f

    def prev_map(rev):
        def f(b, c, s):
            chunk = (n - 1 - s) if rev else s
            return (jnp.maximum((b * n + chunk) * hb - 1, 0), c)
        return f

    def next_map(rev):
        def f(b, c, s):
            chunk = (n - 1 - s) if rev else s
            return (jnp.minimum((b * n + chunk + 1) * hb, last_halo), c)
        return f

    blocks = LRU_CH // LRU_BLOCK
    in_specs = []
    for rev in (False, True):
        in_specs += [
            pl.BlockSpec((rows, LRU_CH), cur_map(rev)),
            pl.BlockSpec((LRU_HALO, LRU_CH), prev_map(rev)),
            pl.BlockSpec((LRU_HALO, LRU_CH), next_map(rev)),
        ]
    in_specs += [
        pl.BlockSpec((CONV_W, LRU_CH), lambda b, c, s: (0, c)),
        pl.BlockSpec((1, LRU_CH), lambda b, c, s: (0, c)),
        pl.BlockSpec((2, blocks, LRU_BLOCK, 2 * LRU_BLOCK), lambda b, c, s: (0, c, 0, 0)),
        pl.BlockSpec((2, blocks, 1, 2 * LRU_BLOCK), lambda b, c, s: (0, c, 0, 0)),
        pl.BlockSpec((2, 1, LRU_CH), lambda b, c, s: (0, 0, c)),
    ]
    slab = pltpu.VMEM((LRU_LANE_GROUPS * pitch, LANES), jnp.float32)
    out = jax.ShapeDtypeStruct((T, MIX_WIDTH), jnp.bfloat16)
    return pl.pallas_call(
        functools.partial(_lru_kernel, rows=rows, pitch=pitch),
        grid=(batch, MIX_WIDTH // LRU_CH, n),
        in_specs=in_specs,
        out_specs=[pl.BlockSpec((rows, LRU_CH), cur_map(False)),
                   pl.BlockSpec((rows, LRU_CH), cur_map(True))],
        out_shape=[out, out],
        scratch_shapes=[slab, slab, slab, slab, slab, slab,
                        pltpu.VMEM((2, LRU_LANE_GROUPS, LANES), jnp.float32)],
        compiler_params=_params("parallel", "parallel", "arbitrary"),
        name="rglru",
    )(u, u, u, u, u, u, conv_w, conv_b.reshape(1, MIX_WIDTH), wg, gb,
      lam.reshape(2, 1, MIX_WIDTH))


def _mla_q_kernel(cq_ref, g_ref, wa_ref, wb_ref, ct_ref, st_ref, o_ref, h_ref, *, heads):
    @pl.when(pl.program_id(1) == 0)
    def _():
        x = cq_ref[...].astype(jnp.float32)
        ms = jnp.mean(x * x, axis=-1, keepdims=True)
        h_ref[...] = ((x * lax.rsqrt(ms + EPS)) * g_ref[...]).astype(h_ref.dtype)

    h = h_ref[...]
    qa = jnp.dot(h, wa_ref[...], preferred_element_type=jnp.float32)
    qb = jnp.dot(h, wb_ref[...], preferred_element_type=jnp.float32)
    scale = (QK_NOPE + QK_ROPE) ** -0.5 * math.log2(math.e)
    ct = ct_ref[...] * scale
    st = st_ref[...] * scale
    for hh in range(heads):
        lo = hh * QK_PAD
        o_ref[:, lo:lo + LANES] = (qa[:, lo:lo + LANES] * scale).astype(o_ref.dtype)
        rope = qa[:, lo + LANES:lo + 2 * LANES] * ct + qb[:, hh * LANES:(hh + 1) * LANES] * st
        o_ref[:, lo + LANES:lo + 2 * LANES] = rope.astype(o_ref.dtype)


def _mla_q(u, q_norm, wa, wb, ct, st, *, seq, tm, heads):
    T = u.shape[0]
    tm = min(tm, seq)
    per_seq = seq // tm
    cq_blk = B_CQ_OFF // Q_LORA
    return pl.pallas_call(
        functools.partial(_mla_q_kernel, heads=heads),
        grid=(T // tm, MLA_HEADS // heads),
        in_specs=[
            pl.BlockSpec((tm, Q_LORA), lambda i, j: (i, cq_blk)),
            pl.BlockSpec((1, Q_LORA), lambda i, j: (0, 0)),
            pl.BlockSpec((Q_LORA, heads * QK_PAD), lambda i, j: (0, j)),
            pl.BlockSpec((Q_LORA, heads * LANES), lambda i, j: (0, j)),
            pl.BlockSpec((tm, LANES), lambda i, j: (i % per_seq, 0)),
            pl.BlockSpec((tm, LANES), lambda i, j: (i % per_seq, 0)),
        ],
        out_specs=pl.BlockSpec((tm, heads * QK_PAD), lambda i, j: (i, j)),
        out_shape=jax.ShapeDtypeStruct((T, MLA_HEADS * QK_PAD), jnp.bfloat16),
        scratch_shapes=[pltpu.VMEM((tm, Q_LORA), jnp.bfloat16)],
        compiler_params=_params("parallel", "arbitrary"),
        name="mla_q_proj",
    )(u, q_norm.reshape(1, Q_LORA), wa, wb, ct, st)


def _mla_kv_kernel(ckv_ref, g_ref, wk_ref, wvt_ref, kr_ref, tab_ref, k_ref, vt_ref):
    x = ckv_ref[...].astype(jnp.float32)
    ms = jnp.mean(x * x, axis=-1, keepdims=True)
    h = ((x * lax.rsqrt(ms + EPS)) * g_ref[...]).astype(jnp.bfloat16)
    kn = jnp.dot(h, wk_ref[...], preferred_element_type=jnp.float32)
    vt = lax.dot_general(wvt_ref[...], h, (((1,), (1,)), ((), ())),
                         preferred_element_type=jnp.float32).astype(vt_ref.dtype)
    ones = jnp.ones((VT_ROWS - V_DIM, vt.shape[1]), vt_ref.dtype)
    for hh in range(MLA_HEADS):
        vt_ref[hh * VT_ROWS:hh * VT_ROWS + V_DIM, :] = vt[hh * V_DIM:(hh + 1) * V_DIM, :]
        vt_ref[hh * VT_ROWS + V_DIM:(hh + 1) * VT_ROWS, :] = ones
    t = kr_ref[...].astype(jnp.float32) * tab_ref[...]
    rot = t + pltpu.roll(t, QK_ROPE, 1)
    lane = lax.broadcasted_iota(jnp.int32, rot.shape, 1)
    rot = jnp.where(lane < QK_ROPE, rot, jnp.where(lane == QK_ROPE, 1.0, 0.0)).astype(k_ref.dtype)
    for hh in range(MLA_HEADS):
        k_ref[:, hh * QK_PAD:hh * QK_PAD + LANES] = kn[:, hh * LANES:(hh + 1) * LANES].astype(k_ref.dtype)
        k_ref[:, hh * QK_PAD + LANES:(hh + 1) * QK_PAD] = rot


def _mla_kv(u, kv_norm, wk, wvt, tab, *, seq, tm):
    T = u.shape[0]
    tm = min(tm, seq)
    per_seq = seq // tm
    ckv_blk = B_CKV_OFF // KV_LORA
    kr_blk = B_KR_OFF // LANES
    return pl.pallas_call(
        _mla_kv_kernel,
        grid=(T // tm,),
        in_specs=[
            pl.BlockSpec((tm, KV_LORA), lambda i: (i, ckv_blk)),
            pl.BlockSpec((1, KV_LORA), lambda i: (0, 0)),
            pl.BlockSpec((KV_LORA, MLA_HEADS * QK_NOPE), lambda i: (0, 0)),
            pl.BlockSpec((MLA_HEADS * V_DIM, KV_LORA), lambda i: (0, 0)),
            pl.BlockSpec((tm, LANES), lambda i: (i, kr_blk)),
            pl.BlockSpec((tm, LANES), lambda i: (i % per_seq, 0)),
        ],
        out_specs=[pl.BlockSpec((tm, MLA_HEADS * QK_PAD), lambda i: (i, 0)),
                   pl.BlockSpec((MLA_HEADS * VT_ROWS, tm), lambda i: (0, i))],
        out_shape=[jax.ShapeDtypeStruct((T, MLA_HEADS * QK_PAD), jnp.bfloat16),
                   jax.ShapeDtypeStruct((MLA_HEADS * VT_ROWS, T), jnp.bfloat16)],
        compiler_params=_params("parallel"),
        name="mla_kv_proj",
    )(u, kv_norm.reshape(1, KV_LORA), wk, wvt, u, tab)


def _flash_kernel(q_ref, k_ref, vt_ref, o_ref, qt_buf, p_buf, acc_ref, *, tk, n_kv):
    bf = jnp.bfloat16
    qt = q_ref[...].astype(jnp.float32).T.astype(bf)
    qt_buf[0] = qt
    qt_buf[1] = qt
    refs = {}
    m_true = worst = None
    for j in range(n_kv + 1):
        if j < n_kv:
            k = k_ref[j * tk:(j + 1) * tk, :]
            s = jnp.dot(k, qt_buf[j % 2], preferred_element_type=jnp.float32)
            cmax = jnp.max(s, axis=0, keepdims=True)
            if j == 0:
                refs[0] = cmax
                p_buf[0] = jnp.exp2((s - cmax).astype(bf))
                m_true = cmax
            else:
                p_buf[j % 2] = jnp.exp2(s.astype(bf))
                m_true = jnp.maximum(m_true, refs[j] + cmax)
                worst = cmax if worst is None else jnp.maximum(worst, cmax)
            for t in ([1, 2] if j == 0 else [j + 2]):
                if t < n_kv:
                    nref = m_true.astype(bf)
                    qt_buf[t % 2, AUG_ROW:AUG_ROW + BF16_ROWS, :] = jnp.broadcast_to(
                        -nref, (BF16_ROWS, nref.shape[1]))
                    refs[t] = nref.astype(jnp.float32)
        if j >= 1:
            pv = jnp.dot(vt_ref[:, (j - 1) * tk:j * tk], p_buf[(j - 1) % 2],
                         preferred_element_type=jnp.float32)
            if j == 1:
                acc_ref[...] = pv
            else:
                acc_ref[...] = jnp.exp2(refs[j - 2] - refs[j - 1]) * acc_ref[...] + pv

    def finish():
        out = acc_ref[:V_DIM, :] / acc_ref[V_DIM:V_DIM + 1, :]
        o_ref[...] = out.T.astype(o_ref.dtype)

    finish()
    if worst is None:
        return

    @pl.when(jnp.max(worst) > FAST_PATH_LIMIT)
    def _():
        m_run = None
        for j in range(n_kv):
            s = jnp.dot(k_ref[j * tk:(j + 1) * tk, :], qt, preferred_element_type=jnp.float32)
            cmax = jnp.max(s, axis=0, keepdims=True)
            m_new = cmax if j == 0 else jnp.maximum(m_run, cmax)
            p = jnp.exp2((s - m_new).astype(bf))
            pv = jnp.dot(vt_ref[:, j * tk:(j + 1) * tk], p, preferred_element_type=jnp.float32)
            if j == 0:
                acc_ref[...] = pv
            else:
                acc_ref[...] = jnp.exp2(m_run - m_new) * acc_ref[...] + pv
            m_run = m_new
        finish()


def _flash_attention(q, k, vt, *, batch, seq, tq, tk):
    T = q.shape[0]
    tq = min(tq, seq)
    tk = min(tk, seq)
    nq = seq // tq
    return pl.pallas_call(
        functools.partial(_flash_kernel, tk=tk, n_kv=seq // tk),
        grid=(batch, MLA_HEADS, nq),
        in_specs=[
            pl.BlockSpec((tq, QK_PAD), lambda b, h, i: (b * nq + i, h)),
            pl.BlockSpec((seq, QK_PAD), lambda b, h, i: (b, h)),
            pl.BlockSpec((VT_ROWS, seq), lambda b, h, i: (h, b)),
        ],
        out_specs=pl.BlockSpec((tq, V_DIM), lambda b, h, i: (b * nq + i, h)),
        out_shape=jax.ShapeDtypeStruct((T, MLA_HEADS * V_DIM), jnp.bfloat16),
        scratch_shapes=[pltpu.VMEM((2, QK_PAD, tq), jnp.bfloat16),
                        pltpu.VMEM((2, tk, tq), jnp.bfloat16),
                        pltpu.VMEM((VT_ROWS, tq), jnp.float32)],
        compiler_params=_params("parallel", "parallel", "arbitrary"),
        name="mla_flash_attention",
    )(q, k, vt)


def _pool_kernel(cur_ref, prev_ref, next_ref, w_ref, sc_ref, o_ref, *, seq, tm):
    i = pl.program_id(1)
    n = pl.num_programs(1)
    win = tm + 2 * POOL_HALO
    keep_prev = (i > 0).astype(jnp.float32)
    keep_next = (i < n - 1).astype(jnp.float32)
    t = i * tm + lax.broadcasted_iota(jnp.int32, (tm, 1), 0)
    for g, w in enumerate(POOL_WINDOWS):
        lo = g * POOL_GROUP
        x = cur_ref[:, lo:lo + POOL_GROUP].astype(jnp.float32)
        p = prev_ref[:, lo:lo + POOL_GROUP].astype(jnp.float32) * keep_prev
        q = next_ref[:, lo:lo + POOL_GROUP].astype(jnp.float32) * keep_next
        run = jnp.concatenate([p, x, q], axis=0)
        span = 1
        while span < w:
            run = run + pltpu.roll(run, span, 0)
            span *= 2
        ahead = w - w // 2 - 1
        if ahead:
            run = pltpu.roll(run, win - ahead, 0)
        total = run[POOL_HALO:POOL_HALO + tm]
        count = jnp.minimum(t + (w - w // 2), seq) - jnp.maximum(t - w // 2, 0)
        pooled = total / count.astype(jnp.float32) - x
        y = jnp.dot(pooled.astype(jnp.bfloat16), w_ref[g], preferred_element_type=jnp.float32)
        o_ref[:, lo:lo + POOL_GROUP] = (y * sc_ref[:, lo:lo + POOL_GROUP]).astype(o_ref.dtype)


def _pool_mixer(u, w_group, scale, *, batch, seq, tm):
    T = u.shape[0]
    tm = min(tm, seq)
    n = seq // tm
    hb = tm // POOL_HALO
    last_halo = T // POOL_HALO - 1
    return pl.pallas_call(
        functools.partial(_pool_kernel, seq=seq, tm=tm),
        grid=(batch, n),
        in_specs=[
            pl.BlockSpec((tm, MIX_WIDTH), lambda b, i: (b * n + i, 0)),
            pl.BlockSpec((POOL_HALO, MIX_WIDTH),
                         lambda b, i: (jnp.maximum((b * n + i) * hb - 1, 0), 0)),
            pl.BlockSpec((POOL_HALO, MIX_WIDTH),
                         lambda b, i: (jnp.minimum((b * n + i + 1) * hb, last_halo), 0)),
            pl.BlockSpec((len(POOL_WINDOWS), POOL_GROUP, POOL_GROUP), lambda b, i: (0, 0, 0)),
            pl.BlockSpec((1, MIX_WIDTH), lambda b, i: (0, 0)),
        ],
        out_specs=pl.BlockSpec((tm, MIX_WIDTH), lambda b, i: (b * n + i, 0)),
        out_shape=jax.ShapeDtypeStruct((T, MIX_WIDTH), jnp.bfloat16),
        compiler_params=_params("parallel", "parallel"),
        name="pool_mixer",
    )(u, u, u, w_group, scale.reshape(1, MIX_WIDTH))


def _rope_tables(seq):
    inv_freq = 1.0 / (ROPE_THETA ** (jnp.arange(0, QK_ROPE, 2, dtype=jnp.float32) / QK_ROPE))
    ang = jnp.arange(seq, dtype=jnp.float32)[:, None] * inv_freq[None, :]
    cos, sin = jnp.cos(ang), jnp.sin(ang)
    zeros = jnp.zeros((seq, LANES - QK_ROPE), jnp.float32)
    ct = jnp.concatenate([cos, cos, zeros], axis=-1)
    st = jnp.concatenate([-sin, sin, zeros], axis=-1)
    tab = jnp.concatenate([cos, cos, -sin, sin], axis=-1)
    return ct, st, tab


def _swap_halves(w):
    half = QK_ROPE // 2
    return jnp.concatenate([w[..., half:], w[..., :half]], axis=-1)


def _prep_b(b_w_in, b_w_q_up, b_w_kv_up):
    bf = jnp.bfloat16
    K = b_w_in.shape[0]
    o1 = Q_LORA
    o2 = o1 + KV_LORA
    o3 = o2 + QK_ROPE
    o4 = o3 + XA_WIDTH
    w_cq, w_ckv, w_kr, w_xq, w_gate = (b_w_in[:, :o1], b_w_in[:, o1:o2], b_w_in[:, o2:o3],
                                       b_w_in[:, o3:o4], b_w_in[:, o4:])
    w_in = jnp.concatenate([w_gate, w_xq, w_cq, w_ckv, w_kr, _swap_halves(w_kr),
                            jnp.zeros((K, LANES), b_w_in.dtype)], axis=-1).astype(bf)
    wq = b_w_q_up.reshape(Q_LORA, MLA_HEADS, QK_NOPE + QK_ROPE)
    q_nope, q_rope = wq[..., :QK_NOPE], wq[..., QK_NOPE:]
    pad = jnp.zeros((Q_LORA, MLA_HEADS, LANES - QK_ROPE), wq.dtype)
    wa = jnp.concatenate([q_nope, q_rope, pad], axis=-1).reshape(Q_LORA, MLA_HEADS * QK_PAD).astype(bf)
    wb = jnp.concatenate([_swap_halves(q_rope), pad], axis=-1).reshape(Q_LORA, MLA_HEADS * LANES).astype(bf)
    wkv = b_w_kv_up.reshape(KV_LORA, MLA_HEADS, QK_NOPE + V_DIM)
    wk = wkv[..., :QK_NOPE].reshape(KV_LORA, MLA_HEADS * QK_NOPE).astype(bf)
    wvt = wkv[..., QK_NOPE:].reshape(KV_LORA, MLA_HEADS * V_DIM).T.astype(bf)
    return w_in, wa, wb, wk, wvt


def _prep_lru(gate_w, gate_b):
    wg = (0.5 * jnp.concatenate([gate_w[:, 0], gate_w[:, 1]], axis=-1)).astype(jnp.bfloat16)
    gb = gate_b.reshape(2, 2, LRU_BLOCKS, 1, LRU_BLOCK)
    gb = 0.5 * jnp.concatenate([gb[:, 0], gb[:, 1]], axis=-1)
    return wg, gb


def _trunk(x, mem, w):
    batch, seq, _ = x.shape
    T = batch * seq
    x = x.reshape(T, D_MODEL)
    mem = mem.reshape(batch * MEM_LEN, D_MODEL)
    ct, st, tab = _rope_tables(seq)
    depth = w["norm_pre"].shape[0]
    for i in range(depth):
        kind, j = i % N_MIXERS, i // N_MIXERS
        kv = _norm_matmul(mem, w["norm_mem"][i], w["w_mem_kv"][i], tm=512, tn=1024)
        if kind == 0:
            u = _norm_matmul(x, w["norm_pre"][i], w["a_w_in"][j], tm=1024, tn=1024)
            mixes = _lru_mixer(u, w["a_conv_w"][j], w["a_conv_b"][j], w["a_wg"][j], w["a_gb"][j],
                               w["a_lambda"][j], batch=batch, seq=seq, rows=256)
            xq_off, gate_off = MIX_WIDTH, BRANCH
        elif kind == 1:
            u = _norm_matmul(x, w["norm_pre"][i], w["b_w_in"][j], tm=1024, tn=1024)
            q = _mla_q(u, w["b_q_norm"][j], w["b_wa"][j], w["b_wb"][j], ct, st,
                       seq=seq, tm=512, heads=4)
            k, vt = _mla_kv(u, w["b_kv_norm"][j], w["b_wk"][j], w["b_wvt"][j], tab, seq=seq, tm=512)
            mixes = [_flash_attention(q, k, vt, batch=batch, seq=seq, tq=512, tk=512)]
            xq_off, gate_off = B_XQ_OFF, B_GATE_OFF
        else:
            u = _norm_matmul(x, w["norm_pre"][i], w["c_w_in"][j], tm=1024, tn=1024)
            mixes = [_pool_mixer(u, w["c_w_group"][j], w["c_scale"][j], batch=batch, seq=seq, tm=512)]
            xq_off, gate_off = MIX_WIDTH, BRANCH
        x = _out_block(mixes, u, xq_off, gate_off, kv, w["w_out"][i], x, w["norm_post"][i],
                       seq=seq, tm=256)
    return x.reshape(batch, seq, D_MODEL)


def kernel(x_prompt, x_sample, mem_prompt, mem_sample, norm_pre, norm_post, norm_mem, w_mem_kv, w_out,
           a_w_in, a_conv_w, a_conv_b, a_gate_w, a_gate_b, a_lambda,
           b_w_in, b_q_norm, b_kv_norm, b_w_q_up, b_w_kv_up,
           c_w_in, c_w_group, c_scale):
    bf = jnp.bfloat16
    b_parts = [_prep_b(b_w_in[j], b_w_q_up[j], b_w_kv_up[j]) for j in range(b_w_in.shape[0])]
    lru_parts = [_prep_lru(a_gate_w[j], a_gate_b[j]) for j in range(a_gate_w.shape[0])]
    w = dict(
        norm_pre=norm_pre, norm_post=norm_post, norm_mem=norm_mem,
        w_mem_kv=w_mem_kv.astype(bf), w_out=w_out.astype(bf),
        a_w_in=a_w_in.astype(bf), a_conv_w=a_conv_w, a_conv_b=a_conv_b,
        a_wg=[p[0] for p in lru_parts], a_gb=[p[1] for p in lru_parts], a_lambda=a_lambda,
        b_w_in=[p[0] for p in b_parts], b_q_norm=b_q_norm, b_kv_norm=b_kv_norm,
        b_wa=[p[1] for p in b_parts], b_wb=[p[2] for p in b_parts],
        b_wk=[p[3] for p in b_parts], b_wvt=[p[4] for p in b_parts],
        c_w_in=c_w_in.astype(bf), c_w_group=c_w_group.astype(bf), c_scale=c_scale,
    )
    return (_trunk(x_prompt, mem_prompt, w), _trunk(x_sample, mem_sample, w))
```

```python
import functools
import math

import jax
import jax.numpy as jnp
from jax import lax
from jax.experimental import pallas as pl
from jax.experimental.pallas import tpu as pltpu

D_MODEL = 2048
MIX_WIDTH = 3 * D_MODEL // 2
XA_HEADS = 4
XA_HEAD_DIM = D_MODEL // 8
XA_WIDTH = XA_HEADS * XA_HEAD_DIM
BRANCH = MIX_WIDTH + XA_WIDTH
MEM_LEN = 256
N_MIXERS = 3
CONV_W = 4
CONV_LEFT = 2
LRU_BLOCKS = 12
LRU_BLOCK = MIX_WIDTH // LRU_BLOCKS
LRU_C = 8.0
QK_NOPE = 128
QK_ROPE = 64
V_DIM = 128
MLA_HEADS = MIX_WIDTH // V_DIM
Q_LORA = D_MODEL // 4
KV_LORA = D_MODEL // 8
ROPE_THETA = 10000.0
POOL_WINDOWS = (2, 4, 8, 16)
POOL_GROUP = MIX_WIDTH // len(POOL_WINDOWS)
EPS = 1e-6

LANES = 128
SUBLANES = 8
BF16_ROWS = 16
QK_PAD = 256
AUG_ROW = QK_NOPE + QK_ROPE
FAST_PATH_LIMIT = 64.0
VT_ROWS = V_DIM + BF16_ROWS
VMEM_LIMIT = 56 * 1024 * 1024

B_GATE_OFF = 0
B_XQ_OFF = BRANCH
B_CQ_OFF = B_XQ_OFF + XA_WIDTH
B_CKV_OFF = B_CQ_OFF + Q_LORA
B_KR_OFF = B_CKV_OFF + KV_LORA
B_IN_WIDTH = B_KR_OFF + 2 * QK_ROPE + LANES

LRU_CH = 1024
LRU_LANE_GROUPS = LRU_CH // LANES
LRU_HALO = BF16_ROWS
LRU_STEP = 4
RSQRT_FLOOR = 1e-30

POOL_HALO = BF16_ROWS


def _params(*sem, flags=None):
    return pltpu.CompilerParams(dimension_semantics=sem, vmem_limit_bytes=VMEM_LIMIT, flags=flags)


def _resident(block_shape, index_map):
    return pl.BlockSpec(block_shape, index_map, pipeline_mode=pl.Buffered(1))


def _sigmoid(x):
    return 1.0 / (1.0 + jnp.exp(-x))


def _norm_matmul_kernel(x_ref, g_ref, w_ref, o_ref, h_ref):
    @pl.when(pl.program_id(1) == 0)
    def _():
        x = x_ref[...].astype(jnp.float32)
        ms = jnp.mean(x * x, axis=-1, keepdims=True)
        h_ref[...] = ((x * lax.rsqrt(ms + EPS)) * g_ref[...]).astype(h_ref.dtype)

    o_ref[...] = jnp.dot(h_ref[...], w_ref[...],
                         preferred_element_type=jnp.float32).astype(o_ref.dtype)


def _norm_matmul(x, g, w, *, tm, tn):
    T, K = x.shape
    N = w.shape[1]
    tm = min(tm, T)
    return pl.pallas_call(
        _norm_matmul_kernel,
        grid=(T // tm, N // tn),
        in_specs=[
            pl.BlockSpec((tm, K), lambda i, j: (i, 0)),
            pl.BlockSpec((1, K), lambda i, j: (0, 0)),
            pl.BlockSpec((K, tn), lambda i, j: (0, j)),
        ],
        out_specs=pl.BlockSpec((tm, tn), lambda i, j: (i, j)),
        out_shape=jax.ShapeDtypeStruct((T, N), jnp.bfloat16),
        scratch_shapes=[pltpu.VMEM((tm, K), jnp.bfloat16)],
        compiler_params=_params("parallel", "arbitrary"),
        name="norm_matmul",
    )(x, g.reshape(1, K), w)


def _out_kernel(*refs, n_mix):
    mix_refs = refs[:n_mix]
    xq_ref, gate_ref, kv_ref, w_ref, x_ref, g_ref, o_ref = refs[n_mix:]

    mix = mix_refs[0][...]
    for r in mix_refs[1:]:
        mix = mix + r[...]

    xq = xq_ref[...]
    scale = XA_HEAD_DIM ** -0.5 * math.log2(math.e)
    xa = []
    for h in range(XA_HEADS):
        lo = h * XA_HEAD_DIM
        q = xq[:, lo:lo + XA_HEAD_DIM]
        k = kv_ref[:, lo:lo + XA_HEAD_DIM]
        v = kv_ref[:, XA_WIDTH + lo:XA_WIDTH + lo + XA_HEAD_DIM]
        s = lax.dot_general(q, k, (((1,), (1,)), ((), ())), preferred_element_type=jnp.float32)
        m = jnp.max(s, axis=-1, keepdims=True)
        p = jnp.exp2((s - m) * scale)
        l = jnp.sum(p, axis=-1, keepdims=True)
        p = (p / l).astype(jnp.bfloat16)
        xa.append(jnp.dot(p, v, preferred_element_type=jnp.float32))

    half_gate = 0.5 * gate_ref[...].astype(jnp.float32)
    act = (half_gate * jnp.tanh(half_gate) + half_gate).astype(jnp.bfloat16)
    acc = jnp.dot(mix * act[:, :MIX_WIDTH], w_ref[:MIX_WIDTH, :], preferred_element_type=jnp.float32)
    xa = jnp.concatenate([a.astype(jnp.bfloat16) for a in xa], axis=-1)
    acc = acc + jnp.dot(xa * act[:, MIX_WIDTH:], w_ref[MIX_WIDTH:, :],
                        preferred_element_type=jnp.float32)

    ms = jnp.mean(acc * acc, axis=-1, keepdims=True)
    o_ref[...] = x_ref[...] + (acc * lax.rsqrt(ms + EPS)) * g_ref[...]


def _out_block(mixes, u, xq_off, gate_off, kv, w_out, x, g_post, *, seq, tm):
    T = x.shape[0]
    tm = min(tm, seq)
    per_seq = seq // tm
    xq_blk = xq_off // XA_WIDTH
    gate_blk = gate_off // BRANCH
    mix_spec = pl.BlockSpec((tm, MIX_WIDTH), lambda i: (i, 0))
    return pl.pallas_call(
        functools.partial(_out_kernel, n_mix=len(mixes)),
        grid=(T // tm,),
        in_specs=[mix_spec] * len(mixes) + [
            pl.BlockSpec((tm, XA_WIDTH), lambda i: (i, xq_blk)),
            pl.BlockSpec((tm, BRANCH), lambda i: (i, gate_blk)),
            pl.BlockSpec((MEM_LEN, 2 * XA_WIDTH), lambda i: (i // per_seq, 0)),
            _resident((BRANCH, D_MODEL), lambda i: (0, 0)),
            pl.BlockSpec((tm, D_MODEL), lambda i: (i, 0)),
            pl.BlockSpec((1, D_MODEL), lambda i: (0, 0)),
        ],
        out_specs=pl.BlockSpec((tm, D_MODEL), lambda i: (i, 0)),
        out_shape=jax.ShapeDtypeStruct((T, D_MODEL), jnp.float32),
        compiler_params=_params("parallel"),
        name="xattn_gate_outproj",
    )(*mixes, u, u, kv, w_out, x, g_post.reshape(1, D_MODEL))


def _lru_kernel(cur_f, prev_f, next_f, cur_r, prev_r, next_r, cw_ref, cb_ref, wg_ref, gb_ref,
                lam_ref, hf_ref, hr_ref, a_f, b_f, h_f, a_r, b_r, h_r, carry, *, rows, pitch):
    s = pl.program_id(2)
    n = pl.num_programs(2)

    @pl.when(s == 0)
    def _():
        carry[...] = jnp.zeros_like(carry)

    win = rows + 2 * LRU_HALO

    def gates(cur, prev, nxt, chunk, d, a_s, b_s):
        p = prev[...].astype(jnp.float32) * (chunk > 0).astype(jnp.float32)
        q = nxt[...].astype(jnp.float32) * (chunk < n - 1).astype(jnp.float32)
        w = jnp.concatenate([p, cur[...].astype(jnp.float32), q], axis=0)
        taps = (pltpu.roll(w, 2, 0), pltpu.roll(w, 1, 0), w, pltpu.roll(w, win - 1, 0))
        u = cb_ref[...]
        for k in range(CONV_W):
            u = u + taps[k][LRU_HALO:LRU_HALO + rows] * cw_ref[k:k + 1, :]
        lam = lam_ref[d]
        softplus = jnp.maximum(-lam, 0.0) + jnp.log(1.0 + jnp.exp(-jnp.abs(lam)))
        half_decay = (-0.5 * LRU_C * math.log2(math.e)) * softplus
        for blk in range(LRU_CH // LRU_BLOCK):
            lo = blk * LRU_BLOCK
            ub = u[:, lo:lo + LRU_BLOCK]
            g = jnp.dot(ub.astype(jnp.bfloat16), wg_ref[d, blk],
                        preferred_element_type=jnp.float32) + gb_ref[d, blk]
            hd = half_decay[:, lo:lo + LRU_BLOCK]
            a = jnp.exp2(hd * jnp.tanh(g[:, :LRU_BLOCK]) + hd)
            half_u = 0.5 * ub
            iu = half_u * jnp.tanh(g[:, LRU_BLOCK:]) + half_u
            y = 1.0 - a * a
            b = (y * lax.rsqrt(jnp.maximum(y, RSQRT_FLOOR))) * iu
            for half in range(LRU_BLOCK // LANES):
                base = (blk * (LRU_BLOCK // LANES) + half) * pitch
                a_s[base:base + rows, :] = a[:, half * LANES:(half + 1) * LANES]
                b_s[base:base + rows, :] = b[:, half * LANES:(half + 1) * LANES]

    gates(cur_f, prev_f, next_f, s, 0, a_f, b_f)
    gates(cur_r, prev_r, next_r, n - 1 - s, 1, a_r, b_r)

    def block(a_s, b_s, h_s, h, t0, sign):
        idx = [pl.ds(t0 + sign * j, LRU_LANE_GROUPS, stride=pitch) for j in range(LRU_STEP)]
        a = [a_s[i, :] for i in idx]
        b = [b_s[i, :] for i in idx]
        a01, b01 = a[1] * a[0], a[1] * b[0] + b[1]
        a23, b23 = a[3] * a[2], a[3] * b[2] + b[3]
        a03, b03 = a23 * a01, a23 * b01 + b23
        h0 = a[0] * h + b[0]
        h1 = a01 * h + b01
        h3 = a03 * h + b03
        h2 = a[2] * h1 + b[2]
        for i, v in zip(idx, (h0, h1, h2, h3)):
            h_s[i, :] = v
        return h3

    blocks_per_iter = 8

    def body(it, hs):
        hf, hr = hs
        for j in range(blocks_per_iter):
            t = (it * blocks_per_iter + j) * LRU_STEP
            hf = block(a_f, b_f, h_f, hf, t, 1)
            hr = block(a_r, b_r, h_r, hr, rows - 1 - t, -1)
        return hf, hr

    hf, hr = lax.fori_loop(0, rows // (LRU_STEP * blocks_per_iter), body, (carry[0], carry[1]))
    carry[0] = hf
    carry[1] = hr

    for grp in range(LRU_LANE_GROUPS):
        base = grp * pitch
        hf_ref[:, grp * LANES:(grp + 1) * LANES] = h_f[base:base + rows, :].astype(hf_ref.dtype)
        hr_ref[:, grp * LANES:(grp + 1) * LANES] = h_r[base:base + rows, :].astype(hr_ref.dtype)


def _lru_mixer(u, conv_w, conv_b, wg, gb, lam, *, batch, seq, rows):
    T = u.shape[0]
    rows = min(rows, seq)
    n = seq // rows
    pitch = rows + SUBLANES
    hb = rows // LRU_HALO
    last_halo = T // LRU_HALO - 1

    def cur_map(rev):
        def f(b, c, s):
            chunk = (n - 1 - s) if rev else s
            return (b * n + chunk, c)
        return f

    def prev_map(rev):
        def f(b, c, s):
            chunk = (n - 1 - s) if rev else s
            return (jnp.maximum((b * n + chunk) * hb - 1, 0), c)
        return f

    def next_map(rev):
        def f(b, c, s):
            chunk = (n - 1 - s) if rev else s
            return (jnp.minimum((b * n + chunk + 1) * hb, last_halo), c)
        return f

    blocks = LRU_CH // LRU_BLOCK
    in_specs = []
    for rev in (False, True):
        in_specs += [
            pl.BlockSpec((rows, LRU_CH), cur_map(rev)),
            pl.BlockSpec((LRU_HALO, LRU_CH), prev_map(rev)),
            pl.BlockSpec((LRU_HALO, LRU_CH), next_map(rev)),
        ]
    in_specs += [
        pl.BlockSpec((CONV_W, LRU_CH), lambda b, c, s: (0, c)),
        pl.BlockSpec((1, LRU_CH), lambda b, c, s: (0, c)),
        pl.BlockSpec((2, blocks, LRU_BLOCK, 2 * LRU_BLOCK), lambda b, c, s: (0, c, 0, 0)),
        pl.BlockSpec((2, blocks, 1, 2 * LRU_BLOCK), lambda b, c, s: (0, c, 0, 0)),
        pl.BlockSpec((2, 1, LRU_CH), lambda b, c, s: (0, 0, c)),
    ]
    slab = pltpu.VMEM((LRU_LANE_GROUPS * pitch, LANES), jnp.float32)
    out = jax.ShapeDtypeStruct((T, MIX_WIDTH), jnp.bfloat16)
    return pl.pallas_call(
        functools.partial(_lru_kernel, rows=rows, pitch=pitch),
        grid=(batch, MIX_WIDTH // LRU_CH, n),
        in_specs=in_specs,
        out_specs=[pl.BlockSpec((rows, LRU_CH), cur_map(False)),
                   pl.BlockSpec((rows, LRU_CH), cur_map(True))],
        out_shape=[out, out],
        scratch_shapes=[slab, slab, slab, slab, slab, slab,
                        pltpu.VMEM((2, LRU_LANE_GROUPS, LANES), jnp.float32)],
        compiler_params=_params("parallel", "parallel", "arbitrary"),
        name="rglru",
    )(u, u, u, u, u, u, conv_w, conv_b.reshape(1, MIX_WIDTH), wg, gb,
      lam.reshape(2, 1, MIX_WIDTH))


def _mla_q_kernel(cq_ref, g_ref, wa_ref, wb_ref, ct_ref, st_ref, o_ref, h_ref, *, heads):
    @pl.when(pl.program_id(1) == 0)
    def _():
        x = cq_ref[...].astype(jnp.float32)
        ms = jnp.mean(x * x, axis=-1, keepdims=True)
        h_ref[...] = ((x * lax.rsqrt(ms + EPS)) * g_ref[...]).astype(h_ref.dtype)

    h = h_ref[...]
    qa = jnp.dot(h, wa_ref[...], preferred_element_type=jnp.float32)
    qb = jnp.dot(h, wb_ref[...], preferred_element_type=jnp.float32)
    scale = (QK_NOPE + QK_ROPE) ** -0.5 * math.log2(math.e)
    ct = ct_ref[...] * scale
    st = st_ref[...] * scale
    for hh in range(heads):
        lo = hh * QK_PAD
        o_ref[:, lo:lo + LANES] = (qa[:, lo:lo + LANES] * scale).astype(o_ref.dtype)
        rope = qa[:, lo + LANES:lo + 2 * LANES] * ct + qb[:, hh * LANES:(hh + 1) * LANES] * st
        o_ref[:, lo + LANES:lo + 2 * LANES] = rope.astype(o_ref.dtype)


def _mla_q(u, q_norm, wa, wb, ct, st, *, seq, tm, heads):
    T = u.shape[0]
    tm = min(tm, seq)
    per_seq = seq // tm
    cq_blk = B_CQ_OFF // Q_LORA
    return pl.pallas_call(
        functools.partial(_mla_q_kernel, heads=heads),
        grid=(T // tm, MLA_HEADS // heads),
        in_specs=[
            pl.BlockSpec((tm, Q_LORA), lambda i, j: (i, cq_blk)),
            pl.BlockSpec((1, Q_LORA), lambda i, j: (0, 0)),
            pl.BlockSpec((Q_LORA, heads * QK_PAD), lambda i, j: (0, j)),
            pl.BlockSpec((Q_LORA, heads * LANES), lambda i, j: (0, j)),
            pl.BlockSpec((tm, LANES), lambda i, j: (i % per_seq, 0)),
            pl.BlockSpec((tm, LANES), lambda i, j: (i % per_seq, 0)),
        ],
        out_specs=pl.BlockSpec((tm, heads * QK_PAD), lambda i, j: (i, j)),
        out_shape=jax.ShapeDtypeStruct((T, MLA_HEADS * QK_PAD), jnp.bfloat16),
        scratch_shapes=[pltpu.VMEM((tm, Q_LORA), jnp.bfloat16)],
        compiler_params=_params("parallel", "arbitrary"),
        name="mla_q_proj",
    )(u, q_norm.reshape(1, Q_LORA), wa, wb, ct, st)


def _mla_kv_kernel(ckv_ref, g_ref, wk_ref, wvt_ref, kr_ref, tab_ref, k_ref, vt_ref):
    x = ckv_ref[...].astype(jnp.float32)
    ms = jnp.mean(x * x, axis=-1, keepdims=True)
    h = ((x * lax.rsqrt(ms + EPS)) * g_ref[...]).astype(jnp.bfloat16)
    kn = jnp.dot(h, wk_ref[...], preferred_element_type=jnp.float32)
    vt = lax.dot_general(wvt_ref[...], h, (((1,), (1,)), ((), ())),
                         preferred_element_type=jnp.float32).astype(vt_ref.dtype)
    ones = jnp.ones((VT_ROWS - V_DIM, vt.shape[1]), vt_ref.dtype)
    for hh in range(MLA_HEADS):
        vt_ref[hh * VT_ROWS:hh * VT_ROWS + V_DIM, :] = vt[hh * V_DIM:(hh + 1) * V_DIM, :]
        vt_ref[hh * VT_ROWS + V_DIM:(hh + 1) * VT_ROWS, :] = ones
    t = kr_ref[...].astype(jnp.float32) * tab_ref[...]
    rot = t + pltpu.roll(t, QK_ROPE, 1)
    lane = lax.broadcasted_iota(jnp.int32, rot.shape, 1)
    rot = jnp.where(lane < QK_ROPE, rot, jnp.where(lane == QK_ROPE, 1.0, 0.0)).astype(k_ref.dtype)
    for hh in range(MLA_HEADS):
        k_ref[:, hh * QK_PAD:hh * QK_PAD + LANES] = kn[:, hh * LANES:(hh + 1) * LANES].astype(k_ref.dtype)
        k_ref[:, hh * QK_PAD + LANES:(hh + 1) * QK_PAD] = rot


def _mla_kv(u, kv_norm, wk, wvt, tab, *, seq, tm):
    T = u.shape[0]
    tm = min(tm, seq)
    per_seq = seq // tm
    ckv_blk = B_CKV_OFF // KV_LORA
    kr_blk = B_KR_OFF // LANES
    return pl.pallas_call(
        _mla_kv_kernel,
        grid=(T // tm,),
        in_specs=[
            pl.BlockSpec((tm, KV_LORA), lambda i: (i, ckv_blk)),
            pl.BlockSpec((1, KV_LORA), lambda i: (0, 0)),
            pl.BlockSpec((KV_LORA, MLA_HEADS * QK_NOPE), lambda i: (0, 0)),
            pl.BlockSpec((MLA_HEADS * V_DIM, KV_LORA), lambda i: (0, 0)),
            pl.BlockSpec((tm, LANES), lambda i: (i, kr_blk)),
            pl.BlockSpec((tm, LANES), lambda i: (i % per_seq, 0)),
        ],
        out_specs=[pl.BlockSpec((tm, MLA_HEADS * QK_PAD), lambda i: (i, 0)),
                   pl.BlockSpec((MLA_HEADS * VT_ROWS, tm), lambda i: (0, i))],
        out_shape=[jax.ShapeDtypeStruct((T, MLA_HEADS * QK_PAD), jnp.bfloat16),
                   jax.ShapeDtypeStruct((MLA_HEADS * VT_ROWS, T), jnp.bfloat16)],
        compiler_params=_params("parallel"),
        name="mla_kv_proj",
    )(u, kv_norm.reshape(1, KV_LORA), wk, wvt, u, tab)


def _flash_kernel(q_ref, k_ref, vt_ref, o_ref, qt_buf, acc_ref, *, tk, n_kv):
    bf = jnp.bfloat16
    qt = q_ref[...].astype(jnp.float32).T.astype(bf)
    qt_buf[0] = qt
    qt_buf[1] = qt
    refs = {}
    m_true = worst = None
    for j in range(n_kv):
        k = k_ref[j * tk:(j + 1) * tk, :]
        s = jnp.dot(k, qt_buf[j % 2], preferred_element_type=jnp.float32)
        cmax = jnp.max(s, axis=0, keepdims=True)
        if j == 0:
            refs[0] = cmax
            p = jnp.exp2((s - cmax).astype(bf))
            m_true = cmax
        else:
            p = jnp.exp2(s.astype(bf))
            m_true = jnp.maximum(m_true, refs[j] + cmax)
            worst = cmax if worst is None else jnp.maximum(worst, cmax)
        for t in ([1, 2] if j == 0 else [j + 2]):
            if t < n_kv:
                nref = m_true.astype(bf)
                qt_buf[t % 2, AUG_ROW:AUG_ROW + BF16_ROWS, :] = jnp.broadcast_to(
                    -nref, (BF16_ROWS, nref.shape[1]))
                refs[t] = nref.astype(jnp.float32)
        pv = jnp.dot(vt_ref[:, j * tk:(j + 1) * tk], p, preferred_element_type=jnp.float32)
        if j == 0:
            acc_ref[...] = pv
        else:
            acc_ref[...] = jnp.exp2(refs[j - 1] - refs[j]) * acc_ref[...] + pv

    def finish():
        out = acc_ref[:V_DIM, :] / acc_ref[V_DIM:V_DIM + 1, :]
        o_ref[...] = out.T.astype(o_ref.dtype)

    finish()
    if worst is None:
        return

    @pl.when(jnp.max(worst) > FAST_PATH_LIMIT)
    def _():
        m_run = None
        for j in range(n_kv):
            s = jnp.dot(k_ref[j * tk:(j + 1) * tk, :], qt, preferred_element_type=jnp.float32)
            cmax = jnp.max(s, axis=0, keepdims=True)
            m_new = cmax if j == 0 else jnp.maximum(m_run, cmax)
            p = jnp.exp2((s - m_new).astype(bf))
            pv = jnp.dot(vt_ref[:, j * tk:(j + 1) * tk], p, preferred_element_type=jnp.float32)
            if j == 0:
                acc_ref[...] = pv
            else:
                acc_ref[...] = jnp.exp2(m_run - m_new) * acc_ref[...] + pv
            m_run = m_new
        finish()


def _flash_attention(q, k, vt, *, batch, seq, tq, tk):
    T = q.shape[0]
    tq = min(tq, seq)
    tk = min(tk, seq)
    nq = seq // tq
    return pl.pallas_call(
        functools.partial(_flash_kernel, tk=tk, n_kv=seq // tk),
        grid=(batch, MLA_HEADS, nq),
        in_specs=[
            pl.BlockSpec((tq, QK_PAD), lambda b, h, i: (b * nq + i, h)),
            pl.BlockSpec((seq, QK_PAD), lambda b, h, i: (b, h)),
            pl.BlockSpec((VT_ROWS, seq), lambda b, h, i: (h, b)),
        ],
        out_specs=pl.BlockSpec((tq, V_DIM), lambda b, h, i: (b * nq + i, h)),
        out_shape=jax.ShapeDtypeStruct((T, MLA_HEADS * V_DIM), jnp.bfloat16),
        scratch_shapes=[pltpu.VMEM((2, QK_PAD, tq), jnp.bfloat16),
                        pltpu.VMEM((VT_ROWS, tq), jnp.float32)],
        compiler_params=_params("parallel", "parallel", "arbitrary"),
        name="mla_flash_attention",
    )(q, k, vt)


def _pool_kernel(cur_ref, prev_ref, next_ref, w_ref, sc_ref, o_ref, *, seq, tm):
    i = pl.program_id(1)
    n = pl.num_programs(1)
    win = tm + 2 * POOL_HALO
    keep_prev = (i > 0).astype(jnp.float32)
    keep_next = (i < n - 1).astype(jnp.float32)
    t = i * tm + lax.broadcasted_iota(jnp.int32, (tm, 1), 0)
    for g, w in enumerate(POOL_WINDOWS):
        lo = g * POOL_GROUP
        x = cur_ref[:, lo:lo + POOL_GROUP].astype(jnp.float32)
        p = prev_ref[:, lo:lo + POOL_GROUP].astype(jnp.float32) * keep_prev
        q = next_ref[:, lo:lo + POOL_GROUP].astype(jnp.float32) * keep_next
        run = jnp.concatenate([p, x, q], axis=0)
        span = 1
        while span < w:
            run = run + pltpu.roll(run, span, 0)
            span *= 2
        ahead = w - w // 2 - 1
        if ahead:
            run = pltpu.roll(run, win - ahead, 0)
        total = run[POOL_HALO:POOL_HALO + tm]
        count = jnp.minimum(t + (w - w // 2), seq) - jnp.maximum(t - w // 2, 0)
        pooled = total / count.astype(jnp.float32) - x
        y = jnp.dot(pooled.astype(jnp.bfloat16), w_ref[g], preferred_element_type=jnp.float32)
        o_ref[:, lo:lo + POOL_GROUP] = (y * sc_ref[:, lo:lo + POOL_GROUP]).astype(o_ref.dtype)


def _pool_mixer(u, w_group, scale, *, batch, seq, tm):
    T = u.shape[0]
    tm = min(tm, seq)
    n = seq // tm
    hb = tm // POOL_HALO
    last_halo = T // POOL_HALO - 1
    return pl.pallas_call(
        functools.partial(_pool_kernel, seq=seq, tm=tm),
        grid=(batch, n),
        in_specs=[
            pl.BlockSpec((tm, MIX_WIDTH), lambda b, i: (b * n + i, 0)),
            pl.BlockSpec((POOL_HALO, MIX_WIDTH),
                         lambda b, i: (jnp.maximum((b * n + i) * hb - 1, 0), 0)),
            pl.BlockSpec((POOL_HALO, MIX_WIDTH),
                         lambda b, i: (jnp.minimum((b * n + i + 1) * hb, last_halo), 0)),
            pl.BlockSpec((len(POOL_WINDOWS), POOL_GROUP, POOL_GROUP), lambda b, i: (0, 0, 0)),
            pl.BlockSpec((1, MIX_WIDTH), lambda b, i: (0, 0)),
        ],
        out_specs=pl.BlockSpec((tm, MIX_WIDTH), lambda b, i: (b * n + i, 0)),
        out_shape=jax.ShapeDtypeStruct((T, MIX_WIDTH), jnp.bfloat16),
        compiler_params=_params("parallel", "parallel"),
        name="pool_mixer",
    )(u, u, u, w_group, scale.reshape(1, MIX_WIDTH))


def _rope_tables(seq):
    inv_freq = 1.0 / (ROPE_THETA ** (jnp.arange(0, QK_ROPE, 2, dtype=jnp.float32) / QK_ROPE))
    ang = jnp.arange(seq, dtype=jnp.float32)[:, None] * inv_freq[None, :]
    cos, sin = jnp.cos(ang), jnp.sin(ang)
    zeros = jnp.zeros((seq, LANES - QK_ROPE), jnp.float32)
    ct = jnp.concatenate([cos, cos, zeros], axis=-1)
    st = jnp.concatenate([-sin, sin, zeros], axis=-1)
    tab = jnp.concatenate([cos, cos, -sin, sin], axis=-1)
    return ct, st, tab


def _swap_halves(w):
    half = QK_ROPE // 2
    return jnp.concatenate([w[..., half:], w[..., :half]], axis=-1)


def _prep_b(b_w_in, b_w_q_up, b_w_kv_up):
    bf = jnp.bfloat16
    K = b_w_in.shape[0]
    o1 = Q_LORA
    o2 = o1 + KV_LORA
    o3 = o2 + QK_ROPE
    o4 = o3 + XA_WIDTH
    w_cq, w_ckv, w_kr, w_xq, w_gate = (b_w_in[:, :o1], b_w_in[:, o1:o2], b_w_in[:, o2:o3],
                                       b_w_in[:, o3:o4], b_w_in[:, o4:])
    w_in = jnp.concatenate([w_gate, w_xq, w_cq, w_ckv, w_kr, _swap_halves(w_kr),
                            jnp.zeros((K, LANES), b_w_in.dtype)], axis=-1).astype(bf)
    wq = b_w_q_up.reshape(Q_LORA, MLA_HEADS, QK_NOPE + QK_ROPE)
    q_nope, q_rope = wq[..., :QK_NOPE], wq[..., QK_NOPE:]
    pad = jnp.zeros((Q_LORA, MLA_HEADS, LANES - QK_ROPE), wq.dtype)
    wa = jnp.concatenate([q_nope, q_rope, pad], axis=-1).reshape(Q_LORA, MLA_HEADS * QK_PAD).astype(bf)
    wb = jnp.concatenate([_swap_halves(q_rope), pad], axis=-1).reshape(Q_LORA, MLA_HEADS * LANES).astype(bf)
    wkv = b_w_kv_up.reshape(KV_LORA, MLA_HEADS, QK_NOPE + V_DIM)
    wk = wkv[..., :QK_NOPE].reshape(KV_LORA, MLA_HEADS * QK_NOPE).astype(bf)
    wvt = wkv[..., QK_NOPE:].reshape(KV_LORA, MLA_HEADS * V_DIM).T.astype(bf)
    return w_in, wa, wb, wk, wvt


def _prep_lru(gate_w, gate_b):
    wg = (0.5 * jnp.concatenate([gate_w[:, 0], gate_w[:, 1]], axis=-1)).astype(jnp.bfloat16)
    gb = gate_b.reshape(2, 2, LRU_BLOCKS, 1, LRU_BLOCK)
    gb = 0.5 * jnp.concatenate([gb[:, 0], gb[:, 1]], axis=-1)
    return wg, gb


def _trunk(x, mem, w):
    batch, seq, _ = x.shape
    T = batch * seq
    x = x.reshape(T, D_MODEL)
    mem = mem.reshape(batch * MEM_LEN, D_MODEL)
    ct, st, tab = _rope_tables(seq)
    depth = w["norm_pre"].shape[0]
    for i in range(depth):
        kind, j = i % N_MIXERS, i // N_MIXERS
        kv = _norm_matmul(mem, w["norm_mem"][i], w["w_mem_kv"][i], tm=512, tn=1024)
        if kind == 0:
            u = _norm_matmul(x, w["norm_pre"][i], w["a_w_in"][j], tm=1024, tn=2048)
            mixes = _lru_mixer(u, w["a_conv_w"][j], w["a_conv_b"][j], w["a_wg"][j], w["a_gb"][j],
                               w["a_lambda"][j], batch=batch, seq=seq, rows=256)
            xq_off, gate_off = MIX_WIDTH, BRANCH
        elif kind == 1:
            u = _norm_matmul(x, w["norm_pre"][i], w["b_w_in"][j], tm=1024, tn=2048)
            q = _mla_q(u, w["b_q_norm"][j], w["b_wa"][j], w["b_wb"][j], ct, st,
                       seq=seq, tm=512, heads=4)
            k, vt = _mla_kv(u, w["b_kv_norm"][j], w["b_wk"][j], w["b_wvt"][j], tab, seq=seq, tm=512)
            mixes = [_flash_attention(q, k, vt, batch=batch, seq=seq, tq=1024, tk=512)]
            xq_off, gate_off = B_XQ_OFF, B_GATE_OFF
        else:
            u = _norm_matmul(x, w["norm_pre"][i], w["c_w_in"][j], tm=1024, tn=2048)
            mixes = [_pool_mixer(u, w["c_w_group"][j], w["c_scale"][j], batch=batch, seq=seq, tm=512)]
            xq_off, gate_off = MIX_WIDTH, BRANCH
        x = _out_block(mixes, u, xq_off, gate_off, kv, w["w_out"][i], x, w["norm_post"][i],
                       seq=seq, tm=256)
    return x.reshape(batch, seq, D_MODEL)


def kernel(x_prompt, x_sample, mem_prompt, mem_sample, norm_pre, norm_post, norm_mem, w_mem_kv, w_out,
           a_w_in, a_conv_w, a_conv_b, a_gate_w, a_gate_b, a_lambda,
           b_w_in, b_q_norm, b_kv_norm, b_w_q_up, b_w_kv_up,
           c_w_in, c_w_group, c_scale):
    bf = jnp.bfloat16
    b_parts = [_prep_b(b_w_in[j], b_w_q_up[j], b_w_kv_up[j]) for j in range(b_w_in.shape[0])]
    lru_parts = [_prep_lru(a_gate_w[j], a_gate_b[j]) for j in range(a_gate_w.shape[0])]
    w = dict(
        norm_pre=norm_pre, norm_post=norm_post, norm_mem=norm_mem,
        w_mem_kv=w_mem_kv.astype(bf), w_out=w_out.astype(bf),
        a_w_in=a_w_in.astype(bf), a_conv_w=a_conv_w, a_conv_b=a_conv_b,
        a_wg=[p[0] for p in lru_parts], a_gb=[p[1] for p in lru_parts], a_lambda=a_lambda,
        b_w_in=[p[0] for p in b_parts], b_q_norm=b_q_norm, b_kv_norm=b_kv_norm,
        b_wa=[p[1] for p in b_parts], b_wb=[p[2] for p in b_parts],
        b_wk=[p[3] for p in b_parts], b_wvt=[p[4] for p in b_parts],
        c_w_in=c_w_in.astype(bf), c_w_group=c_w_group.astype(bf), c_scale=c_scale,
    )
    return (_trunk(x_prompt, mem_prompt, w), _trunk(x_sample, mem_sample, w))
```

```python
import functools
import math

import jax
import jax.numpy as jnp
from jax import lax
from jax.experimental import pallas as pl
from jax.experimental.pallas import tpu as pltpu

D_MODEL = 2048
MIX_WIDTH = 3 * D_MODEL // 2
XA_HEADS = 4
XA_HEAD_DIM = D_MODEL // 8
XA_WIDTH = XA_HEADS * XA_HEAD_DIM
BRANCH = MIX_WIDTH + XA_WIDTH
MEM_LEN = 256
N_MIXERS = 3
CONV_W = 4
CONV_LEFT = 2
LRU_BLOCKS = 12
LRU_BLOCK = MIX_WIDTH // LRU_BLOCKS
LRU_C = 8.0
QK_NOPE = 128
QK_ROPE = 64
V_DIM = 128
MLA_HEADS = MIX_WIDTH // V_DIM
Q_LORA = D_MODEL // 4
KV_LORA = D_MODEL // 8
ROPE_THETA = 10000.0
POOL_WINDOWS = (2, 4, 8, 16)
POOL_GROUP = MIX_WIDTH // len(POOL_WINDOWS)
EPS = 1e-6

LANES = 128
SUBLANES = 8
BF16_ROWS = 16
QK_PAD = 256
AUG_ROW = QK_NOPE + QK_ROPE
FAST_PATH_LIMIT = 64.0
VT_ROWS = V_DIM + BF16_ROWS
VMEM_LIMIT = 56 * 1024 * 1024

B_GATE_OFF = 0
B_XQ_OFF = BRANCH
B_CQ_OFF = B_XQ_OFF + XA_WIDTH
B_CKV_OFF = B_CQ_OFF + Q_LORA
B_KR_OFF = B_CKV_OFF + KV_LORA
B_IN_WIDTH = B_KR_OFF + 2 * QK_ROPE + LANES

LRU_CH = 1024
LRU_LANE_GROUPS = LRU_CH // LANES
LRU_HALO = BF16_ROWS
LRU_STEP = 4
RSQRT_FLOOR = 1e-30

POOL_HALO = BF16_ROWS


def _params(*sem, flags=None):
    return pltpu.CompilerParams(dimension_semantics=sem, vmem_limit_bytes=VMEM_LIMIT, flags=flags)


def _resident(block_shape, index_map):
    return pl.BlockSpec(block_shape, index_map, pipeline_mode=pl.Buffered(1))


def _sigmoid(x):
    return 1.0 / (1.0 + jnp.exp(-x))


def _norm_matmul_kernel(x_ref, g_ref, w_ref, o_ref, h_ref):
    @pl.when(pl.program_id(1) == 0)
    def _():
        x = x_ref[...].astype(jnp.float32)
        ms = jnp.mean(x * x, axis=-1, keepdims=True)
        h_ref[...] = ((x * lax.rsqrt(ms + EPS)) * g_ref[...]).astype(h_ref.dtype)

    o_ref[...] = jnp.dot(h_ref[...], w_ref[...],
                         preferred_element_type=jnp.float32).astype(o_ref.dtype)


def _norm_matmul(x, g, w, *, tm, tn):
    T, K = x.shape
    N = w.shape[1]
    tm = min(tm, T)
    return pl.pallas_call(
        _norm_matmul_kernel,
        grid=(T // tm, N // tn),
        in_specs=[
            pl.BlockSpec((tm, K), lambda i, j: (i, 0)),
            pl.BlockSpec((1, K), lambda i, j: (0, 0)),
            pl.BlockSpec((K, tn), lambda i, j: (0, j)),
        ],
        out_specs=pl.BlockSpec((tm, tn), lambda i, j: (i, j)),
        out_shape=jax.ShapeDtypeStruct((T, N), jnp.bfloat16),
        scratch_shapes=[pltpu.VMEM((tm, K), jnp.bfloat16)],
        compiler_params=_params("parallel", "arbitrary"),
        name="norm_matmul",
    )(x, g.reshape(1, K), w)


def _out_kernel(*refs, n_mix):
    mix_refs = refs[:n_mix]
    xq_ref, gate_ref, kv_ref, w_ref, x_ref, g_ref, o_ref = refs[n_mix:]

    mix = mix_refs[0][...]
    for r in mix_refs[1:]:
        mix = mix + r[...]

    xq = xq_ref[...]
    scale = XA_HEAD_DIM ** -0.5 * math.log2(math.e)
    xa = []
    for h in range(XA_HEADS):
        lo = h * XA_HEAD_DIM
        q = xq[:, lo:lo + XA_HEAD_DIM]
        k = kv_ref[:, lo:lo + XA_HEAD_DIM]
        v = kv_ref[:, XA_WIDTH + lo:XA_WIDTH + lo + XA_HEAD_DIM]
        s = lax.dot_general(q, k, (((1,), (1,)), ((), ())), preferred_element_type=jnp.float32)
        m = jnp.max(s, axis=-1, keepdims=True)
        p = jnp.exp2((s - m) * scale)
        l = jnp.sum(p, axis=-1, keepdims=True)
        p = (p / l).astype(jnp.bfloat16)
        xa.append(jnp.dot(p, v, preferred_element_type=jnp.float32))

    half_gate = 0.5 * gate_ref[...].astype(jnp.float32)
    act = (half_gate * jnp.tanh(half_gate) + half_gate).astype(jnp.bfloat16)
    acc = jnp.dot(mix * act[:, :MIX_WIDTH], w_ref[:MIX_WIDTH, :], preferred_element_type=jnp.float32)
    xa = jnp.concatenate([a.astype(jnp.bfloat16) for a in xa], axis=-1)
    acc = acc + jnp.dot(xa * act[:, MIX_WIDTH:], w_ref[MIX_WIDTH:, :],
                        preferred_element_type=jnp.float32)

    ms = jnp.mean(acc * acc, axis=-1, keepdims=True)
    o_ref[...] = x_ref[...] + (acc * lax.rsqrt(ms + EPS)) * g_ref[...]


def _out_block(mixes, u, xq_off, gate_off, kv, w_out, x, g_post, *, seq, tm):
    T = x.shape[0]
    tm = min(tm, seq)
    per_seq = seq // tm
    xq_blk = xq_off // XA_WIDTH
    gate_blk = gate_off // BRANCH
    mix_spec = pl.BlockSpec((tm, MIX_WIDTH), lambda i: (i, 0))
    return pl.pallas_call(
        functools.partial(_out_kernel, n_mix=len(mixes)),
        grid=(T // tm,),
        in_specs=[mix_spec] * len(mixes) + [
            pl.BlockSpec((tm, XA_WIDTH), lambda i: (i, xq_blk)),
            pl.BlockSpec((tm, BRANCH), lambda i: (i, gate_blk)),
            pl.BlockSpec((MEM_LEN, 2 * XA_WIDTH), lambda i: (i // per_seq, 0)),
            _resident((BRANCH, D_MODEL), lambda i: (0, 0)),
            pl.BlockSpec((tm, D_MODEL), lambda i: (i, 0)),
            pl.BlockSpec((1, D_MODEL), lambda i: (0, 0)),
        ],
        out_specs=pl.BlockSpec((tm, D_MODEL), lambda i: (i, 0)),
        out_shape=jax.ShapeDtypeStruct((T, D_MODEL), jnp.float32),
        compiler_params=_params("parallel"),
        name="xattn_gate_outproj",
    )(*mixes, u, u, kv, w_out, x, g_post.reshape(1, D_MODEL))


def _lru_kernel(cur_f, prev_f, next_f, cur_r, prev_r, next_r, cw_ref, cb_ref, wg_ref,
                gb_ref, lam_ref, hf_ref, hr_ref, a_f, b_f, h_f, a_r, b_r, h_r, carry, *, rows, pitch):
    s = pl.program_id(2)
    n = pl.num_programs(2)

    @pl.when(s == 0)
    def _():
        carry[...] = jnp.zeros_like(carry)

    win = rows + 2 * LRU_HALO

    def gates(cur, prev, nxt, chunk, d, a_s, b_s):
        p = prev[...].astype(jnp.float32) * (chunk > 0).astype(jnp.float32)
        q = nxt[...].astype(jnp.float32) * (chunk < n - 1).astype(jnp.float32)
        w = jnp.concatenate([p, cur[...].astype(jnp.float32), q], axis=0)
        taps = (pltpu.roll(w, 2, 0), pltpu.roll(w, 1, 0), w, pltpu.roll(w, win - 1, 0))
        u = cb_ref[...]
        for k in range(CONV_W):
            u = u + taps[k][LRU_HALO:LRU_HALO + rows] * cw_ref[k:k + 1, :]
        lam = lam_ref[d]
        softplus = jnp.maximum(-lam, 0.0) + jnp.log(1.0 + jnp.exp(-jnp.abs(lam)))
        half_decay = (-0.5 * LRU_C * math.log2(math.e)) * softplus
        for blk in range(LRU_CH // LRU_BLOCK):
            lo = blk * LRU_BLOCK
            ub = u[:, lo:lo + LRU_BLOCK]
            g = jnp.dot(ub.astype(jnp.bfloat16), wg_ref[d, blk],
                        preferred_element_type=jnp.float32) + gb_ref[d, blk]
            hd = half_decay[:, lo:lo + LRU_BLOCK]
            a = jnp.exp2(hd * jnp.tanh(g[:, :LRU_BLOCK]) + hd)
            half_u = 0.5 * ub
            iu = half_u * jnp.tanh(g[:, LRU_BLOCK:]) + half_u
            y = 1.0 - a * a
            b = (y * lax.rsqrt(jnp.maximum(y, RSQRT_FLOOR))) * iu
            for half in range(LRU_BLOCK // LANES):
                base = (blk * (LRU_BLOCK // LANES) + half) * pitch
                a_s[base:base + rows, :] = a[:, half * LANES:(half + 1) * LANES]
                b_s[base:base + rows, :] = b[:, half * LANES:(half + 1) * LANES]

    gates(cur_f, prev_f, next_f, s, 0, a_f, b_f)
    gates(cur_r, prev_r, next_r, n - 1 - s, 1, a_r, b_r)

    def block(a_s, b_s, h_s, h, t0, sign):
        idx = [pl.ds(t0 + sign * j, LRU_LANE_GROUPS, stride=pitch) for j in range(LRU_STEP)]
        a = [a_s[i, :] for i in idx]
        b = [b_s[i, :] for i in idx]
        a01, b01 = a[1] * a[0], a[1] * b[0] + b[1]
        a23, b23 = a[3] * a[2], a[3] * b[2] + b[3]
        a03, b03 = a23 * a01, a23 * b01 + b23
        h0 = a[0] * h + b[0]
        h1 = a01 * h + b01
        h3 = a03 * h + b03
        h2 = a[2] * h1 + b[2]
        for i, v in zip(idx, (h0, h1, h2, h3)):
            h_s[i, :] = v
        return h3

    blocks_per_iter = 8

    def body(it, hs):
        hf, hr = hs
        for j in range(blocks_per_iter):
            t = (it * blocks_per_iter + j) * LRU_STEP
            hf = block(a_f, b_f, h_f, hf, t, 1)
            hr = block(a_r, b_r, h_r, hr, rows - 1 - t, -1)
        return hf, hr

    hf, hr = lax.fori_loop(0, rows // (LRU_STEP * blocks_per_iter), body, (carry[0], carry[1]))
    carry[0] = hf
    carry[1] = hr

    for grp in range(LRU_LANE_GROUPS):
        base = grp * pitch
        hf_ref[:, grp * LANES:(grp + 1) * LANES] = h_f[base:base + rows, :].astype(hf_ref.dtype)
        hr_ref[:, grp * LANES:(grp + 1) * LANES] = h_r[base:base + rows, :].astype(hr_ref.dtype)


def _lru_mixer(u, conv_w, conv_b, wg, gb, lam, *, batch, seq, rows):
    T = u.shape[0]
    rows = min(rows, seq)
    n = seq // rows
    pitch = rows + SUBLANES
    hb = rows // LRU_HALO
    last_halo = T // LRU_HALO - 1

    def cur_map(rev):
        def f(b, c, s):
            chunk = (n - 1 - s) if rev else s
            return (b * n + chunk, c)
        return f

    def prev_map(rev):
        def f(b, c, s):
            chunk = (n - 1 - s) if rev else s
            return (jnp.maximum((b * n + chunk) * hb - 1, 0), c)
        return f

    def next_map(rev):
        def f(b, c, s):
            chunk = (n - 1 - s) if rev else s
            return (jnp.minimum((b * n + chunk + 1) * hb, last_halo), c)
        return f

    blocks = LRU_CH // LRU_BLOCK
    in_specs = []
    for rev in (False, True):
        in_specs += [
            pl.BlockSpec((rows, LRU_CH), cur_map(rev)),
            pl.BlockSpec((LRU_HALO, LRU_CH), prev_map(rev)),
            pl.BlockSpec((LRU_HALO, LRU_CH), next_map(rev)),
        ]
    in_specs += [
        pl.BlockSpec((CONV_W, LRU_CH), lambda b, c, s: (0, c)),
        pl.BlockSpec((1, LRU_CH), lambda b, c, s: (0, c)),
        pl.BlockSpec((2, blocks, LRU_BLOCK, 2 * LRU_BLOCK), lambda b, c, s: (0, c, 0, 0)),
        pl.BlockSpec((2, blocks, 1, 2 * LRU_BLOCK), lambda b, c, s: (0, c, 0, 0)),
        pl.BlockSpec((2, 1, LRU_CH), lambda b, c, s: (0, 0, c)),
    ]
    slab = pltpu.VMEM((LRU_LANE_GROUPS * pitch, LANES), jnp.float32)
    out = jax.ShapeDtypeStruct((T, MIX_WIDTH), jnp.bfloat16)
    return pl.pallas_call(
        functools.partial(_lru_kernel, rows=rows, pitch=pitch),
        grid=(batch, MIX_WIDTH // LRU_CH, n),
        in_specs=in_specs,
        out_specs=[pl.BlockSpec((rows, LRU_CH), cur_map(False)),
                   pl.BlockSpec((rows, LRU_CH), cur_map(True))],
        out_shape=[out, out],
        scratch_shapes=[slab, slab, slab, slab, slab, slab,
                        pltpu.VMEM((2, LRU_LANE_GROUPS, LANES), jnp.float32)],
        compiler_params=_params("parallel", "parallel", "arbitrary"),
        name="rglru",
    )(u, u, u, u, u, u, conv_w, conv_b.reshape(1, MIX_WIDTH), wg, gb,
      lam.reshape(2, 1, MIX_WIDTH))


def _mla_q_kernel(cq_ref, g_ref, wa_ref, wb_ref, ct_ref, st_ref, o_ref, h_ref, *, heads):
    @pl.when(pl.program_id(1) == 0)
    def _():
        x = cq_ref[...].astype(jnp.float32)
        ms = jnp.mean(x * x, axis=-1, keepdims=True)
        h_ref[...] = ((x * lax.rsqrt(ms + EPS)) * g_ref[...]).astype(h_ref.dtype)

    h = h_ref[...]
    qa = jnp.dot(h, wa_ref[...], preferred_element_type=jnp.float32)
    qb = jnp.dot(h, wb_ref[...], preferred_element_type=jnp.float32)
    scale = (QK_NOPE + QK_ROPE) ** -0.5 * math.log2(math.e)
    ct = ct_ref[...] * scale
    st = st_ref[...] * scale
    for hh in range(heads):
        lo = hh * QK_PAD
        o_ref[:, lo:lo + LANES] = (qa[:, lo:lo + LANES] * scale).astype(o_ref.dtype)
        rope = qa[:, lo + LANES:lo + 2 * LANES] * ct + qb[:, hh * LANES:(hh + 1) * LANES] * st
        o_ref[:, lo + LANES:lo + 2 * LANES] = rope.astype(o_ref.dtype)


def _mla_q(u, q_norm, wa, wb, ct, st, *, seq, tm, heads):
    T = u.shape[0]
    tm = min(tm, seq)
    per_seq = seq // tm
    cq_blk = B_CQ_OFF // Q_LORA
    return pl.pallas_call(
        functools.partial(_mla_q_kernel, heads=heads),
        grid=(T // tm, MLA_HEADS // heads),
        in_specs=[
            pl.BlockSpec((tm, Q_LORA), lambda i, j: (i, cq_blk)),
            pl.BlockSpec((1, Q_LORA), lambda i, j: (0, 0)),
            pl.BlockSpec((Q_LORA, heads * QK_PAD), lambda i, j: (0, j)),
            pl.BlockSpec((Q_LORA, heads * LANES), lambda i, j: (0, j)),
            pl.BlockSpec((tm, LANES), lambda i, j: (i % per_seq, 0)),
            pl.BlockSpec((tm, LANES), lambda i, j: (i % per_seq, 0)),
        ],
        out_specs=pl.BlockSpec((tm, heads * QK_PAD), lambda i, j: (i, j)),
        out_shape=jax.ShapeDtypeStruct((T, MLA_HEADS * QK_PAD), jnp.bfloat16),
        scratch_shapes=[pltpu.VMEM((tm, Q_LORA), jnp.bfloat16)],
        compiler_params=_params("parallel", "arbitrary"),
        name="mla_q_proj",
    )(u, q_norm.reshape(1, Q_LORA), wa, wb, ct, st)


def _mla_kv_kernel(ckv_ref, g_ref, wk_ref, wvt_ref, kr_ref, tab_ref, k_ref, vt_ref):
    x = ckv_ref[...].astype(jnp.float32)
    ms = jnp.mean(x * x, axis=-1, keepdims=True)
    h = ((x * lax.rsqrt(ms + EPS)) * g_ref[...]).astype(jnp.bfloat16)
    kn = jnp.dot(h, wk_ref[...], preferred_element_type=jnp.float32)
    vt = lax.dot_general(wvt_ref[...], h, (((1,), (1,)), ((), ())),
                         preferred_element_type=jnp.float32).astype(vt_ref.dtype)
    ones = jnp.ones((VT_ROWS - V_DIM, vt.shape[1]), vt_ref.dtype)
    for hh in range(MLA_HEADS):
        vt_ref[hh * VT_ROWS:hh * VT_ROWS + V_DIM, :] = vt[hh * V_DIM:(hh + 1) * V_DIM, :]
        vt_ref[hh * VT_ROWS + V_DIM:(hh + 1) * VT_ROWS, :] = ones
    t = kr_ref[...].astype(jnp.float32) * tab_ref[...]
    rot = t + pltpu.roll(t, QK_ROPE, 1)
    lane = lax.broadcasted_iota(jnp.int32, rot.shape, 1)
    rot = jnp.where(lane < QK_ROPE, rot, jnp.where(lane == QK_ROPE, 1.0, 0.0)).astype(k_ref.dtype)
    for hh in range(MLA_HEADS):
        k_ref[:, hh * QK_PAD:hh * QK_PAD + LANES] = kn[:, hh * LANES:(hh + 1) * LANES].astype(k_ref.dtype)
        k_ref[:, hh * QK_PAD + LANES:(hh + 1) * QK_PAD] = rot


def _mla_kv(u, kv_norm, wk, wvt, tab, *, seq, tm):
    T = u.shape[0]
    tm = min(tm, seq)
    per_seq = seq // tm
    ckv_blk = B_CKV_OFF // KV_LORA
    kr_blk = B_KR_OFF // LANES
    return pl.pallas_call(
        _mla_kv_kernel,
        grid=(T // tm,),
        in_specs=[
            pl.BlockSpec((tm, KV_LORA), lambda i: (i, ckv_blk)),
            pl.BlockSpec((1, KV_LORA), lambda i: (0, 0)),
            pl.BlockSpec((KV_LORA, MLA_HEADS * QK_NOPE), lambda i: (0, 0)),
            pl.BlockSpec((MLA_HEADS * V_DIM, KV_LORA), lambda i: (0, 0)),
            pl.BlockSpec((tm, LANES), lambda i: (i, kr_blk)),
            pl.BlockSpec((tm, LANES), lambda i: (i % per_seq, 0)),
        ],
        out_specs=[pl.BlockSpec((tm, MLA_HEADS * QK_PAD), lambda i: (i, 0)),
                   pl.BlockSpec((MLA_HEADS * VT_ROWS, tm), lambda i: (0, i))],
        out_shape=[jax.ShapeDtypeStruct((T, MLA_HEADS * QK_PAD), jnp.bfloat16),
                   jax.ShapeDtypeStruct((MLA_HEADS * VT_ROWS, T), jnp.bfloat16)],
        compiler_params=_params("parallel"),
        name="mla_kv_proj",
    )(u, kv_norm.reshape(1, KV_LORA), wk, wvt, u, tab)


def _flash_kernel(q_ref, k_ref, vt_ref, o_ref, qt_buf, acc_ref, *, tk, n_kv):
    bf = jnp.bfloat16
    qt = q_ref[...].astype(jnp.float32).T.astype(bf)
    qt_buf[0] = qt
    qt_buf[1] = qt
    refs = {}
    m_true = worst = None
    for j in range(n_kv):
        k = k_ref[j * tk:(j + 1) * tk, :]
        s = jnp.dot(k, qt_buf[j % 2], preferred_element_type=jnp.float32)
        if j == 0:
            cmax = jnp.max(s, axis=0, keepdims=True)
            refs[0] = cmax
            p = jnp.exp2((s - cmax).astype(bf))
            m_true = cmax
        else:
            p = jnp.exp2(s.astype(bf))
            pmax = jnp.max(p, axis=0, keepdims=True).astype(jnp.float32)
            m_true = jnp.maximum(m_true, refs[j] + jnp.log2(pmax))
            worst = pmax if worst is None else jnp.maximum(worst, pmax)
        for t in ([1, 2] if j == 0 else [j + 2]):
            if t < n_kv:
                nref = m_true.astype(bf)
                qt_buf[t % 2, AUG_ROW:AUG_ROW + BF16_ROWS, :] = jnp.broadcast_to(
                    -nref, (BF16_ROWS, nref.shape[1]))
                refs[t] = nref.astype(jnp.float32)
        pv = jnp.dot(vt_ref[:, j * tk:(j + 1) * tk], p, preferred_element_type=jnp.float32)
        if j == 0:
            acc_ref[...] = pv
        else:
            acc_ref[...] = jnp.exp2(refs[j - 1] - refs[j]) * acc_ref[...] + pv

    def finish():
        out = acc_ref[:V_DIM, :] / acc_ref[V_DIM:V_DIM + 1, :]
        o_ref[...] = out.T.astype(o_ref.dtype)

    finish()
    if worst is None:
        return

    @pl.when(jnp.logical_not(jnp.max(worst) <= 2.0 ** FAST_PATH_LIMIT))
    def _():
        m_run = None
        for j in range(n_kv):
            s = jnp.dot(k_ref[j * tk:(j + 1) * tk, :], qt, preferred_element_type=jnp.float32)
            cmax = jnp.max(s, axis=0, keepdims=True)
            m_new = cmax if j == 0 else jnp.maximum(m_run, cmax)
            p = jnp.exp2((s - m_new).astype(bf))
            pv = jnp.dot(vt_ref[:, j * tk:(j + 1) * tk], p, preferred_element_type=jnp.float32)
            if j == 0:
                acc_ref[...] = pv
            else:
                acc_ref[...] = jnp.exp2(m_run - m_new) * acc_ref[...] + pv
            m_run = m_new
        finish()


def _flash_attention(q, k, vt, *, batch, seq, tq, tk):
    T = q.shape[0]
    tq = min(tq, seq)
    tk = min(tk, seq)
    nq = seq // tq
    return pl.pallas_call(
        functools.partial(_flash_kernel, tk=tk, n_kv=seq // tk),
        grid=(batch, MLA_HEADS, nq),
        in_specs=[
            pl.BlockSpec((tq, QK_PAD), lambda b, h, i: (b * nq + i, h)),
            pl.BlockSpec((seq, QK_PAD), lambda b, h, i: (b, h)),
            pl.BlockSpec((VT_ROWS, seq), lambda b, h, i: (h, b)),
        ],
        out_specs=pl.BlockSpec((tq, V_DIM), lambda b, h, i: (b * nq + i, h)),
        out_shape=jax.ShapeDtypeStruct((T, MLA_HEADS * V_DIM), jnp.bfloat16),
        scratch_shapes=[pltpu.VMEM((2, QK_PAD, tq), jnp.bfloat16),
                        pltpu.VMEM((VT_ROWS, tq), jnp.float32)],
        compiler_params=_params("parallel", "parallel", "arbitrary"),
        name="mla_flash_attention",
    )(q, k, vt)


def _pool_kernel(cur_ref, prev_ref, next_ref, w_ref, sc_ref, o_ref, *, seq, tm):
    i = pl.program_id(1)
    n = pl.num_programs(1)
    win = tm + 2 * POOL_HALO
    keep_prev = (i > 0).astype(jnp.float32)
    keep_next = (i < n - 1).astype(jnp.float32)
    t = i * tm + lax.broadcasted_iota(jnp.int32, (tm, 1), 0)
    for g, w in enumerate(POOL_WINDOWS):
        lo = g * POOL_GROUP
        x = cur_ref[:, lo:lo + POOL_GROUP].astype(jnp.float32)
        p = prev_ref[:, lo:lo + POOL_GROUP].astype(jnp.float32) * keep_prev
        q = next_ref[:, lo:lo + POOL_GROUP].astype(jnp.float32) * keep_next
        run = jnp.concatenate([p, x, q], axis=0)
        span = 1
        while span < w:
            run = run + pltpu.roll(run, span, 0)
            span *= 2
        ahead = w - w // 2 - 1
        if ahead:
            run = pltpu.roll(run, win - ahead, 0)
        total = run[POOL_HALO:POOL_HALO + tm]
        count = jnp.minimum(t + (w - w // 2), seq) - jnp.maximum(t - w // 2, 0)
        pooled = total / count.astype(jnp.float32) - x
        y = jnp.dot(pooled.astype(jnp.bfloat16), w_ref[g], preferred_element_type=jnp.float32)
        o_ref[:, lo:lo + POOL_GROUP] = (y * sc_ref[:, lo:lo + POOL_GROUP]).astype(o_ref.dtype)


def _pool_mixer(u, w_group, scale, *, batch, seq, tm):
    T = u.shape[0]
    tm = min(tm, seq)
    n = seq // tm
    hb = tm // POOL_HALO
    last_halo = T // POOL_HALO - 1
    return pl.pallas_call(
        functools.partial(_pool_kernel, seq=seq, tm=tm),
        grid=(batch, n),
        in_specs=[
            pl.BlockSpec((tm, MIX_WIDTH), lambda b, i: (b * n + i, 0)),
            pl.BlockSpec((POOL_HALO, MIX_WIDTH),
                         lambda b, i: (jnp.maximum((b * n + i) * hb - 1, 0), 0)),
            pl.BlockSpec((POOL_HALO, MIX_WIDTH),
                         lambda b, i: (jnp.minimum((b * n + i + 1) * hb, last_halo), 0)),
            pl.BlockSpec((len(POOL_WINDOWS), POOL_GROUP, POOL_GROUP), lambda b, i: (0, 0, 0)),
            pl.BlockSpec((1, MIX_WIDTH), lambda b, i: (0, 0)),
        ],
        out_specs=pl.BlockSpec((tm, MIX_WIDTH), lambda b, i: (b * n + i, 0)),
        out_shape=jax.ShapeDtypeStruct((T, MIX_WIDTH), jnp.bfloat16),
        compiler_params=_params("parallel", "parallel"),
        name="pool_mixer",
    )(u, u, u, w_group, scale.reshape(1, MIX_WIDTH))


def _rope_tables(seq):
    inv_freq = 1.0 / (ROPE_THETA ** (jnp.arange(0, QK_ROPE, 2, dtype=jnp.float32) / QK_ROPE))
    ang = jnp.arange(seq, dtype=jnp.float32)[:, None] * inv_freq[None, :]
    cos, sin = jnp.cos(ang), jnp.sin(ang)
    zeros = jnp.zeros((seq, LANES - QK_ROPE), jnp.float32)
    ct = jnp.concatenate([cos, cos, zeros], axis=-1)
    st = jnp.concatenate([-sin, sin, zeros], axis=-1)
    tab = jnp.concatenate([cos, cos, -sin, sin], axis=-1)
    return ct, st, tab


def _swap_halves(w):
    half = QK_ROPE // 2
    return jnp.concatenate([w[..., half:], w[..., :half]], axis=-1)


def _prep_b(b_w_in, b_w_q_up, b_w_kv_up):
    bf = jnp.bfloat16
    K = b_w_in.shape[0]
    o1 = Q_LORA
    o2 = o1 + KV_LORA
    o3 = o2 + QK_ROPE
    o4 = o3 + XA_WIDTH
    w_cq, w_ckv, w_kr, w_xq, w_gate = (b_w_in[:, :o1], b_w_in[:, o1:o2], b_w_in[:, o2:o3],
                                       b_w_in[:, o3:o4], b_w_in[:, o4:])
    w_in = jnp.concatenate([w_gate, w_xq, w_cq, w_ckv, w_kr, _swap_halves(w_kr),
                            jnp.zeros((K, LANES), b_w_in.dtype)], axis=-1).astype(bf)
    wq = b_w_q_up.reshape(Q_LORA, MLA_HEADS, QK_NOPE + QK_ROPE)
    q_nope, q_rope = wq[..., :QK_NOPE], wq[..., QK_NOPE:]
    pad = jnp.zeros((Q_LORA, MLA_HEADS, LANES - QK_ROPE), wq.dtype)
    wa = jnp.concatenate([q_nope, q_rope, pad], axis=-1).reshape(Q_LORA, MLA_HEADS * QK_PAD).astype(bf)
    wb = jnp.concatenate([_swap_halves(q_rope), pad], axis=-1).reshape(Q_LORA, MLA_HEADS * LANES).astype(bf)
    wkv = b_w_kv_up.reshape(KV_LORA, MLA_HEADS, QK_NOPE + V_DIM)
    wk = wkv[..., :QK_NOPE].reshape(KV_LORA, MLA_HEADS * QK_NOPE).astype(bf)
    wvt = wkv[..., QK_NOPE:].reshape(KV_LORA, MLA_HEADS * V_DIM).T.astype(bf)
    return w_in, wa, wb, wk, wvt


def _prep_lru(gate_w, gate_b):
    wg = (0.5 * jnp.concatenate([gate_w[:, 0], gate_w[:, 1]], axis=-1)).astype(jnp.bfloat16)
    gb = gate_b.reshape(2, 2, LRU_BLOCKS, 1, LRU_BLOCK)
    gb = 0.5 * jnp.concatenate([gb[:, 0], gb[:, 1]], axis=-1)
    return wg, gb


def _trunk(x, mem, w):
    batch, seq, _ = x.shape
    T = batch * seq
    x = x.reshape(T, D_MODEL)
    mem = mem.reshape(batch * MEM_LEN, D_MODEL)
    ct, st, tab = _rope_tables(seq)
    depth = w["norm_pre"].shape[0]
    for i in range(depth):
        kind, j = i % N_MIXERS, i // N_MIXERS
        kv = _norm_matmul(mem, w["norm_mem"][i], w["w_mem_kv"][i], tm=512, tn=1024)
        if kind == 0:
            u = _norm_matmul(x, w["norm_pre"][i], w["a_w_in"][j], tm=1024, tn=2048)
            mixes = _lru_mixer(u, w["a_conv_w"][j], w["a_conv_b"][j], w["a_wg"][j], w["a_gb"][j],
                               w["a_lambda"][j], batch=batch, seq=seq, rows=256)
            xq_off, gate_off = MIX_WIDTH, BRANCH
        elif kind == 1:
            u = _norm_matmul(x, w["norm_pre"][i], w["b_w_in"][j], tm=1024, tn=2048)
            q = _mla_q(u, w["b_q_norm"][j], w["b_wa"][j], w["b_wb"][j], ct, st,
                       seq=seq, tm=1024, heads=8)
            k, vt = _mla_kv(u, w["b_kv_norm"][j], w["b_wk"][j], w["b_wvt"][j], tab, seq=seq, tm=512)
            mixes = [_flash_attention(q, k, vt, batch=batch, seq=seq, tq=1024, tk=512)]
            xq_off, gate_off = B_XQ_OFF, B_GATE_OFF
        else:
            u = _norm_matmul(x, w["norm_pre"][i], w["c_w_in"][j], tm=1024, tn=2048)
            mixes = [_pool_mixer(u, w["c_w_group"][j], w["c_scale"][j], batch=batch, seq=seq, tm=512)]
            xq_off, gate_off = MIX_WIDTH, BRANCH
        x = _out_block(mixes, u, xq_off, gate_off, kv, w["w_out"][i], x, w["norm_post"][i],
                       seq=seq, tm=256)
    return x.reshape(batch, seq, D_MODEL)


def kernel(x_prompt, x_sample, mem_prompt, mem_sample, norm_pre, norm_post, norm_mem, w_mem_kv, w_out,
           a_w_in, a_conv_w, a_conv_b, a_gate_w, a_gate_b, a_lambda,
           b_w_in, b_q_norm, b_kv_norm, b_w_q_up, b_w_kv_up,
           c_w_in, c_w_group, c_scale):
    bf = jnp.bfloat16
    b_parts = [_prep_b(b_w_in[j], b_w_q_up[j], b_w_kv_up[j]) for j in range(b_w_in.shape[0])]
    lru_parts = [_prep_lru(a_gate_w[j], a_gate_b[j]) for j in range(a_gate_w.shape[0])]
    def per_layer(stack):
        return [stack[j].astype(bf) for j in range(stack.shape[0])]

    w = dict(
        norm_pre=norm_pre, norm_post=norm_post, norm_mem=norm_mem,
        w_mem_kv=per_layer(w_mem_kv), w_out=per_layer(w_out),
        a_w_in=per_layer(a_w_in), a_conv_w=a_conv_w, a_conv_b=a_conv_b,
        a_wg=[p[0] for p in lru_parts], a_gb=[p[1] for p in lru_parts], a_lambda=a_lambda,
        b_w_in=[p[0] for p in b_parts], b_q_norm=b_q_norm, b_kv_norm=b_kv_norm,
        b_wa=[p[1] for p in b_parts], b_wb=[p[2] for p in b_parts],
        b_wk=[p[3] for p in b_parts], b_wvt=[p[4] for p in b_parts],
        c_w_in=per_layer(c_w_in), c_w_group=per_layer(c_w_group), c_scale=c_scale,
    )
    return (_trunk(x_prompt, mem_prompt, w), _trunk(x_sample, mem_sample, w))
```

```python
import functools
import math

import jax
import jax.numpy as jnp
from jax import lax
from jax.experimental import pallas as pl
from jax.experimental.pallas import tpu as pltpu

D_MODEL = 2048
MIX_WIDTH = 3 * D_MODEL // 2
XA_HEADS = 4
XA_HEAD_DIM = D_MODEL // 8
XA_WIDTH = XA_HEADS * XA_HEAD_DIM
BRANCH = MIX_WIDTH + XA_WIDTH
MEM_LEN = 256
N_MIXERS = 3
CONV_W = 4
CONV_LEFT = 2
LRU_BLOCKS = 12
LRU_BLOCK = MIX_WIDTH // LRU_BLOCKS
LRU_C = 8.0
QK_NOPE = 128
QK_ROPE = 64
V_DIM = 128
MLA_HEADS = MIX_WIDTH // V_DIM
Q_LORA = D_MODEL // 4
KV_LORA = D_MODEL // 8
ROPE_THETA = 10000.0
POOL_WINDOWS = (2, 4, 8, 16)
POOL_GROUP = MIX_WIDTH // len(POOL_WINDOWS)
EPS = 1e-6

LANES = 128
SUBLANES = 8
BF16_ROWS = 16
QK_PAD = 256
AUG_ROW = QK_NOPE + QK_ROPE
FAST_PATH_LIMIT = 64.0
VT_ROWS = V_DIM + BF16_ROWS
VMEM_LIMIT = 56 * 1024 * 1024

B_GATE_OFF = 0
B_XQ_OFF = BRANCH
B_CQ_OFF = B_XQ_OFF + XA_WIDTH
B_CKV_OFF = B_CQ_OFF + Q_LORA
B_KR_OFF = B_CKV_OFF + KV_LORA
B_IN_WIDTH = B_KR_OFF + 2 * QK_ROPE + LANES

LRU_CH = 1024
LRU_LANE_GROUPS = LRU_CH // LANES
LRU_HALO = BF16_ROWS
LRU_STEP = 4
RSQRT_FLOOR = 1e-30

POOL_HALO = BF16_ROWS


def _params(*sem, flags=None):
    return pltpu.CompilerParams(dimension_semantics=sem, vmem_limit_bytes=VMEM_LIMIT, flags=flags)


def _resident(block_shape, index_map):
    return pl.BlockSpec(block_shape, index_map, pipeline_mode=pl.Buffered(1))


def _sigmoid(x):
    return 1.0 / (1.0 + jnp.exp(-x))


def _norm_matmul_kernel(x_ref, g_ref, w_ref, o_ref, h_ref):
    @pl.when(pl.program_id(1) == 0)
    def _():
        x = x_ref[...].astype(jnp.float32)
        ms = jnp.mean(x * x, axis=-1, keepdims=True)
        h_ref[...] = ((x * lax.rsqrt(ms + EPS)) * g_ref[...]).astype(h_ref.dtype)

    o_ref[...] = jnp.dot(h_ref[...], w_ref[...],
                         preferred_element_type=jnp.float32).astype(o_ref.dtype)


def _norm_matmul(x, g, w, layer, *, tm, tn):
    T, K = x.shape
    N = w.shape[2]
    tm = min(tm, T)
    return pl.pallas_call(
        _norm_matmul_kernel,
        grid=(T // tm, N // tn),
        in_specs=[
            pl.BlockSpec((tm, K), lambda i, j: (i, 0)),
            pl.BlockSpec((1, K), lambda i, j: (0, 0)),
            pl.BlockSpec((None, K, tn), lambda i, j: (layer, 0, j)),
        ],
        out_specs=pl.BlockSpec((tm, tn), lambda i, j: (i, j)),
        out_shape=jax.ShapeDtypeStruct((T, N), jnp.bfloat16),
        scratch_shapes=[pltpu.VMEM((tm, K), jnp.bfloat16)],
        compiler_params=_params("parallel", "arbitrary"),
        name="norm_matmul",
    )(x, g.reshape(1, K), w)


def _out_kernel(*refs, n_mix):
    mix_refs = refs[:n_mix]
    xq_ref, gate_ref, kv_ref, w_ref, x_ref, g_ref, o_ref = refs[n_mix:]

    mix = mix_refs[0][...]
    for r in mix_refs[1:]:
        mix = mix + r[...]

    xq = xq_ref[...]
    scale = XA_HEAD_DIM ** -0.5 * math.log2(math.e)
    xa = []
    for h in range(XA_HEADS):
        lo = h * XA_HEAD_DIM
        q = xq[:, lo:lo + XA_HEAD_DIM]
        k = kv_ref[:, lo:lo + XA_HEAD_DIM]
        v = kv_ref[:, XA_WIDTH + lo:XA_WIDTH + lo + XA_HEAD_DIM]
        s = lax.dot_general(q, k, (((1,), (1,)), ((), ())), preferred_element_type=jnp.float32)
        m = jnp.max(s, axis=-1, keepdims=True)
        p = jnp.exp2((s - m) * scale)
        l = jnp.sum(p, axis=-1, keepdims=True)
        p = (p / l).astype(jnp.bfloat16)
        xa.append(jnp.dot(p, v, preferred_element_type=jnp.float32))

    half_gate = 0.5 * gate_ref[...].astype(jnp.float32)
    act = (half_gate * jnp.tanh(half_gate) + half_gate).astype(jnp.bfloat16)
    acc = jnp.dot(mix * act[:, :MIX_WIDTH], w_ref[:MIX_WIDTH, :], preferred_element_type=jnp.float32)
    xa = jnp.concatenate([a.astype(jnp.bfloat16) for a in xa], axis=-1)
    acc = acc + jnp.dot(xa * act[:, MIX_WIDTH:], w_ref[MIX_WIDTH:, :],
                        preferred_element_type=jnp.float32)

    ms = jnp.mean(acc * acc, axis=-1, keepdims=True)
    o_ref[...] = x_ref[...] + (acc * lax.rsqrt(ms + EPS)) * g_ref[...]


def _out_block(mixes, u, xq_off, gate_off, kv, w_out, layer, x, g_post, *, seq, tm):
    T = x.shape[0]
    tm = min(tm, seq)
    per_seq = seq // tm
    xq_blk = xq_off // XA_WIDTH
    gate_blk = gate_off // BRANCH
    mix_spec = pl.BlockSpec((tm, MIX_WIDTH), lambda i: (i, 0))
    return pl.pallas_call(
        functools.partial(_out_kernel, n_mix=len(mixes)),
        grid=(T // tm,),
        in_specs=[mix_spec] * len(mixes) + [
            pl.BlockSpec((tm, XA_WIDTH), lambda i: (i, xq_blk)),
            pl.BlockSpec((tm, BRANCH), lambda i: (i, gate_blk)),
            pl.BlockSpec((MEM_LEN, 2 * XA_WIDTH), lambda i: (i // per_seq, 0)),
            _resident((None, BRANCH, D_MODEL), lambda i: (layer, 0, 0)),
            pl.BlockSpec((tm, D_MODEL), lambda i: (i, 0)),
            pl.BlockSpec((1, D_MODEL), lambda i: (0, 0)),
        ],
        out_specs=pl.BlockSpec((tm, D_MODEL), lambda i: (i, 0)),
        out_shape=jax.ShapeDtypeStruct((T, D_MODEL), jnp.float32),
        compiler_params=_params("parallel"),
        name="xattn_gate_outproj",
    )(*mixes, u, u, kv, w_out, x, g_post.reshape(1, D_MODEL))


def _lru_kernel(cur_f, prev_f, next_f, cur_r, prev_r, next_r, cw_ref, cb_ref, wg_ref,
                gb_ref, lam_ref, hf_ref, hr_ref, a_f, b_f, h_f, a_r, b_r, h_r, carry, *, rows, pitch):
    s = pl.program_id(2)
    n = pl.num_programs(2)

    @pl.when(s == 0)
    def _():
        carry[...] = jnp.zeros_like(carry)

    win = rows + 2 * LRU_HALO

    def gates(cur, prev, nxt, chunk, d, a_s, b_s):
        p = prev[...].astype(jnp.float32) * (chunk > 0).astype(jnp.float32)
        q = nxt[...].astype(jnp.float32) * (chunk < n - 1).astype(jnp.float32)
        w = jnp.concatenate([p, cur[...].astype(jnp.float32), q], axis=0)
        taps = (pltpu.roll(w, 2, 0), pltpu.roll(w, 1, 0), w, pltpu.roll(w, win - 1, 0))
        u = cb_ref[...]
        for k in range(CONV_W):
            u = u + taps[k][LRU_HALO:LRU_HALO + rows] * cw_ref[k:k + 1, :]
        lam = lam_ref[d]
        softplus = jnp.maximum(-lam, 0.0) + jnp.log(1.0 + jnp.exp(-jnp.abs(lam)))
        half_decay = (-0.5 * LRU_C * math.log2(math.e)) * softplus
        for blk in range(LRU_CH // LRU_BLOCK):
            lo = blk * LRU_BLOCK
            ub = u[:, lo:lo + LRU_BLOCK]
            g = jnp.dot(ub.astype(jnp.bfloat16), wg_ref[d, blk],
                        preferred_element_type=jnp.float32) + gb_ref[d, blk]
            hd = half_decay[:, lo:lo + LRU_BLOCK]
            a = jnp.exp2(hd * jnp.tanh(g[:, :LRU_BLOCK]) + hd)
            half_u = 0.5 * ub
            iu = half_u * jnp.tanh(g[:, LRU_BLOCK:]) + half_u
            y = 1.0 - a * a
            b = (y * lax.rsqrt(jnp.maximum(y, RSQRT_FLOOR))) * iu
            for half in range(LRU_BLOCK // LANES):
                base = (blk * (LRU_BLOCK // LANES) + half) * pitch
                a_s[base:base + rows, :] = a[:, half * LANES:(half + 1) * LANES]
                b_s[base:base + rows, :] = b[:, half * LANES:(half + 1) * LANES]

    gates(cur_f, prev_f, next_f, s, 0, a_f, b_f)
    gates(cur_r, prev_r, next_r, n - 1 - s, 1, a_r, b_r)

    def block(a_s, b_s, h_s, h, t0, sign):
        idx = [pl.ds(t0 + sign * j, LRU_LANE_GROUPS, stride=pitch) for j in range(LRU_STEP)]
        a = [a_s[i, :] for i in idx]
        b = [b_s[i, :] for i in idx]
        a01, b01 = a[1] * a[0], a[1] * b[0] + b[1]
        a23, b23 = a[3] * a[2], a[3] * b[2] + b[3]
        a03, b03 = a23 * a01, a23 * b01 + b23
        h0 = a[0] * h + b[0]
        h1 = a01 * h + b01
        h3 = a03 * h + b03
        h2 = a[2] * h1 + b[2]
        for i, v in zip(idx, (h0, h1, h2, h3)):
            h_s[i, :] = v
        return h3

    blocks_per_iter = 8

    def body(it, hs):
        hf, hr = hs
        for j in range(blocks_per_iter):
            t = (it * blocks_per_iter + j) * LRU_STEP
            hf = block(a_f, b_f, h_f, hf, t, 1)
            hr = block(a_r, b_r, h_r, hr, rows - 1 - t, -1)
        return hf, hr

    hf, hr = lax.fori_loop(0, rows // (LRU_STEP * blocks_per_iter), body, (carry[0], carry[1]))
    carry[0] = hf
    carry[1] = hr

    for grp in range(LRU_LANE_GROUPS):
        base = grp * pitch
        hf_ref[:, grp * LANES:(grp + 1) * LANES] = h_f[base:base + rows, :].astype(hf_ref.dtype)
        hr_ref[:, grp * LANES:(grp + 1) * LANES] = h_r[base:base + rows, :].astype(hr_ref.dtype)


def _lru_mixer(u, conv_w, conv_b, wg, gb, lam, *, batch, seq, rows):
    T = u.shape[0]
    rows = min(rows, seq)
    n = seq // rows
    pitch = rows + SUBLANES
    hb = rows // LRU_HALO
    last_halo = T // LRU_HALO - 1

    def cur_map(rev):
        def f(b, c, s):
            chunk = (n - 1 - s) if rev else s
            return (b * n + chunk, c)
        return f

    def prev_map(rev):
        def f(b, c, s):
            chunk = (n - 1 - s) if rev else s
            return (jnp.maximum((b * n + chunk) * hb - 1, 0), c)
        return f

    def next_map(rev):
        def f(b, c, s):
            chunk = (n - 1 - s) if rev else s
            return (jnp.minimum((b * n + chunk + 1) * hb, last_halo), c)
        return f

    blocks = LRU_CH // LRU_BLOCK
    in_specs = []
    for rev in (False, True):
        in_specs += [
            pl.BlockSpec((rows, LRU_CH), cur_map(rev)),
            pl.BlockSpec((LRU_HALO, LRU_CH), prev_map(rev)),
            pl.BlockSpec((LRU_HALO, LRU_CH), next_map(rev)),
        ]
    in_specs += [
        pl.BlockSpec((CONV_W, LRU_CH), lambda b, c, s: (0, c)),
        pl.BlockSpec((1, LRU_CH), lambda b, c, s: (0, c)),
        pl.BlockSpec((2, blocks, LRU_BLOCK, 2 * LRU_BLOCK), lambda b, c, s: (0, c, 0, 0)),
        pl.BlockSpec((2, blocks, 1, 2 * LRU_BLOCK), lambda b, c, s: (0, c, 0, 0)),
        pl.BlockSpec((2, 1, LRU_CH), lambda b, c, s: (0, 0, c)),
    ]
    slab = pltpu.VMEM((LRU_LANE_GROUPS * pitch, LANES), jnp.float32)
    out = jax.ShapeDtypeStruct((T, MIX_WIDTH), jnp.bfloat16)
    return pl.pallas_call(
        functools.partial(_lru_kernel, rows=rows, pitch=pitch),
        grid=(batch, MIX_WIDTH // LRU_CH, n),
        in_specs=in_specs,
        out_specs=[pl.BlockSpec((rows, LRU_CH), cur_map(False)),
                   pl.BlockSpec((rows, LRU_CH), cur_map(True))],
        out_shape=[out, out],
        scratch_shapes=[slab, slab, slab, slab, slab, slab,
                        pltpu.VMEM((2, LRU_LANE_GROUPS, LANES), jnp.float32)],
        compiler_params=_params("parallel", "parallel", "arbitrary"),
        name="rglru",
    )(u, u, u, u, u, u, conv_w, conv_b.reshape(1, MIX_WIDTH), wg, gb,
      lam.reshape(2, 1, MIX_WIDTH))


def _mla_q_kernel(cq_ref, g_ref, wa_ref, wb_ref, ct_ref, st_ref, o_ref, h_ref, *, heads):
    @pl.when(pl.program_id(1) == 0)
    def _():
        x = cq_ref[...].astype(jnp.float32)
        ms = jnp.mean(x * x, axis=-1, keepdims=True)
        h_ref[...] = ((x * lax.rsqrt(ms + EPS)) * g_ref[...]).astype(h_ref.dtype)

    h = h_ref[...]
    qa = jnp.dot(h, wa_ref[...], preferred_element_type=jnp.float32)
    qb = jnp.dot(h, wb_ref[...], preferred_element_type=jnp.float32)
    scale = (QK_NOPE + QK_ROPE) ** -0.5 * math.log2(math.e)
    ct = ct_ref[...] * scale
    st = st_ref[...] * scale
    for hh in range(heads):
        lo = hh * QK_PAD
        o_ref[:, lo:lo + LANES] = (qa[:, lo:lo + LANES] * scale).astype(o_ref.dtype)
        rope = qa[:, lo + LANES:lo + 2 * LANES] * ct + qb[:, hh * LANES:(hh + 1) * LANES] * st
        o_ref[:, lo + LANES:lo + 2 * LANES] = rope.astype(o_ref.dtype)


def _mla_q(u, q_norm, wa, wb, ct, st, *, seq, tm, heads):
    T = u.shape[0]
    tm = min(tm, seq)
    per_seq = seq // tm
    cq_blk = B_CQ_OFF // Q_LORA
    return pl.pallas_call(
        functools.partial(_mla_q_kernel, heads=heads),
        grid=(T // tm, MLA_HEADS // heads),
        in_specs=[
            pl.BlockSpec((tm, Q_LORA), lambda i, j: (i, cq_blk)),
            pl.BlockSpec((1, Q_LORA), lambda i, j: (0, 0)),
            pl.BlockSpec((Q_LORA, heads * QK_PAD), lambda i, j: (0, j)),
            pl.BlockSpec((Q_LORA, heads * LANES), lambda i, j: (0, j)),
            pl.BlockSpec((tm, LANES), lambda i, j: (i % per_seq, 0)),
            pl.BlockSpec((tm, LANES), lambda i, j: (i % per_seq, 0)),
        ],
        out_specs=pl.BlockSpec((tm, heads * QK_PAD), lambda i, j: (i, j)),
        out_shape=jax.ShapeDtypeStruct((T, MLA_HEADS * QK_PAD), jnp.bfloat16),
        scratch_shapes=[pltpu.VMEM((tm, Q_LORA), jnp.bfloat16)],
        compiler_params=_params("parallel", "arbitrary"),
        name="mla_q_proj",
    )(u, q_norm.reshape(1, Q_LORA), wa, wb, ct, st)


def _mla_kv_kernel(ckv_ref, g_ref, wk_ref, wvt_ref, kr_ref, tab_ref, k_ref, vt_ref):
    x = ckv_ref[...].astype(jnp.float32)
    ms = jnp.mean(x * x, axis=-1, keepdims=True)
    h = ((x * lax.rsqrt(ms + EPS)) * g_ref[...]).astype(jnp.bfloat16)
    kn = jnp.dot(h, wk_ref[...], preferred_element_type=jnp.float32)
    vt = lax.dot_general(wvt_ref[...], h, (((1,), (1,)), ((), ())),
                         preferred_element_type=jnp.float32).astype(vt_ref.dtype)
    ones = jnp.ones((VT_ROWS - V_DIM, vt.shape[1]), vt_ref.dtype)
    for hh in range(MLA_HEADS):
        vt_ref[hh * VT_ROWS:hh * VT_ROWS + V_DIM, :] = vt[hh * V_DIM:(hh + 1) * V_DIM, :]
        vt_ref[hh * VT_ROWS + V_DIM:(hh + 1) * VT_ROWS, :] = ones
    t = kr_ref[...].astype(jnp.float32) * tab_ref[...]
    rot = t + pltpu.roll(t, QK_ROPE, 1)
    lane = lax.broadcasted_iota(jnp.int32, rot.shape, 1)
    rot = jnp.where(lane < QK_ROPE, rot, jnp.where(lane == QK_ROPE, 1.0, 0.0)).astype(k_ref.dtype)
    for hh in range(MLA_HEADS):
        k_ref[:, hh * QK_PAD:hh * QK_PAD + LANES] = kn[:, hh * LANES:(hh + 1) * LANES].astype(k_ref.dtype)
        k_ref[:, hh * QK_PAD + LANES:(hh + 1) * QK_PAD] = rot


def _mla_kv(u, kv_norm, wk, wvt, tab, *, seq, tm):
    T = u.shape[0]
    tm = min(tm, seq)
    per_seq = seq // tm
    ckv_blk = B_CKV_OFF // KV_LORA
    kr_blk = B_KR_OFF // LANES
    return pl.pallas_call(
        _mla_kv_kernel,
        grid=(T // tm,),
        in_specs=[
            pl.BlockSpec((tm, KV_LORA), lambda i: (i, ckv_blk)),
            pl.BlockSpec((1, KV_LORA), lambda i: (0, 0)),
            pl.BlockSpec((KV_LORA, MLA_HEADS * QK_NOPE), lambda i: (0, 0)),
            pl.BlockSpec((MLA_HEADS * V_DIM, KV_LORA), lambda i: (0, 0)),
            pl.BlockSpec((tm, LANES), lambda i: (i, kr_blk)),
            pl.BlockSpec((tm, LANES), lambda i: (i % per_seq, 0)),
        ],
        out_specs=[pl.BlockSpec((tm, MLA_HEADS * QK_PAD), lambda i: (i, 0)),
                   pl.BlockSpec((MLA_HEADS * VT_ROWS, tm), lambda i: (0, i))],
        out_shape=[jax.ShapeDtypeStruct((T, MLA_HEADS * QK_PAD), jnp.bfloat16),
                   jax.ShapeDtypeStruct((MLA_HEADS * VT_ROWS, T), jnp.bfloat16)],
        compiler_params=_params("parallel"),
        name="mla_kv_proj",
    )(u, kv_norm.reshape(1, KV_LORA), wk, wvt, u, tab)


def _flash_head(q_ref, k_ref, vt_ref, o_ref, qt_buf, acc_ref, *, tk, n_kv):
    bf = jnp.bfloat16
    qt = q_ref[...].astype(jnp.float32).T.astype(bf)
    qt_buf[0] = qt
    qt_buf[1] = qt
    refs = {}
    m_true = worst = None
    for j in range(n_kv):
        k = k_ref[j * tk:(j + 1) * tk, :]
        s = jnp.dot(k, qt_buf[j % 2], preferred_element_type=jnp.float32)
        if j == 0:
            cmax = jnp.max(s, axis=0, keepdims=True)
            refs[0] = cmax
            p = jnp.exp2((s - cmax).astype(bf))
            m_true = cmax
        else:
            p = jnp.exp2(s.astype(bf))
            pmax = jnp.max(p, axis=0, keepdims=True).astype(jnp.float32)
            m_true = jnp.maximum(m_true, refs[j] + jnp.log2(pmax))
            worst = pmax if worst is None else jnp.maximum(worst, pmax)
        for t in ([1, 2] if j == 0 else [j + 2]):
            if t < n_kv:
                nref = m_true.astype(bf)
                qt_buf[t % 2, AUG_ROW:AUG_ROW + BF16_ROWS, :] = jnp.broadcast_to(
                    -nref, (BF16_ROWS, nref.shape[1]))
                refs[t] = nref.astype(jnp.float32)
        pv = jnp.dot(vt_ref[:, j * tk:(j + 1) * tk], p, preferred_element_type=jnp.float32)
        if j == 0:
            acc_ref[...] = pv
        else:
            acc_ref[...] = jnp.exp2(refs[j - 1] - refs[j]) * acc_ref[...] + pv

    def finish():
        out = acc_ref[:V_DIM, :] / acc_ref[V_DIM:V_DIM + 1, :]
        o_ref[...] = out.T.astype(o_ref.dtype)

    finish()

    def exact():
        m_run = None
        for j in range(n_kv):
            s = jnp.dot(k_ref[j * tk:(j + 1) * tk, :], qt, preferred_element_type=jnp.float32)
            cmax = jnp.max(s, axis=0, keepdims=True)
            m_new = cmax if j == 0 else jnp.maximum(m_run, cmax)
            p = jnp.exp2((s - m_new).astype(bf))
            pv = jnp.dot(vt_ref[:, j * tk:(j + 1) * tk], p, preferred_element_type=jnp.float32)
            if j == 0:
                acc_ref[...] = pv
            else:
                acc_ref[...] = jnp.exp2(m_run - m_new) * acc_ref[...] + pv
            m_run = m_new
        finish()

    return worst, exact


def _flash_kernel(q_ref, k_ref, vt_ref, o_ref, qt_buf, acc_ref, *, tk, n_kv, heads):
    pending = []
    for h in range(heads):
        pending.append(_flash_head(
            q_ref.at[:, h * QK_PAD:(h + 1) * QK_PAD], k_ref.at[:, h * QK_PAD:(h + 1) * QK_PAD],
            vt_ref.at[h * VT_ROWS:(h + 1) * VT_ROWS, :], o_ref.at[:, h * V_DIM:(h + 1) * V_DIM],
            qt_buf.at[h], acc_ref.at[h], tk=tk, n_kv=n_kv))
    for worst, exact in pending:
        if worst is not None:
            pl.when(jnp.logical_not(jnp.max(worst) <= 2.0 ** FAST_PATH_LIMIT))(exact)


def _flash_attention(q, k, vt, *, batch, seq, tq, tk, heads):
    T = q.shape[0]
    tq = min(tq, seq)
    tk = min(tk, seq)
    nq = seq // tq
    return pl.pallas_call(
        functools.partial(_flash_kernel, tk=tk, n_kv=seq // tk, heads=heads),
        grid=(batch, MLA_HEADS // heads, nq),
        in_specs=[
            pl.BlockSpec((tq, heads * QK_PAD), lambda b, h, i: (b * nq + i, h)),
            pl.BlockSpec((seq, heads * QK_PAD), lambda b, h, i: (b, h)),
            pl.BlockSpec((heads * VT_ROWS, seq), lambda b, h, i: (h, b)),
        ],
        out_specs=pl.BlockSpec((tq, heads * V_DIM), lambda b, h, i: (b * nq + i, h)),
        out_shape=jax.ShapeDtypeStruct((T, MLA_HEADS * V_DIM), jnp.bfloat16),
        scratch_shapes=[pltpu.VMEM((heads, 2, QK_PAD, tq), jnp.bfloat16),
                        pltpu.VMEM((heads, VT_ROWS, tq), jnp.float32)],
        compiler_params=_params("parallel", "parallel", "arbitrary"),
        name="mla_flash_attention",
    )(q, k, vt)


def _pool_kernel(cur_ref, prev_ref, next_ref, w_ref, sc_ref, o_ref, *, seq, tm):
    i = pl.program_id(1)
    n = pl.num_programs(1)
    win = tm + 2 * POOL_HALO
    keep_prev = (i > 0).astype(jnp.float32)
    keep_next = (i < n - 1).astype(jnp.float32)
    t = i * tm + lax.broadcasted_iota(jnp.int32, (tm, 1), 0)
    for g, w in enumerate(POOL_WINDOWS):
        lo = g * POOL_GROUP
        x = cur_ref[:, lo:lo + POOL_GROUP].astype(jnp.float32)
        p = prev_ref[:, lo:lo + POOL_GROUP].astype(jnp.float32) * keep_prev
        q = next_ref[:, lo:lo + POOL_GROUP].astype(jnp.float32) * keep_next
        run = jnp.concatenate([p, x, q], axis=0)
        span = 1
        while span < w:
            run = run + pltpu.roll(run, span, 0)
            span *= 2
        ahead = w - w // 2 - 1
        if ahead:
            run = pltpu.roll(run, win - ahead, 0)
        total = run[POOL_HALO:POOL_HALO + tm]
        count = jnp.minimum(t + (w - w // 2), seq) - jnp.maximum(t - w // 2, 0)
        pooled = total / count.astype(jnp.float32) - x
        y = jnp.dot(pooled.astype(jnp.bfloat16), w_ref[g], preferred_element_type=jnp.float32)
        o_ref[:, lo:lo + POOL_GROUP] = (y * sc_ref[:, lo:lo + POOL_GROUP]).astype(o_ref.dtype)


def _pool_mixer(u, w_group, scale, *, batch, seq, tm):
    T = u.shape[0]
    tm = min(tm, seq)
    n = seq // tm
    hb = tm // POOL_HALO
    last_halo = T // POOL_HALO - 1
    return pl.pallas_call(
        functools.partial(_pool_kernel, seq=seq, tm=tm),
        grid=(batch, n),
        in_specs=[
            pl.BlockSpec((tm, MIX_WIDTH), lambda b, i: (b * n + i, 0)),
            pl.BlockSpec((POOL_HALO, MIX_WIDTH),
                         lambda b, i: (jnp.maximum((b * n + i) * hb - 1, 0), 0)),
            pl.BlockSpec((POOL_HALO, MIX_WIDTH),
                         lambda b, i: (jnp.minimum((b * n + i + 1) * hb, last_halo), 0)),
            pl.BlockSpec((len(POOL_WINDOWS), POOL_GROUP, POOL_GROUP), lambda b, i: (0, 0, 0)),
            pl.BlockSpec((1, MIX_WIDTH), lambda b, i: (0, 0)),
        ],
        out_specs=pl.BlockSpec((tm, MIX_WIDTH), lambda b, i: (b * n + i, 0)),
        out_shape=jax.ShapeDtypeStruct((T, MIX_WIDTH), jnp.bfloat16),
        compiler_params=_params("parallel", "parallel"),
        name="pool_mixer",
    )(u, u, u, w_group, scale.reshape(1, MIX_WIDTH))


def _rope_tables(seq):
    inv_freq = 1.0 / (ROPE_THETA ** (jnp.arange(0, QK_ROPE, 2, dtype=jnp.float32) / QK_ROPE))
    ang = jnp.arange(seq, dtype=jnp.float32)[:, None] * inv_freq[None, :]
    cos, sin = jnp.cos(ang), jnp.sin(ang)
    zeros = jnp.zeros((seq, LANES - QK_ROPE), jnp.float32)
    ct = jnp.concatenate([cos, cos, zeros], axis=-1)
    st = jnp.concatenate([-sin, sin, zeros], axis=-1)
    tab = jnp.concatenate([cos, cos, -sin, sin], axis=-1)
    return ct, st, tab


def _swap_halves(w):
    half = QK_ROPE // 2
    return jnp.concatenate([w[..., half:], w[..., :half]], axis=-1)


def _prep_b(b_w_in, b_w_q_up, b_w_kv_up):
    bf = jnp.bfloat16
    K = b_w_in.shape[0]
    o1 = Q_LORA
    o2 = o1 + KV_LORA
    o3 = o2 + QK_ROPE
    o4 = o3 + XA_WIDTH
    w_cq, w_ckv, w_kr, w_xq, w_gate = (b_w_in[:, :o1], b_w_in[:, o1:o2], b_w_in[:, o2:o3],
                                       b_w_in[:, o3:o4], b_w_in[:, o4:])
    w_in = jnp.concatenate([w_gate, w_xq, w_cq, w_ckv, w_kr, _swap_halves(w_kr),
                            jnp.zeros((K, LANES), b_w_in.dtype)], axis=-1).astype(bf)
    wq = b_w_q_up.reshape(Q_LORA, MLA_HEADS, QK_NOPE + QK_ROPE)
    q_nope, q_rope = wq[..., :QK_NOPE], wq[..., QK_NOPE:]
    pad = jnp.zeros((Q_LORA, MLA_HEADS, LANES - QK_ROPE), wq.dtype)
    wa = jnp.concatenate([q_nope, q_rope, pad], axis=-1).reshape(Q_LORA, MLA_HEADS * QK_PAD).astype(bf)
    wb = jnp.concatenate([_swap_halves(q_rope), pad], axis=-1).reshape(Q_LORA, MLA_HEADS * LANES).astype(bf)
    wkv = b_w_kv_up.reshape(KV_LORA, MLA_HEADS, QK_NOPE + V_DIM)
    wk = wkv[..., :QK_NOPE].reshape(KV_LORA, MLA_HEADS * QK_NOPE).astype(bf)
    wvt = wkv[..., QK_NOPE:].reshape(KV_LORA, MLA_HEADS * V_DIM).T.astype(bf)
    return w_in, wa, wb, wk, wvt


def _prep_lru(gate_w, gate_b):
    wg = (0.5 * jnp.concatenate([gate_w[:, 0], gate_w[:, 1]], axis=-1)).astype(jnp.bfloat16)
    gb = gate_b.reshape(2, 2, LRU_BLOCKS, 1, LRU_BLOCK)
    gb = 0.5 * jnp.concatenate([gb[:, 0], gb[:, 1]], axis=-1)
    return wg, gb


def _trunk(x, mem, w):
    batch, seq, _ = x.shape
    T = batch * seq
    x = x.reshape(T, D_MODEL)
    mem = mem.reshape(batch * MEM_LEN, D_MODEL)
    ct, st, tab = _rope_tables(seq)
    depth = w["norm_pre"].shape[0]
    for i in range(depth):
        kind, j = i % N_MIXERS, i // N_MIXERS
        kv = _norm_matmul(mem, w["norm_mem"][i], w["w_mem_kv"], i, tm=512, tn=1024)
        if kind == 0:
            u = _norm_matmul(x, w["norm_pre"][i], w["a_w_in"], j, tm=1024, tn=2048)
            mixes = _lru_mixer(u, w["a_conv_w"][j], w["a_conv_b"][j], w["a_wg"][j], w["a_gb"][j],
                               w["a_lambda"][j], batch=batch, seq=seq, rows=256)
            xq_off, gate_off = MIX_WIDTH, BRANCH
        elif kind == 1:
            u = _norm_matmul(x, w["norm_pre"][i], w["b_w_in"], j, tm=1024, tn=2048)
            q = _mla_q(u, w["b_q_norm"][j], w["b_wa"][j], w["b_wb"][j], ct, st,
                       seq=seq, tm=1024, heads=8)
            k, vt = _mla_kv(u, w["b_kv_norm"][j], w["b_wk"][j], w["b_wvt"][j], tab, seq=seq, tm=512)
            mixes = [_flash_attention(q, k, vt, batch=batch, seq=seq, tq=1024, tk=512, heads=1)]
            xq_off, gate_off = B_XQ_OFF, B_GATE_OFF
        else:
            u = _norm_matmul(x, w["norm_pre"][i], w["c_w_in"], j, tm=1024, tn=2048)
            mixes = [_pool_mixer(u, w["c_w_group"][j], w["c_scale"][j], batch=batch, seq=seq, tm=512)]
            xq_off, gate_off = MIX_WIDTH, BRANCH
        x = _out_block(mixes, u, xq_off, gate_off, kv, w["w_out"], i, x, w["norm_post"][i],
                       seq=seq, tm=256)
    return x.reshape(batch, seq, D_MODEL)


def kernel(x_prompt, x_sample, mem_prompt, mem_sample, norm_pre, norm_post, norm_mem, w_mem_kv, w_out,
           a_w_in, a_conv_w, a_conv_b, a_gate_w, a_gate_b, a_lambda,
           b_w_in, b_q_norm, b_kv_norm, b_w_q_up, b_w_kv_up,
           c_w_in, c_w_group, c_scale):
    bf = jnp.bfloat16
    b_parts = [_prep_b(b_w_in[j], b_w_q_up[j], b_w_kv_up[j]) for j in range(b_w_in.shape[0])]
    lru_parts = [_prep_lru(a_gate_w[j], a_gate_b[j]) for j in range(a_gate_w.shape[0])]
    w = dict(
        norm_pre=norm_pre, norm_post=norm_post, norm_mem=norm_mem,
        w_mem_kv=w_mem_kv.astype(bf), w_out=w_out.astype(bf),
        a_w_in=a_w_in.astype(bf), a_conv_w=a_conv_w, a_conv_b=a_conv_b,
        a_wg=[p[0] for p in lru_parts], a_gb=[p[1] for p in lru_parts], a_lambda=a_lambda,
        b_w_in=jnp.stack([p[0] for p in b_parts]), b_q_norm=b_q_norm, b_kv_norm=b_kv_norm,
        b_wa=[p[1] for p in b_parts], b_wb=[p[2] for p in b_parts],
        b_wk=[p[3] for p in b_parts], b_wvt=[p[4] for p in b_parts],
        c_w_in=c_w_in.astype(bf), c_w_group=[c_w_group[j].astype(bf) for j in range(c_w_group.shape[0])],
        c_scale=c_scale,
    )
    return (_trunk(x_prompt, mem_prompt, w), _trunk(x_sample, mem_sample, w))
```

```python
import functools
import math

import jax
import jax.numpy as jnp
from jax import lax
from jax.experimental import pallas as pl
from jax.experimental.pallas import tpu as pltpu

D_MODEL = 2048
MIX_WIDTH = 3 * D_MODEL // 2
XA_HEADS = 4
XA_HEAD_DIM = D_MODEL // 8
XA_WIDTH = XA_HEADS * XA_HEAD_DIM
BRANCH = MIX_WIDTH + XA_WIDTH
MEM_LEN = 256
N_MIXERS = 3
CONV_W = 4
CONV_LEFT = 2
LRU_BLOCKS = 12
LRU_BLOCK = MIX_WIDTH // LRU_BLOCKS
LRU_C = 8.0
QK_NOPE = 128
QK_ROPE = 64
V_DIM = 128
MLA_HEADS = MIX_WIDTH // V_DIM
Q_LORA = D_MODEL // 4
KV_LORA = D_MODEL // 8
ROPE_THETA = 10000.0
POOL_WINDOWS = (2, 4, 8, 16)
POOL_GROUP = MIX_WIDTH // len(POOL_WINDOWS)
EPS = 1e-6

LANES = 128
SUBLANES = 8
BF16_ROWS = 16
QK_PAD = 256
AUG_ROW = QK_NOPE + QK_ROPE
FAST_PATH_LIMIT = 64.0
VT_ROWS = V_DIM + BF16_ROWS
VMEM_LIMIT = 56 * 1024 * 1024

B_GATE_OFF = 0
B_XQ_OFF = BRANCH
B_CQ_OFF = B_XQ_OFF + XA_WIDTH
B_CKV_OFF = B_CQ_OFF + Q_LORA
B_KR_OFF = B_CKV_OFF + KV_LORA
B_IN_WIDTH = B_KR_OFF + 2 * QK_ROPE + LANES

LRU_CH = 1024
LRU_LANE_GROUPS = LRU_CH // LANES
LRU_HALO = BF16_ROWS
LRU_STEP = 4
RSQRT_FLOOR = 1e-30

POOL_HALO = BF16_ROWS


def _params(*sem, flags=None):
    return pltpu.CompilerParams(dimension_semantics=sem, vmem_limit_bytes=VMEM_LIMIT, flags=flags)


def _resident(block_shape, index_map):
    return pl.BlockSpec(block_shape, index_map, pipeline_mode=pl.Buffered(1))


def _sigmoid(x):
    return 1.0 / (1.0 + jnp.exp(-x))


def _norm_matmul_kernel(x_ref, g_ref, w_ref, o_ref, h_ref):
    @pl.when(pl.program_id(1) == 0)
    def _():
        x = x_ref[...].astype(jnp.float32)
        ms = jnp.mean(x * x, axis=-1, keepdims=True)
        h_ref[...] = ((x * lax.rsqrt(ms + EPS)) * g_ref[...]).astype(h_ref.dtype)

    o_ref[...] = jnp.dot(h_ref[...], w_ref[...],
                         preferred_element_type=jnp.float32).astype(o_ref.dtype)


def _norm_matmul(x, g, w, layer, *, tm, tn):
    T, K = x.shape
    N = w.shape[2]
    tm = min(tm, T)
    return pl.pallas_call(
        _norm_matmul_kernel,
        grid=(T // tm, N // tn),
        in_specs=[
            pl.BlockSpec((tm, K), lambda i, j: (i, 0)),
            pl.BlockSpec((1, K), lambda i, j: (0, 0)),
            pl.BlockSpec((None, K, tn), lambda i, j: (layer, 0, j)),
        ],
        out_specs=pl.BlockSpec((tm, tn), lambda i, j: (i, j)),
        out_shape=jax.ShapeDtypeStruct((T, N), jnp.bfloat16),
        scratch_shapes=[pltpu.VMEM((tm, K), jnp.bfloat16)],
        compiler_params=_params("parallel", "arbitrary"),
        name="norm_matmul",
    )(x, g.reshape(1, K), w)


def _out_kernel(*refs, n_mix):
    mix_refs = refs[:n_mix]
    xq_ref, gate_ref, kv_ref, w_ref, x_ref, g_ref, o_ref = refs[n_mix:]

    mix = mix_refs[0][...]
    for r in mix_refs[1:]:
        mix = mix + r[...]

    xq = xq_ref[...]
    scale = XA_HEAD_DIM ** -0.5 * math.log2(math.e)
    xa = []
    for h in range(XA_HEADS):
        lo = h * XA_HEAD_DIM
        q = xq[:, lo:lo + XA_HEAD_DIM]
        k = kv_ref[:, lo:lo + XA_HEAD_DIM]
        v = kv_ref[:, XA_WIDTH + lo:XA_WIDTH + lo + XA_HEAD_DIM]
        s = lax.dot_general(q, k, (((1,), (1,)), ((), ())), preferred_element_type=jnp.float32)
        m = jnp.max(s, axis=-1, keepdims=True)
        p = jnp.exp2((s - m) * scale)
        l = jnp.sum(p, axis=-1, keepdims=True)
        p = (p / l).astype(jnp.bfloat16)
        xa.append(jnp.dot(p, v, preferred_element_type=jnp.float32))

    half_gate = 0.5 * gate_ref[...].astype(jnp.float32)
    act = (half_gate * jnp.tanh(half_gate) + half_gate).astype(jnp.bfloat16)
    acc = jnp.dot(mix * act[:, :MIX_WIDTH], w_ref[:MIX_WIDTH, :], preferred_element_type=jnp.float32)
    xa = jnp.concatenate([a.astype(jnp.bfloat16) for a in xa], axis=-1)
    acc = acc + jnp.dot(xa * act[:, MIX_WIDTH:], w_ref[MIX_WIDTH:, :],
                        preferred_element_type=jnp.float32)

    ms = jnp.mean(acc * acc, axis=-1, keepdims=True)
    o_ref[...] = x_ref[...] + (acc * lax.rsqrt(ms + EPS)) * g_ref[...]


def _out_block(mixes, u, xq_off, gate_off, kv, w_out, layer, x, g_post, *, seq, tm):
    T = x.shape[0]
    tm = min(tm, seq)
    per_seq = seq // tm
    xq_blk = xq_off // XA_WIDTH
    gate_blk = gate_off // BRANCH
    mix_spec = pl.BlockSpec((tm, MIX_WIDTH), lambda i: (i, 0))
    return pl.pallas_call(
        functools.partial(_out_kernel, n_mix=len(mixes)),
        grid=(T // tm,),
        in_specs=[mix_spec] * len(mixes) + [
            pl.BlockSpec((tm, XA_WIDTH), lambda i: (i, xq_blk)),
            pl.BlockSpec((tm, BRANCH), lambda i: (i, gate_blk)),
            pl.BlockSpec((MEM_LEN, 2 * XA_WIDTH), lambda i: (i // per_seq, 0)),
            _resident((None, BRANCH, D_MODEL), lambda i: (layer, 0, 0)),
            pl.BlockSpec((tm, D_MODEL), lambda i: (i, 0)),
            pl.BlockSpec((1, D_MODEL), lambda i: (0, 0)),
        ],
        out_specs=pl.BlockSpec((tm, D_MODEL), lambda i: (i, 0)),
        out_shape=jax.ShapeDtypeStruct((T, D_MODEL), jnp.float32),
        compiler_params=_params("parallel"),
        name="xattn_gate_outproj",
    )(*mixes, u, u, kv, w_out, x, g_post.reshape(1, D_MODEL))


def _lru_kernel(cur_f, prev_f, next_f, cur_r, prev_r, next_r, cw_ref, cb_ref, wg_ref,
                gb_ref, lam_ref, hf_ref, hr_ref, w_f, w_r, a_f, b_f, h_f, a_r, b_r, h_r, carry,
                *, rows, pitch):
    s = pl.program_id(2)
    n = pl.num_programs(2)

    @pl.when(s == 0)
    def _():
        carry[...] = jnp.zeros_like(carry)

    win = rows + 2 * LRU_HALO

    def gates(cur, prev, nxt, chunk, d, w_s, a_s, b_s):
        p = prev[...].astype(jnp.float32) * (chunk > 0).astype(jnp.float32)
        q = nxt[...].astype(jnp.float32) * (chunk < n - 1).astype(jnp.float32)
        w = jnp.concatenate([p, cur[...].astype(jnp.float32), q], axis=0)
        u = []
        for grp in range(LRU_LANE_GROUPS):
            lanes = slice(grp * LANES, (grp + 1) * LANES)
            w_s[pl.ds(grp * 2 * win, win, stride=2), :] = w[:, lanes]
            acc = cb_ref[:, lanes]
            for k in range(CONV_W):
                first = grp * 2 * win + 2 * (LRU_HALO - CONV_LEFT + k)
                acc = acc + w_s[pl.ds(first, rows, stride=2), :] * cw_ref[k:k + 1, lanes]
            u.append(acc)
        u = jnp.concatenate(u, axis=-1)
        lam = lam_ref[d]
        softplus = jnp.maximum(-lam, 0.0) + jnp.log(1.0 + jnp.exp(-jnp.abs(lam)))
        half_decay = (-0.5 * LRU_C * math.log2(math.e)) * softplus
        for blk in range(LRU_CH // LRU_BLOCK):
            lo = blk * LRU_BLOCK
            ub = u[:, lo:lo + LRU_BLOCK]
            g = jnp.dot(ub.astype(jnp.bfloat16), wg_ref[d, blk],
                        preferred_element_type=jnp.float32) + gb_ref[d, blk]
            hd = half_decay[:, lo:lo + LRU_BLOCK]
            a = jnp.exp2(hd * jnp.tanh(g[:, :LRU_BLOCK]) + hd)
            half_u = 0.5 * ub
            iu = half_u * jnp.tanh(g[:, LRU_BLOCK:]) + half_u
            y = 1.0 - a * a
            b = (y * lax.rsqrt(jnp.maximum(y, RSQRT_FLOOR))) * iu
            for half in range(LRU_BLOCK // LANES):
                slab_rows = pl.ds((blk * (LRU_BLOCK // LANES) + half) * pitch, rows, stride=2)
                a_s[slab_rows, :] = a[:, half * LANES:(half + 1) * LANES]
                b_s[slab_rows, :] = b[:, half * LANES:(half + 1) * LANES]

    gates(cur_f, prev_f, next_f, s, 0, w_f, a_f, b_f)
    gates(cur_r, prev_r, next_r, n - 1 - s, 1, w_r, a_r, b_r)

    def block(a_s, b_s, h_s, h, t0, sign):
        idx = [pl.ds(2 * (t0 + sign * j), LRU_LANE_GROUPS, stride=pitch) for j in range(LRU_STEP)]
        a = [a_s[i, :] for i in idx]
        b = [b_s[i, :] for i in idx]
        a01, b01 = a[1] * a[0], a[1] * b[0] + b[1]
        a23, b23 = a[3] * a[2], a[3] * b[2] + b[3]
        a03, b03 = a23 * a01, a23 * b01 + b23
        h0 = a[0] * h + b[0]
        h1 = a01 * h + b01
        h3 = a03 * h + b03
        h2 = a[2] * h1 + b[2]
        for i, v in zip(idx, (h0, h1, h2, h3)):
            h_s[i, :] = v
        return h3

    blocks_per_iter = 8

    def body(it, hs):
        hf, hr = hs
        for j in range(blocks_per_iter):
            t = (it * blocks_per_iter + j) * LRU_STEP
            hf = block(a_f, b_f, h_f, hf, t, 1)
            hr = block(a_r, b_r, h_r, hr, rows - 1 - t, -1)
        return hf, hr

    hf, hr = lax.fori_loop(0, rows // (LRU_STEP * blocks_per_iter), body, (carry[0], carry[1]))
    carry[0] = hf
    carry[1] = hr

    for grp in range(LRU_LANE_GROUPS):
        slab_rows = pl.ds(grp * pitch, rows, stride=2)
        hf_ref[:, grp * LANES:(grp + 1) * LANES] = h_f[slab_rows, :].astype(hf_ref.dtype)
        hr_ref[:, grp * LANES:(grp + 1) * LANES] = h_r[slab_rows, :].astype(hr_ref.dtype)


def _lru_mixer(u, conv_w, conv_b, wg, gb, lam, *, batch, seq, rows):
    T = u.shape[0]
    rows = min(rows, seq)
    n = seq // rows
    pitch = 2 * rows + 4
    hb = rows // LRU_HALO
    last_halo = T // LRU_HALO - 1

    def cur_map(rev):
        def f(b, c, s):
            chunk = (n - 1 - s) if rev else s
            return (b * n + chunk, c)
        return f

    def prev_map(rev):
        def f(b, c, s):
            chunk = (n - 1 - s) if rev else s
            return (jnp.maximum((b * n + chunk) * hb - 1, 0), c)
        return f

    def next_map(rev):
        def f(b, c, s):
            chunk = (n - 1 - s) if rev else s
            return (jnp.minimum((b * n + chunk + 1) * hb, last_halo), c)
        return f

    blocks = LRU_CH // LRU_BLOCK
    in_specs = []
    for rev in (False, True):
        in_specs += [
            pl.BlockSpec((rows, LRU_CH), cur_map(rev)),
            pl.BlockSpec((LRU_HALO, LRU_CH), prev_map(rev)),
            pl.BlockSpec((LRU_HALO, LRU_CH), next_map(rev)),
        ]
    in_specs += [
        pl.BlockSpec((CONV_W, LRU_CH), lambda b, c, s: (0, c)),
        pl.BlockSpec((1, LRU_CH), lambda b, c, s: (0, c)),
        pl.BlockSpec((2, blocks, LRU_BLOCK, 2 * LRU_BLOCK), lambda b, c, s: (0, c, 0, 0)),
        pl.BlockSpec((2, blocks, 1, 2 * LRU_BLOCK), lambda b, c, s: (0, c, 0, 0)),
        pl.BlockSpec((2, 1, LRU_CH), lambda b, c, s: (0, 0, c)),
    ]
    slab = pltpu.VMEM((LRU_LANE_GROUPS * pitch, LANES), jnp.float32)
    window = pltpu.VMEM((LRU_LANE_GROUPS * 2 * (rows + 2 * LRU_HALO), LANES), jnp.float32)
    out = jax.ShapeDtypeStruct((T, MIX_WIDTH), jnp.bfloat16)
    return pl.pallas_call(
        functools.partial(_lru_kernel, rows=rows, pitch=pitch),
        grid=(batch, MIX_WIDTH // LRU_CH, n),
        in_specs=in_specs,
        out_specs=[pl.BlockSpec((rows, LRU_CH), cur_map(False)),
                   pl.BlockSpec((rows, LRU_CH), cur_map(True))],
        out_shape=[out, out],
        scratch_shapes=[window, window, slab, slab, slab, slab, slab, slab,
                        pltpu.VMEM((2, LRU_LANE_GROUPS, LANES), jnp.float32)],
        compiler_params=_params("parallel", "parallel", "arbitrary"),
        name="rglru",
    )(u, u, u, u, u, u, conv_w, conv_b.reshape(1, MIX_WIDTH), wg, gb,
      lam.reshape(2, 1, MIX_WIDTH))


def _mla_q_kernel(cq_ref, g_ref, wa_ref, wb_ref, ct_ref, st_ref, o_ref, h_ref, *, heads):
    @pl.when(pl.program_id(1) == 0)
    def _():
        x = cq_ref[...].astype(jnp.float32)
        ms = jnp.mean(x * x, axis=-1, keepdims=True)
        h_ref[...] = ((x * lax.rsqrt(ms + EPS)) * g_ref[...]).astype(h_ref.dtype)

    h = h_ref[...]
    qa = jnp.dot(h, wa_ref[...], preferred_element_type=jnp.float32)
    qb = jnp.dot(h, wb_ref[...], preferred_element_type=jnp.float32)
    scale = (QK_NOPE + QK_ROPE) ** -0.5 * math.log2(math.e)
    ct = ct_ref[...] * scale
    st = st_ref[...] * scale
    for hh in range(heads):
        lo = hh * QK_PAD
        o_ref[:, lo:lo + LANES] = (qa[:, lo:lo + LANES] * scale).astype(o_ref.dtype)
        rope = qa[:, lo + LANES:lo + 2 * LANES] * ct + qb[:, hh * LANES:(hh + 1) * LANES] * st
        o_ref[:, lo + LANES:lo + 2 * LANES] = rope.astype(o_ref.dtype)


def _mla_q(u, q_norm, wa, wb, ct, st, *, seq, tm, heads):
    T = u.shape[0]
    tm = min(tm, seq)
    per_seq = seq // tm
    cq_blk = B_CQ_OFF // Q_LORA
    return pl.pallas_call(
        functools.partial(_mla_q_kernel, heads=heads),
        grid=(T // tm, MLA_HEADS // heads),
        in_specs=[
            pl.BlockSpec((tm, Q_LORA), lambda i, j: (i, cq_blk)),
            pl.BlockSpec((1, Q_LORA), lambda i, j: (0, 0)),
            pl.BlockSpec((Q_LORA, heads * QK_PAD), lambda i, j: (0, j)),
            pl.BlockSpec((Q_LORA, heads * LANES), lambda i, j: (0, j)),
            pl.BlockSpec((tm, LANES), lambda i, j: (i % per_seq, 0)),
            pl.BlockSpec((tm, LANES), lambda i, j: (i % per_seq, 0)),
        ],
        out_specs=pl.BlockSpec((tm, heads * QK_PAD), lambda i, j: (i, j)),
        out_shape=jax.ShapeDtypeStruct((T, MLA_HEADS * QK_PAD), jnp.bfloat16),
        scratch_shapes=[pltpu.VMEM((tm, Q_LORA), jnp.bfloat16)],
        compiler_params=_params("parallel", "arbitrary"),
        name="mla_q_proj",
    )(u, q_norm.reshape(1, Q_LORA), wa, wb, ct, st)


def _mla_kv_kernel(ckv_ref, g_ref, wk_ref, wvt_ref, kr_ref, tab_ref, k_ref, vt_ref):
    x = ckv_ref[...].astype(jnp.float32)
    ms = jnp.mean(x * x, axis=-1, keepdims=True)
    h = ((x * lax.rsqrt(ms + EPS)) * g_ref[...]).astype(jnp.bfloat16)
    kn = jnp.dot(h, wk_ref[...], preferred_element_type=jnp.float32)
    vt = lax.dot_general(wvt_ref[...], h, (((1,), (1,)), ((), ())),
                         preferred_element_type=jnp.float32).astype(vt_ref.dtype)
    ones = jnp.ones((VT_ROWS - V_DIM, vt.shape[1]), vt_ref.dtype)
    for hh in range(MLA_HEADS):
        vt_ref[hh * VT_ROWS:hh * VT_ROWS + V_DIM, :] = vt[hh * V_DIM:(hh + 1) * V_DIM, :]
        vt_ref[hh * VT_ROWS + V_DIM:(hh + 1) * VT_ROWS, :] = ones
    t = kr_ref[...].astype(jnp.float32) * tab_ref[...]
    rot = t + pltpu.roll(t, QK_ROPE, 1)
    lane = lax.broadcasted_iota(jnp.int32, rot.shape, 1)
    rot = jnp.where(lane < QK_ROPE, rot, jnp.where(lane == QK_ROPE, 1.0, 0.0)).astype(k_ref.dtype)
    for hh in range(MLA_HEADS):
        k_ref[:, hh * QK_PAD:hh * QK_PAD + LANES] = kn[:, hh * LANES:(hh + 1) * LANES].astype(k_ref.dtype)
        k_ref[:, hh * QK_PAD + LANES:(hh + 1) * QK_PAD] = rot


def _mla_kv(u, kv_norm, wk, wvt, tab, *, seq, tm):
    T = u.shape[0]
    tm = min(tm, seq)
    per_seq = seq // tm
    ckv_blk = B_CKV_OFF // KV_LORA
    kr_blk = B_KR_OFF // LANES
    return pl.pallas_call(
        _mla_kv_kernel,
        grid=(T // tm,),
        in_specs=[
            pl.BlockSpec((tm, KV_LORA), lambda i: (i, ckv_blk)),
            pl.BlockSpec((1, KV_LORA), lambda i: (0, 0)),
            pl.BlockSpec((KV_LORA, MLA_HEADS * QK_NOPE), lambda i: (0, 0)),
            pl.BlockSpec((MLA_HEADS * V_DIM, KV_LORA), lambda i: (0, 0)),
            pl.BlockSpec((tm, LANES), lambda i: (i, kr_blk)),
            pl.BlockSpec((tm, LANES), lambda i: (i % per_seq, 0)),
        ],
        out_specs=[pl.BlockSpec((tm, MLA_HEADS * QK_PAD), lambda i: (i, 0)),
                   pl.BlockSpec((MLA_HEADS * VT_ROWS, tm), lambda i: (0, i))],
        out_shape=[jax.ShapeDtypeStruct((T, MLA_HEADS * QK_PAD), jnp.bfloat16),
                   jax.ShapeDtypeStruct((MLA_HEADS * VT_ROWS, T), jnp.bfloat16)],
        compiler_params=_params("parallel"),
        name="mla_kv_proj",
    )(u, kv_norm.reshape(1, KV_LORA), wk, wvt, u, tab)


def _flash_head(q_ref, k_ref, vt_ref, o_ref, qt_buf, acc_ref, *, tk, n_kv):
    bf = jnp.bfloat16
    qt = q_ref[...].astype(jnp.float32).T.astype(bf)
    qt_buf[0] = qt
    qt_buf[1] = qt
    refs = {}
    m_true = worst = None
    for j in range(n_kv):
        k = k_ref[j * tk:(j + 1) * tk, :]
        s = jnp.dot(k, qt_buf[j % 2], preferred_element_type=jnp.float32)
        if j == 0:
            cmax = jnp.max(s, axis=0, keepdims=True)
            refs[0] = cmax
            p = jnp.exp2((s - cmax).astype(bf))
            m_true = cmax
        else:
            p = jnp.exp2(s.astype(bf))
            pmax = jnp.max(p, axis=0, keepdims=True).astype(jnp.float32)
            m_true = jnp.maximum(m_true, refs[j] + jnp.log2(pmax))
            worst = pmax if worst is None else jnp.maximum(worst, pmax)
        for t in ([1, 2] if j == 0 else [j + 2]):
            if t < n_kv:
                nref = m_true.astype(bf)
                qt_buf[t % 2, AUG_ROW:AUG_ROW + BF16_ROWS, :] = jnp.broadcast_to(
                    -nref, (BF16_ROWS, nref.shape[1]))
                refs[t] = nref.astype(jnp.float32)
        pv = jnp.dot(vt_ref[:, j * tk:(j + 1) * tk], p, preferred_element_type=jnp.float32)
        if j == 0:
            acc_ref[...] = pv
        else:
            acc_ref[...] = jnp.exp2(refs[j - 1] - refs[j]) * acc_ref[...] + pv

    def finish():
        out = acc_ref[:V_DIM, :] / acc_ref[V_DIM:V_DIM + 1, :]
        o_ref[...] = out.T.astype(o_ref.dtype)

    finish()

    def exact():
        m_run = None
        for j in range(n_kv):
            s = jnp.dot(k_ref[j * tk:(j + 1) * tk, :], qt, preferred_element_type=jnp.float32)
            cmax = jnp.max(s, axis=0, keepdims=True)
            m_new = cmax if j == 0 else jnp.maximum(m_run, cmax)
            p = jnp.exp2((s - m_new).astype(bf))
            pv = jnp.dot(vt_ref[:, j * tk:(j + 1) * tk], p, preferred_element_type=jnp.float32)
            if j == 0:
                acc_ref[...] = pv
            else:
                acc_ref[...] = jnp.exp2(m_run - m_new) * acc_ref[...] + pv
            m_run = m_new
        finish()

    return worst, exact


def _flash_kernel(q_ref, k_ref, vt_ref, o_ref, qt_buf, acc_ref, *, tk, n_kv, heads):
    pending = []
    for h in range(heads):
        pending.append(_flash_head(
            q_ref.at[:, h * QK_PAD:(h + 1) * QK_PAD], k_ref.at[:, h * QK_PAD:(h + 1) * QK_PAD],
            vt_ref.at[h * VT_ROWS:(h + 1) * VT_ROWS, :], o_ref.at[:, h * V_DIM:(h + 1) * V_DIM],
            qt_buf.at[h], acc_ref.at[h], tk=tk, n_kv=n_kv))
    for worst, exact in pending:
        if worst is not None:
            pl.when(jnp.logical_not(jnp.max(worst) <= 2.0 ** FAST_PATH_LIMIT))(exact)


def _flash_attention(q, k, vt, *, batch, seq, tq, tk, heads):
    T = q.shape[0]
    tq = min(tq, seq)
    tk = min(tk, seq)
    nq = seq // tq
    return pl.pallas_call(
        functools.partial(_flash_kernel, tk=tk, n_kv=seq // tk, heads=heads),
        grid=(batch, MLA_HEADS // heads, nq),
        in_specs=[
            pl.BlockSpec((tq, heads * QK_PAD), lambda b, h, i: (b * nq + i, h)),
            pl.BlockSpec((seq, heads * QK_PAD), lambda b, h, i: (b, h)),
            pl.BlockSpec((heads * VT_ROWS, seq), lambda b, h, i: (h, b)),
        ],
        out_specs=pl.BlockSpec((tq, heads * V_DIM), lambda b, h, i: (b * nq + i, h)),
        out_shape=jax.ShapeDtypeStruct((T, MLA_HEADS * V_DIM), jnp.bfloat16),
        scratch_shapes=[pltpu.VMEM((heads, 2, QK_PAD, tq), jnp.bfloat16),
                        pltpu.VMEM((heads, VT_ROWS, tq), jnp.float32)],
        compiler_params=_params("parallel", "parallel", "arbitrary"),
        name="mla_flash_attention",
    )(q, k, vt)


def _pool_kernel(cur_ref, prev_ref, next_ref, w_ref, sc_ref, o_ref, *, seq, tm):
    i = pl.program_id(1)
    n = pl.num_programs(1)
    win = tm + 2 * POOL_HALO
    keep_prev = (i > 0).astype(jnp.float32)
    keep_next = (i < n - 1).astype(jnp.float32)
    t = i * tm + lax.broadcasted_iota(jnp.int32, (tm, 1), 0)
    for g, w in enumerate(POOL_WINDOWS):
        lo = g * POOL_GROUP
        x = cur_ref[:, lo:lo + POOL_GROUP].astype(jnp.float32)
        p = prev_ref[:, lo:lo + POOL_GROUP].astype(jnp.float32) * keep_prev
        q = next_ref[:, lo:lo + POOL_GROUP].astype(jnp.float32) * keep_next
        run = jnp.concatenate([p, x, q], axis=0)
        span = 1
        while span < w:
            run = run + pltpu.roll(run, span, 0)
            span *= 2
        ahead = w - w // 2 - 1
        if ahead:
            run = pltpu.roll(run, win - ahead, 0)
        total = run[POOL_HALO:POOL_HALO + tm]
        count = jnp.minimum(t + (w - w // 2), seq) - jnp.maximum(t - w // 2, 0)
        pooled = total / count.astype(jnp.float32) - x
        y = jnp.dot(pooled.astype(jnp.bfloat16), w_ref[g], preferred_element_type=jnp.float32)
        o_ref[:, lo:lo + POOL_GROUP] = (y * sc_ref[:, lo:lo + POOL_GROUP]).astype(o_ref.dtype)


def _pool_mixer(u, w_group, scale, *, batch, seq, tm):
    T = u.shape[0]
    tm = min(tm, seq)
    n = seq // tm
    hb = tm // POOL_HALO
    last_halo = T // POOL_HALO - 1
    return pl.pallas_call(
        functools.partial(_pool_kernel, seq=seq, tm=tm),
        grid=(batch, n),
        in_specs=[
            pl.BlockSpec((tm, MIX_WIDTH), lambda b, i: (b * n + i, 0)),
            pl.BlockSpec((POOL_HALO, MIX_WIDTH),
                         lambda b, i: (jnp.maximum((b * n + i) * hb - 1, 0), 0)),
            pl.BlockSpec((POOL_HALO, MIX_WIDTH),
                         lambda b, i: (jnp.minimum((b * n + i + 1) * hb, last_halo), 0)),
            pl.BlockSpec((len(POOL_WINDOWS), POOL_GROUP, POOL_GROUP), lambda b, i: (0, 0, 0)),
            pl.BlockSpec((1, MIX_WIDTH), lambda b, i: (0, 0)),
        ],
        out_specs=pl.BlockSpec((tm, MIX_WIDTH), lambda b, i: (b * n + i, 0)),
        out_shape=jax.ShapeDtypeStruct((T, MIX_WIDTH), jnp.bfloat16),
        compiler_params=_params("parallel", "parallel"),
        name="pool_mixer",
    )(u, u, u, w_group, scale.reshape(1, MIX_WIDTH))


def _rope_tables(seq):
    inv_freq = 1.0 / (ROPE_THETA ** (jnp.arange(0, QK_ROPE, 2, dtype=jnp.float32) / QK_ROPE))
    ang = jnp.arange(seq, dtype=jnp.float32)[:, None] * inv_freq[None, :]
    cos, sin = jnp.cos(ang), jnp.sin(ang)
    zeros = jnp.zeros((seq, LANES - QK_ROPE), jnp.float32)
    ct = jnp.concatenate([cos, cos, zeros], axis=-1)
    st = jnp.concatenate([-sin, sin, zeros], axis=-1)
    tab = jnp.concatenate([cos, cos, -sin, sin], axis=-1)
    return ct, st, tab


def _swap_halves(w):
    half = QK_ROPE // 2
    return jnp.concatenate([w[..., half:], w[..., :half]], axis=-1)


def _prep_b(b_w_in, b_w_q_up, b_w_kv_up):
    bf = jnp.bfloat16
    K = b_w_in.shape[0]
    o1 = Q_LORA
    o2 = o1 + KV_LORA
    o3 = o2 + QK_ROPE
    o4 = o3 + XA_WIDTH
    w_cq, w_ckv, w_kr, w_xq, w_gate = (b_w_in[:, :o1], b_w_in[:, o1:o2], b_w_in[:, o2:o3],
                                       b_w_in[:, o3:o4], b_w_in[:, o4:])
    w_in = jnp.concatenate([w_gate, w_xq, w_cq, w_ckv, w_kr, _swap_halves(w_kr),
                            jnp.zeros((K, LANES), b_w_in.dtype)], axis=-1).astype(bf)
    wq = b_w_q_up.reshape(Q_LORA, MLA_HEADS, QK_NOPE + QK_ROPE)
    q_nope, q_rope = wq[..., :QK_NOPE], wq[..., QK_NOPE:]
    pad = jnp.zeros((Q_LORA, MLA_HEADS, LANES - QK_ROPE), wq.dtype)
    wa = jnp.concatenate([q_nope, q_rope, pad], axis=-1).reshape(Q_LORA, MLA_HEADS * QK_PAD).astype(bf)
    wb = jnp.concatenate([_swap_halves(q_rope), pad], axis=-1).reshape(Q_LORA, MLA_HEADS * LANES).astype(bf)
    wkv = b_w_kv_up.reshape(KV_LORA, MLA_HEADS, QK_NOPE + V_DIM)
    wk = wkv[..., :QK_NOPE].reshape(KV_LORA, MLA_HEADS * QK_NOPE).astype(bf)
    wvt = wkv[..., QK_NOPE:].reshape(KV_LORA, MLA_HEADS * V_DIM).T.astype(bf)
    return w_in, wa, wb, wk, wvt


def _prep_lru(gate_w, gate_b):
    wg = (0.5 * jnp.concatenate([gate_w[:, 0], gate_w[:, 1]], axis=-1)).astype(jnp.bfloat16)
    gb = gate_b.reshape(2, 2, LRU_BLOCKS, 1, LRU_BLOCK)
    gb = 0.5 * jnp.concatenate([gb[:, 0], gb[:, 1]], axis=-1)
    return wg, gb


def _trunk(x, mem, w):
    batch, seq, _ = x.shape
    T = batch * seq
    x = x.reshape(T, D_MODEL)
    mem = mem.reshape(batch * MEM_LEN, D_MODEL)
    ct, st, tab = _rope_tables(seq)
    depth = w["norm_pre"].shape[0]
    for i in range(depth):
        kind, j = i % N_MIXERS, i // N_MIXERS
        kv = _norm_matmul(mem, w["norm_mem"][i], w["w_mem_kv"], i, tm=512, tn=1024)
        if kind == 0:
            u = _norm_matmul(x, w["norm_pre"][i], w["a_w_in"], j, tm=1024, tn=2048)
            mixes = _lru_mixer(u, w["a_conv_w"][j], w["a_conv_b"][j], w["a_wg"][j], w["a_gb"][j],
                               w["a_lambda"][j], batch=batch, seq=seq, rows=256)
            xq_off, gate_off = MIX_WIDTH, BRANCH
        elif kind == 1:
            u = _norm_matmul(x, w["norm_pre"][i], w["b_w_in"], j, tm=1024, tn=2048)
            q = _mla_q(u, w["b_q_norm"][j], w["b_wa"][j], w["b_wb"][j], ct, st,
                       seq=seq, tm=1024, heads=8)
            k, vt = _mla_kv(u, w["b_kv_norm"][j], w["b_wk"][j], w["b_wvt"][j], tab, seq=seq, tm=512)
            mixes = [_flash_attention(q, k, vt, batch=batch, seq=seq, tq=1024, tk=512, heads=1)]
            xq_off, gate_off = B_XQ_OFF, B_GATE_OFF
        else:
            u = _norm_matmul(x, w["norm_pre"][i], w["c_w_in"], j, tm=1024, tn=2048)
            mixes = [_pool_mixer(u, w["c_w_group"][j], w["c_scale"][j], batch=batch, seq=seq, tm=512)]
            xq_off, gate_off = MIX_WIDTH, BRANCH
        x = _out_block(mixes, u, xq_off, gate_off, kv, w["w_out"], i, x, w["norm_post"][i],
                       seq=seq, tm=256)
    return x.reshape(batch, seq, D_MODEL)


def kernel(x_prompt, x_sample, mem_prompt, mem_sample, norm_pre, norm_post, norm_mem, w_mem_kv, w_out,
           a_w_in, a_conv_w, a_conv_b, a_gate_w, a_gate_b, a_lambda,
           b_w_in, b_q_norm, b_kv_norm, b_w_q_up, b_w_kv_up,
           c_w_in, c_w_group, c_scale):
    bf = jnp.bfloat16
    b_parts = [_prep_b(b_w_in[j], b_w_q_up[j], b_w_kv_up[j]) for j in range(b_w_in.shape[0])]
    lru_parts = [_prep_lru(a_gate_w[j], a_gate_b[j]) for j in range(a_gate_w.shape[0])]
    w = dict(
        norm_pre=norm_pre, norm_post=norm_post, norm_mem=norm_mem,
        w_mem_kv=w_mem_kv.astype(bf), w_out=w_out.astype(bf),
        a_w_in=a_w_in.astype(bf), a_conv_w=a_conv_w, a_conv_b=a_conv_b,
        a_wg=[p[0] for p in lru_parts], a_gb=[p[1] for p in lru_parts], a_lambda=a_lambda,
        b_w_in=jnp.stack([p[0] for p in b_parts]), b_q_norm=b_q_norm, b_kv_norm=b_kv_norm,
        b_wa=[p[1] for p in b_parts], b_wb=[p[2] for p in b_parts],
        b_wk=[p[3] for p in b_parts], b_wvt=[p[4] for p in b_parts],
        c_w_in=c_w_in.astype(bf), c_w_group=[c_w_group[j].astype(bf) for j in range(c_w_group.shape[0])],
        c_scale=c_scale,
    )
    return (_trunk(x_prompt, mem_prompt, w), _trunk(x_sample, mem_sample, w))
```

```python
import functools
import math

import jax
import jax.numpy as jnp
from jax import lax
from jax.experimental import pallas as pl
from jax.experimental.pallas import tpu as pltpu

D_MODEL = 2048
MIX_WIDTH = 3 * D_MODEL // 2
XA_HEADS = 4
XA_HEAD_DIM = D_MODEL // 8
XA_WIDTH = XA_HEADS * XA_HEAD_DIM
BRANCH = MIX_WIDTH + XA_WIDTH
MEM_LEN = 256
N_MIXERS = 3
CONV_W = 4
CONV_LEFT = 2
LRU_BLOCKS = 12
LRU_BLOCK = MIX_WIDTH // LRU_BLOCKS
LRU_C = 8.0
QK_NOPE = 128
QK_ROPE = 64
V_DIM = 128
MLA_HEADS = MIX_WIDTH // V_DIM
Q_LORA = D_MODEL // 4
KV_LORA = D_MODEL // 8
ROPE_THETA = 10000.0
POOL_WINDOWS = (2, 4, 8, 16)
POOL_GROUP = MIX_WIDTH // len(POOL_WINDOWS)
EPS = 1e-6

LANES = 128
SUBLANES = 8
BF16_ROWS = 16
QK_PAD = 256
AUG_ROW = QK_NOPE + QK_ROPE
FAST_PATH_LIMIT = 64.0
VT_ROWS = V_DIM + BF16_ROWS
VMEM_LIMIT = 56 * 1024 * 1024

B_GATE_OFF = 0
B_XQ_OFF = BRANCH
B_CQ_OFF = B_XQ_OFF + XA_WIDTH
B_CKV_OFF = B_CQ_OFF + Q_LORA
B_KR_OFF = B_CKV_OFF + KV_LORA
B_IN_WIDTH = B_KR_OFF + 2 * QK_ROPE + LANES

LRU_CH = 1024
LRU_LANE_GROUPS = LRU_CH // LANES
LRU_HALO = BF16_ROWS
LRU_STEP = 4
RSQRT_FLOOR = 1e-30

POOL_HALO = BF16_ROWS


def _params(*sem, flags=None):
    return pltpu.CompilerParams(dimension_semantics=sem, vmem_limit_bytes=VMEM_LIMIT, flags=flags)


def _resident(block_shape, index_map):
    return pl.BlockSpec(block_shape, index_map, pipeline_mode=pl.Buffered(1))


def _norm_matmul_kernel(x_ref, g_ref, w_ref, o_ref, h_ref):
    @pl.when(pl.program_id(1) == 0)
    def _():
        x = x_ref[...].astype(jnp.float32)
        ms = jnp.mean(x * x, axis=-1, keepdims=True)
        h_ref[...] = ((x * lax.rsqrt(ms + EPS)) * g_ref[...]).astype(h_ref.dtype)

    o_ref[...] = jnp.dot(h_ref[...], w_ref[...],
                         preferred_element_type=jnp.float32).astype(o_ref.dtype)


def _norm_matmul(x, g, w, layer, *, tm, tn):
    T, K = x.shape
    N = w.shape[2]
    tm = min(tm, T)
    return pl.pallas_call(
        _norm_matmul_kernel,
        grid=(T // tm, N // tn),
        in_specs=[
            pl.BlockSpec((tm, K), lambda i, j: (i, 0)),
            pl.BlockSpec((1, K), lambda i, j: (0, 0)),
            pl.BlockSpec((None, K, tn), lambda i, j: (layer, 0, j)),
        ],
        out_specs=pl.BlockSpec((tm, tn), lambda i, j: (i, j)),
        out_shape=jax.ShapeDtypeStruct((T, N), jnp.bfloat16),
        scratch_shapes=[pltpu.VMEM((tm, K), jnp.bfloat16)],
        compiler_params=_params("parallel", "arbitrary"),
        name="norm_matmul",
    )(x, g.reshape(1, K), w)


def _out_kernel(*refs, n_mix):
    mix_refs = refs[:n_mix]
    xq_ref, gate_ref, kv_ref, w_ref, x_ref, g_ref, o_ref = refs[n_mix:]

    mix = mix_refs[0][...]
    for r in mix_refs[1:]:
        mix = mix + r[...]

    xq = xq_ref[...]
    scale = XA_HEAD_DIM ** -0.5 * math.log2(math.e)
    xa = []
    for h in range(XA_HEADS):
        lo = h * XA_HEAD_DIM
        q = xq[:, lo:lo + XA_HEAD_DIM]
        k = kv_ref[:, lo:lo + XA_HEAD_DIM]
        v = kv_ref[:, XA_WIDTH + lo:XA_WIDTH + lo + XA_HEAD_DIM]
        s = lax.dot_general(q, k, (((1,), (1,)), ((), ())), preferred_element_type=jnp.float32)
        m = jnp.max(s, axis=-1, keepdims=True)
        p = jnp.exp2((s - m) * scale)
        l = jnp.sum(p, axis=-1, keepdims=True)
        p = (p / l).astype(jnp.bfloat16)
        xa.append(jnp.dot(p, v, preferred_element_type=jnp.float32))

    half_gate = 0.5 * gate_ref[...].astype(jnp.float32)
    act = (half_gate * jnp.tanh(half_gate) + half_gate).astype(jnp.bfloat16)
    acc = jnp.dot(mix * act[:, :MIX_WIDTH], w_ref[:MIX_WIDTH, :], preferred_element_type=jnp.float32)
    xa = jnp.concatenate([a.astype(jnp.bfloat16) for a in xa], axis=-1)
    acc = acc + jnp.dot(xa * act[:, MIX_WIDTH:], w_ref[MIX_WIDTH:, :],
                        preferred_element_type=jnp.float32)

    ms = jnp.mean(acc * acc, axis=-1, keepdims=True)
    o_ref[...] = x_ref[...] + (acc * lax.rsqrt(ms + EPS)) * g_ref[...]


def _out_block(mixes, u, xq_off, gate_off, kv, w_out, layer, x, g_post, *, seq, tm):
    T = x.shape[0]
    tm = min(tm, seq)
    per_seq = seq // tm
    xq_blk = xq_off // XA_WIDTH
    gate_blk = gate_off // BRANCH
    mix_spec = pl.BlockSpec((tm, MIX_WIDTH), lambda i: (i, 0))
    return pl.pallas_call(
        functools.partial(_out_kernel, n_mix=len(mixes)),
        grid=(T // tm,),
        in_specs=[mix_spec] * len(mixes) + [
            pl.BlockSpec((tm, XA_WIDTH), lambda i: (i, xq_blk)),
            pl.BlockSpec((tm, BRANCH), lambda i: (i, gate_blk)),
            pl.BlockSpec((MEM_LEN, 2 * XA_WIDTH), lambda i: (i // per_seq, 0)),
            _resident((None, BRANCH, D_MODEL), lambda i: (layer, 0, 0)),
            pl.BlockSpec((tm, D_MODEL), lambda i: (i, 0)),
            pl.BlockSpec((1, D_MODEL), lambda i: (0, 0)),
        ],
        out_specs=pl.BlockSpec((tm, D_MODEL), lambda i: (i, 0)),
        out_shape=jax.ShapeDtypeStruct((T, D_MODEL), jnp.float32),
        compiler_params=_params("parallel"),
        name="xattn_gate_outproj",
    )(*mixes, u, u, kv, w_out, x, g_post.reshape(1, D_MODEL))


def _lru_kernel(cur_f, prev_f, next_f, cur_r, prev_r, next_r, cw_ref, cb_ref, wg_ref,
                gb_ref, lam_ref, hf_ref, hr_ref, w_f, w_r, a_f, b_f, h_f, a_r, b_r, h_r, carry,
                *, rows, pitch):
    s = pl.program_id(2)
    n = pl.num_programs(2)

    @pl.when(s == 0)
    def _():
        carry[...] = jnp.zeros_like(carry)

    win = rows + 2 * LRU_HALO

    def gates(cur, prev, nxt, chunk, d, w_s, a_s, b_s):
        p = prev[...].astype(jnp.float32) * (chunk > 0).astype(jnp.float32)
        q = nxt[...].astype(jnp.float32) * (chunk < n - 1).astype(jnp.float32)
        w = jnp.concatenate([p, cur[...].astype(jnp.float32), q], axis=0)
        half_cw = 0.5 * cw_ref[...]
        half_cb = 0.5 * cb_ref[...]
        half_u = []
        for grp in range(LRU_LANE_GROUPS):
            lanes = slice(grp * LANES, (grp + 1) * LANES)
            w_s[pl.ds(grp * 2 * win, win, stride=2), :] = w[:, lanes]
            acc = half_cb[:, lanes]
            for k in range(CONV_W):
                first = grp * 2 * win + 2 * (LRU_HALO - CONV_LEFT + k)
                acc = acc + w_s[pl.ds(first, rows, stride=2), :] * half_cw[k:k + 1, lanes]
            half_u.append(acc)
        half_u = jnp.concatenate(half_u, axis=-1)
        lam = lam_ref[d]
        softplus = jnp.maximum(-lam, 0.0) + jnp.log(1.0 + jnp.exp(-jnp.abs(lam)))
        half_decay = (-0.5 * LRU_C * math.log2(math.e)) * softplus
        for blk in range(LRU_CH // LRU_BLOCK):
            lo = blk * LRU_BLOCK
            hu = half_u[:, lo:lo + LRU_BLOCK]
            g = jnp.dot(hu.astype(jnp.bfloat16), wg_ref[d, blk],
                        preferred_element_type=jnp.float32) + gb_ref[d, blk]
            hd = half_decay[:, lo:lo + LRU_BLOCK]
            a = jnp.exp2(hd * jnp.tanh(g[:, :LRU_BLOCK]) + hd)
            iu = hu * jnp.tanh(g[:, LRU_BLOCK:]) + hu
            y = 1.0 - a * a
            b = (y * lax.rsqrt(jnp.maximum(y, RSQRT_FLOOR))) * iu
            for half in range(LRU_BLOCK // LANES):
                slab_rows = pl.ds((blk * (LRU_BLOCK // LANES) + half) * pitch, rows, stride=2)
                a_s[slab_rows, :] = a[:, half * LANES:(half + 1) * LANES]
                b_s[slab_rows, :] = b[:, half * LANES:(half + 1) * LANES]

    gates(cur_f, prev_f, next_f, s, 0, w_f, a_f, b_f)
    gates(cur_r, prev_r, next_r, n - 1 - s, 1, w_r, a_r, b_r)

    def block(a_s, b_s, h_s, h, t0, sign):
        idx = [pl.ds(2 * (t0 + sign * j), LRU_LANE_GROUPS, stride=pitch) for j in range(LRU_STEP)]
        a = [a_s[i, :] for i in idx]
        b = [b_s[i, :] for i in idx]
        a01, b01 = a[1] * a[0], a[1] * b[0] + b[1]
        a23, b23 = a[3] * a[2], a[3] * b[2] + b[3]
        a03, b03 = a23 * a01, a23 * b01 + b23
        h0 = a[0] * h + b[0]
        h1 = a01 * h + b01
        h3 = a03 * h + b03
        h2 = a[2] * h1 + b[2]
        for i, v in zip(idx, (h0, h1, h2, h3)):
            h_s[i, :] = v
        return h3

    blocks_per_iter = 8

    def body(it, hs):
        hf, hr = hs
        for j in range(blocks_per_iter):
            t = (it * blocks_per_iter + j) * LRU_STEP
            hf = block(a_f, b_f, h_f, hf, t, 1)
            hr = block(a_r, b_r, h_r, hr, rows - 1 - t, -1)
        return hf, hr

    hf, hr = lax.fori_loop(0, rows // (LRU_STEP * blocks_per_iter), body, (carry[0], carry[1]))
    carry[0] = hf
    carry[1] = hr

    for grp in range(LRU_LANE_GROUPS):
        slab_rows = pl.ds(grp * pitch, rows, stride=2)
        hf_ref[:, grp * LANES:(grp + 1) * LANES] = h_f[slab_rows, :].astype(hf_ref.dtype)
        hr_ref[:, grp * LANES:(grp + 1) * LANES] = h_r[slab_rows, :].astype(hr_ref.dtype)


def _lru_mixer(u, conv_w, conv_b, wg, gb, lam, *, batch, seq, rows):
    T = u.shape[0]
    rows = min(rows, seq)
    n = seq // rows
    pitch = 2 * rows + 4
    hb = rows // LRU_HALO
    last_halo = T // LRU_HALO - 1

    def cur_map(rev):
        def f(b, c, s):
            chunk = (n - 1 - s) if rev else s
            return (b * n + chunk, c)
        return f

    def prev_map(rev):
        def f(b, c, s):
            chunk = (n - 1 - s) if rev else s
            return (jnp.maximum((b * n + chunk) * hb - 1, 0), c)
        return f

    def next_map(rev):
        def f(b, c, s):
            chunk = (n - 1 - s) if rev else s
            return (jnp.minimum((b * n + chunk + 1) * hb, last_halo), c)
        return f

    blocks = LRU_CH // LRU_BLOCK
    in_specs = []
    for rev in (False, True):
        in_specs += [
            pl.BlockSpec((rows, LRU_CH), cur_map(rev)),
            pl.BlockSpec((LRU_HALO, LRU_CH), prev_map(rev)),
            pl.BlockSpec((LRU_HALO, LRU_CH), next_map(rev)),
        ]
    in_specs += [
        pl.BlockSpec((CONV_W, LRU_CH), lambda b, c, s: (0, c)),
        pl.BlockSpec((1, LRU_CH), lambda b, c, s: (0, c)),
        pl.BlockSpec((2, blocks, LRU_BLOCK, 2 * LRU_BLOCK), lambda b, c, s: (0, c, 0, 0)),
        pl.BlockSpec((2, blocks, 1, 2 * LRU_BLOCK), lambda b, c, s: (0, c, 0, 0)),
        pl.BlockSpec((2, 1, LRU_CH), lambda b, c, s: (0, 0, c)),
    ]
    slab = pltpu.VMEM((LRU_LANE_GROUPS * pitch, LANES), jnp.float32)
    window = pltpu.VMEM((LRU_LANE_GROUPS * 2 * (rows + 2 * LRU_HALO), LANES), jnp.float32)
    out = jax.ShapeDtypeStruct((T, MIX_WIDTH), jnp.bfloat16)
    return pl.pallas_call(
        functools.partial(_lru_kernel, rows=rows, pitch=pitch),
        grid=(batch, MIX_WIDTH // LRU_CH, n),
        in_specs=in_specs,
        out_specs=[pl.BlockSpec((rows, LRU_CH), cur_map(False)),
                   pl.BlockSpec((rows, LRU_CH), cur_map(True))],
        out_shape=[out, out],
        scratch_shapes=[window, window, slab, slab, slab, slab, slab, slab,
                        pltpu.VMEM((2, LRU_LANE_GROUPS, LANES), jnp.float32)],
        compiler_params=_params("parallel", "parallel", "arbitrary"),
        name="rglru",
    )(u, u, u, u, u, u, conv_w, conv_b.reshape(1, MIX_WIDTH), wg, gb,
      lam.reshape(2, 1, MIX_WIDTH))


def _mla_q_kernel(cq_ref, g_ref, wa_ref, wb_ref, ct_ref, st_ref, o_ref, h_ref, *, heads):
    @pl.when(pl.program_id(1) == 0)
    def _():
        x = cq_ref[...].astype(jnp.float32)
        ms = jnp.mean(x * x, axis=-1, keepdims=True)
        h_ref[...] = ((x * lax.rsqrt(ms + EPS)) * g_ref[...]).astype(h_ref.dtype)

    h = h_ref[...]
    nt = (((1,), (1,)), ((), ()))
    qa = lax.dot_general(wa_ref[...], h, nt, preferred_element_type=jnp.float32)
    qb = lax.dot_general(wb_ref[...], h, nt, preferred_element_type=jnp.float32)
    scale = (QK_NOPE + QK_ROPE) ** -0.5 * math.log2(math.e)
    ct = ct_ref[...] * scale
    st = st_ref[...] * scale
    for hh in range(heads):
        lo = hh * QK_PAD
        o_ref[lo:lo + LANES, :] = (qa[lo:lo + LANES] * scale).astype(o_ref.dtype)
        rope = qa[lo + LANES:lo + 2 * LANES] * ct + qb[hh * LANES:(hh + 1) * LANES] * st
        o_ref[lo + LANES:lo + 2 * LANES, :] = rope.astype(o_ref.dtype)


def _mla_q(u, q_norm, wa_t, wb_t, ct_t, st_t, *, seq, tm, heads):
    T = u.shape[0]
    tm = min(tm, seq)
    per_seq = seq // tm
    cq_blk = B_CQ_OFF // Q_LORA
    return pl.pallas_call(
        functools.partial(_mla_q_kernel, heads=heads),
        grid=(T // tm, MLA_HEADS // heads),
        in_specs=[
            pl.BlockSpec((tm, Q_LORA), lambda i, j: (i, cq_blk)),
            pl.BlockSpec((1, Q_LORA), lambda i, j: (0, 0)),
            pl.BlockSpec((heads * QK_PAD, Q_LORA), lambda i, j: (j, 0)),
            pl.BlockSpec((heads * LANES, Q_LORA), lambda i, j: (j, 0)),
            pl.BlockSpec((LANES, tm), lambda i, j: (0, i % per_seq)),
            pl.BlockSpec((LANES, tm), lambda i, j: (0, i % per_seq)),
        ],
        out_specs=pl.BlockSpec((heads * QK_PAD, tm), lambda i, j: (j, i)),
        out_shape=jax.ShapeDtypeStruct((MLA_HEADS * QK_PAD, T), jnp.bfloat16),
        scratch_shapes=[pltpu.VMEM((tm, Q_LORA), jnp.bfloat16)],
        compiler_params=_params("parallel", "arbitrary"),
        name="mla_q_proj",
    )(u, q_norm.reshape(1, Q_LORA), wa_t, wb_t, ct_t, st_t)


def _mla_kv_kernel(ckv_ref, g_ref, wk_ref, wvt_ref, kr_ref, tab_ref, k_ref, vt_ref):
    x = ckv_ref[...].astype(jnp.float32)
    ms = jnp.mean(x * x, axis=-1, keepdims=True)
    h = ((x * lax.rsqrt(ms + EPS)) * g_ref[...]).astype(jnp.bfloat16)
    kn = jnp.dot(h, wk_ref[...], preferred_element_type=jnp.float32)
    vt = lax.dot_general(wvt_ref[...], h, (((1,), (1,)), ((), ())),
                         preferred_element_type=jnp.float32).astype(vt_ref.dtype)
    ones = jnp.ones((VT_ROWS - V_DIM, vt.shape[1]), vt_ref.dtype)
    for hh in range(MLA_HEADS):
        vt_ref[hh * VT_ROWS:hh * VT_ROWS + V_DIM, :] = vt[hh * V_DIM:(hh + 1) * V_DIM, :]
        vt_ref[hh * VT_ROWS + V_DIM:(hh + 1) * VT_ROWS, :] = ones
    t = kr_ref[...].astype(jnp.float32) * tab_ref[...]
    rot = t + pltpu.roll(t, QK_ROPE, 1)
    lane = lax.broadcasted_iota(jnp.int32, rot.shape, 1)
    rot = jnp.where(lane < QK_ROPE, rot, jnp.where(lane == QK_ROPE, 1.0, 0.0)).astype(k_ref.dtype)
    for hh in range(MLA_HEADS):
        k_ref[:, hh * QK_PAD:hh * QK_PAD + LANES] = kn[:, hh * LANES:(hh + 1) * LANES].astype(k_ref.dtype)
        k_ref[:, hh * QK_PAD + LANES:(hh + 1) * QK_PAD] = rot


def _mla_kv(u, kv_norm, wk, wvt, tab, *, seq, tm):
    T = u.shape[0]
    tm = min(tm, seq)
    per_seq = seq // tm
    ckv_blk = B_CKV_OFF // KV_LORA
    kr_blk = B_KR_OFF // LANES
    return pl.pallas_call(
        _mla_kv_kernel,
        grid=(T // tm,),
        in_specs=[
            pl.BlockSpec((tm, KV_LORA), lambda i: (i, ckv_blk)),
            pl.BlockSpec((1, KV_LORA), lambda i: (0, 0)),
            pl.BlockSpec((KV_LORA, MLA_HEADS * QK_NOPE), lambda i: (0, 0)),
            pl.BlockSpec((MLA_HEADS * V_DIM, KV_LORA), lambda i: (0, 0)),
            pl.BlockSpec((tm, LANES), lambda i: (i, kr_blk)),
            pl.BlockSpec((tm, LANES), lambda i: (i % per_seq, 0)),
        ],
        out_specs=[pl.BlockSpec((tm, MLA_HEADS * QK_PAD), lambda i: (i, 0)),
                   pl.BlockSpec((MLA_HEADS * VT_ROWS, tm), lambda i: (0, i))],
        out_shape=[jax.ShapeDtypeStruct((T, MLA_HEADS * QK_PAD), jnp.bfloat16),
                   jax.ShapeDtypeStruct((MLA_HEADS * VT_ROWS, T), jnp.bfloat16)],
        compiler_params=_params("parallel"),
        name="mla_kv_proj",
    )(u, kv_norm.reshape(1, KV_LORA), wk, wvt, u, tab)


def _flash_head(q_ref, k_ref, vt_ref, o_ref, qt_buf, acc_ref, *, tk, n_kv):
    bf = jnp.bfloat16
    qt = q_ref[...]
    qt_buf[0] = qt
    qt_buf[1] = qt
    refs = {}

    def set_reference(t, value):
        ref = value.astype(bf)
        qt_buf[t % 2, AUG_ROW:AUG_ROW + BF16_ROWS, :] = jnp.broadcast_to(-ref, (BF16_ROWS, ref.shape[1]))
        refs[t] = ref.astype(jnp.float32)

    first = jnp.dot(k_ref[:BF16_ROWS, :], qt, preferred_element_type=jnp.float32)
    set_reference(0, jnp.max(first, axis=0, keepdims=True))
    m_true = worst = None
    for j in range(n_kv):
        k = k_ref[j * tk:(j + 1) * tk, :]
        s = jnp.dot(k, qt_buf[j % 2], preferred_element_type=jnp.float32)
        p = jnp.exp2(s.astype(bf))
        pmax = jnp.max(p, axis=0, keepdims=True).astype(jnp.float32)
        seen = refs[j] + jnp.log2(pmax)
        m_true = seen if j == 0 else jnp.maximum(m_true, seen)
        worst = pmax if j == 0 else jnp.maximum(worst, pmax)
        for t in ([1, 2] if j == 0 else [j + 2]):
            if t < n_kv:
                set_reference(t, m_true)
        pv = jnp.dot(vt_ref[:, j * tk:(j + 1) * tk], p, preferred_element_type=jnp.float32)
        if j == 0:
            acc_ref[...] = pv
        else:
            acc_ref[...] = jnp.exp2(refs[j - 1] - refs[j]) * acc_ref[...] + pv

    def finish():
        out = acc_ref[:V_DIM, :] / acc_ref[V_DIM:V_DIM + 1, :]
        o_ref[...] = out.T.astype(o_ref.dtype)

    finish()

    def exact():
        m_run = None
        for j in range(n_kv):
            s = jnp.dot(k_ref[j * tk:(j + 1) * tk, :], qt, preferred_element_type=jnp.float32)
            cmax = jnp.max(s, axis=0, keepdims=True)
            m_new = cmax if j == 0 else jnp.maximum(m_run, cmax)
            p = jnp.exp2((s - m_new).astype(bf))
            pv = jnp.dot(vt_ref[:, j * tk:(j + 1) * tk], p, preferred_element_type=jnp.float32)
            if j == 0:
                acc_ref[...] = pv
            else:
                acc_ref[...] = jnp.exp2(m_run - m_new) * acc_ref[...] + pv
            m_run = m_new
        finish()

    return worst, exact


def _flash_kernel(q_ref, k_ref, vt_ref, o_ref, qt_buf, acc_ref, *, tk, n_kv, heads):
    pending = []
    for h in range(heads):
        pending.append(_flash_head(
            q_ref.at[h * QK_PAD:(h + 1) * QK_PAD, :], k_ref.at[:, h * QK_PAD:(h + 1) * QK_PAD],
            vt_ref.at[h * VT_ROWS:(h + 1) * VT_ROWS, :], o_ref.at[:, h * V_DIM:(h + 1) * V_DIM],
            qt_buf.at[h], acc_ref.at[h], tk=tk, n_kv=n_kv))
    for worst, exact in pending:
        pl.when(jnp.logical_not(jnp.max(worst) <= 2.0 ** FAST_PATH_LIMIT))(exact)


def _flash_attention(qt, k, vt, *, batch, seq, tq, tk, heads):
    T = k.shape[0]
    tq = min(tq, seq)
    tk = min(tk, seq)
    nq = seq // tq
    return pl.pallas_call(
        functools.partial(_flash_kernel, tk=tk, n_kv=seq // tk, heads=heads),
        grid=(batch, MLA_HEADS // heads, nq),
        in_specs=[
            pl.BlockSpec((heads * QK_PAD, tq), lambda b, h, i: (h, b * nq + i)),
            pl.BlockSpec((seq, heads * QK_PAD), lambda b, h, i: (b, h)),
            pl.BlockSpec((heads * VT_ROWS, seq), lambda b, h, i: (h, b)),
        ],
        out_specs=pl.BlockSpec((tq, heads * V_DIM), lambda b, h, i: (b * nq + i, h)),
        out_shape=jax.ShapeDtypeStruct((T, MLA_HEADS * V_DIM), jnp.bfloat16),
        scratch_shapes=[pltpu.VMEM((heads, 2, QK_PAD, tq), jnp.bfloat16),
                        pltpu.VMEM((heads, VT_ROWS, tq), jnp.float32)],
        compiler_params=_params("parallel", "parallel", "arbitrary"),
        name="mla_flash_attention",
    )(qt, k, vt)


def _pool_kernel(cur_ref, prev_ref, next_ref, w_ref, sc_ref, o_ref, *, seq, tm):
    i = pl.program_id(1)
    n = pl.num_programs(1)
    win = tm + 2 * POOL_HALO
    keep_prev = (i > 0).astype(jnp.float32)
    keep_next = (i < n - 1).astype(jnp.float32)
    t = i * tm + lax.broadcasted_iota(jnp.int32, (tm, 1), 0)
    for g, w in enumerate(POOL_WINDOWS):
        lo = g * POOL_GROUP
        x = cur_ref[:, lo:lo + POOL_GROUP].astype(jnp.float32)
        p = prev_ref[:, lo:lo + POOL_GROUP].astype(jnp.float32) * keep_prev
        q = next_ref[:, lo:lo + POOL_GROUP].astype(jnp.float32) * keep_next
        run = jnp.concatenate([p, x, q], axis=0)
        span = 1
        while span < w:
            run = run + pltpu.roll(run, span, 0)
            span *= 2
        ahead = w - w // 2 - 1
        if ahead:
            run = pltpu.roll(run, win - ahead, 0)
        total = run[POOL_HALO:POOL_HALO + tm]
        count = jnp.minimum(t + (w - w // 2), seq) - jnp.maximum(t - w // 2, 0)
        pooled = total / count.astype(jnp.float32) - x
        y = jnp.dot(pooled.astype(jnp.bfloat16), w_ref[g], preferred_element_type=jnp.float32)
        o_ref[:, lo:lo + POOL_GROUP] = (y * sc_ref[:, lo:lo + POOL_GROUP]).astype(o_ref.dtype)


def _pool_mixer(u, w_group, scale, *, batch, seq, tm):
    T = u.shape[0]
    tm = min(tm, seq)
    n = seq // tm
    hb = tm // POOL_HALO
    last_halo = T // POOL_HALO - 1
    return pl.pallas_call(
        functools.partial(_pool_kernel, seq=seq, tm=tm),
        grid=(batch, n),
        in_specs=[
            pl.BlockSpec((tm, MIX_WIDTH), lambda b, i: (b * n + i, 0)),
            pl.BlockSpec((POOL_HALO, MIX_WIDTH),
                         lambda b, i: (jnp.maximum((b * n + i) * hb - 1, 0), 0)),
            pl.BlockSpec((POOL_HALO, MIX_WIDTH),
                         lambda b, i: (jnp.minimum((b * n + i + 1) * hb, last_halo), 0)),
            pl.BlockSpec((len(POOL_WINDOWS), POOL_GROUP, POOL_GROUP), lambda b, i: (0, 0, 0)),
            pl.BlockSpec((1, MIX_WIDTH), lambda b, i: (0, 0)),
        ],
        out_specs=pl.BlockSpec((tm, MIX_WIDTH), lambda b, i: (b * n + i, 0)),
        out_shape=jax.ShapeDtypeStruct((T, MIX_WIDTH), jnp.bfloat16),
        compiler_params=_params("parallel", "parallel"),
        name="pool_mixer",
    )(u, u, u, w_group, scale.reshape(1, MIX_WIDTH))


def _rope_tables(seq):
    inv_freq = 1.0 / (ROPE_THETA ** (jnp.arange(0, QK_ROPE, 2, dtype=jnp.float32) / QK_ROPE))
    ang = jnp.arange(seq, dtype=jnp.float32)[:, None] * inv_freq[None, :]
    cos, sin = jnp.cos(ang), jnp.sin(ang)
    zeros = jnp.zeros((seq, LANES - QK_ROPE), jnp.float32)
    ct = jnp.concatenate([cos, cos, zeros], axis=-1).T
    st = jnp.concatenate([-sin, sin, zeros], axis=-1).T
    tab = jnp.concatenate([cos, cos, -sin, sin], axis=-1)
    return ct, st, tab


def _swap_halves(w):
    half = QK_ROPE // 2
    return jnp.concatenate([w[..., half:], w[..., :half]], axis=-1)


def _prep_b(b_w_in, b_w_q_up, b_w_kv_up):
    bf = jnp.bfloat16
    K = b_w_in.shape[0]
    o1 = Q_LORA
    o2 = o1 + KV_LORA
    o3 = o2 + QK_ROPE
    o4 = o3 + XA_WIDTH
    w_cq, w_ckv, w_kr, w_xq, w_gate = (b_w_in[:, :o1], b_w_in[:, o1:o2], b_w_in[:, o2:o3],
                                       b_w_in[:, o3:o4], b_w_in[:, o4:])
    w_in = jnp.concatenate([w_gate, w_xq, w_cq, w_ckv, w_kr, _swap_halves(w_kr),
                            jnp.zeros((K, LANES), b_w_in.dtype)], axis=-1).astype(bf)
    wq = b_w_q_up.reshape(Q_LORA, MLA_HEADS, QK_NOPE + QK_ROPE)
    q_nope, q_rope = wq[..., :QK_NOPE], wq[..., QK_NOPE:]
    pad = jnp.zeros((Q_LORA, MLA_HEADS, LANES - QK_ROPE), wq.dtype)
    wa = jnp.concatenate([q_nope, q_rope, pad], axis=-1).reshape(Q_LORA, MLA_HEADS * QK_PAD).T.astype(bf)
    wb = jnp.concatenate([_swap_halves(q_rope), pad], axis=-1).reshape(Q_LORA, MLA_HEADS * LANES).T.astype(bf)
    wkv = b_w_kv_up.reshape(KV_LORA, MLA_HEADS, QK_NOPE + V_DIM)
    wk = wkv[..., :QK_NOPE].reshape(KV_LORA, MLA_HEADS * QK_NOPE).astype(bf)
    wvt = wkv[..., QK_NOPE:].reshape(KV_LORA, MLA_HEADS * V_DIM).T.astype(bf)
    return w_in, wa, wb, wk, wvt


def _prep_lru(gate_w, gate_b):
    wg = jnp.concatenate([gate_w[:, 0], gate_w[:, 1]], axis=-1).astype(jnp.bfloat16)
    gb = gate_b.reshape(2, 2, LRU_BLOCKS, 1, LRU_BLOCK)
    gb = 0.5 * jnp.concatenate([gb[:, 0], gb[:, 1]], axis=-1)
    return wg, gb


def _trunk(x, mem, w):
    batch, seq, _ = x.shape
    T = batch * seq
    x = x.reshape(T, D_MODEL)
    mem = mem.reshape(batch * MEM_LEN, D_MODEL)
    ct, st, tab = _rope_tables(seq)
    depth = w["norm_pre"].shape[0]
    for i in range(depth):
        kind, j = i % N_MIXERS, i // N_MIXERS
        kv = _norm_matmul(mem, w["norm_mem"][i], w["w_mem_kv"], i, tm=512, tn=1024)
        if kind == 0:
            u = _norm_matmul(x, w["norm_pre"][i], w["a_w_in"], j, tm=1024, tn=2048)
            mixes = _lru_mixer(u, w["a_conv_w"][j], w["a_conv_b"][j], w["a_wg"][j], w["a_gb"][j],
                               w["a_lambda"][j], batch=batch, seq=seq, rows=256)
            xq_off, gate_off = MIX_WIDTH, BRANCH
        elif kind == 1:
            u = _norm_matmul(x, w["norm_pre"][i], w["b_w_in"], j, tm=1024, tn=2048)
            q = _mla_q(u, w["b_q_norm"][j], w["b_wa"][j], w["b_wb"][j], ct, st,
                       seq=seq, tm=1024, heads=8)
            k, vt = _mla_kv(u, w["b_kv_norm"][j], w["b_wk"][j], w["b_wvt"][j], tab, seq=seq, tm=512)
            mixes = [_flash_attention(q, k, vt, batch=batch, seq=seq, tq=1024, tk=512, heads=1)]
            xq_off, gate_off = B_XQ_OFF, B_GATE_OFF
        else:
            u = _norm_matmul(x, w["norm_pre"][i], w["c_w_in"], j, tm=1024, tn=2048)
            mixes = [_pool_mixer(u, w["c_w_group"][j], w["c_scale"][j], batch=batch, seq=seq, tm=512)]
            xq_off, gate_off = MIX_WIDTH, BRANCH
        x = _out_block(mixes, u, xq_off, gate_off, kv, w["w_out"], i, x, w["norm_post"][i],
                       seq=seq, tm=256)
    return x.reshape(batch, seq, D_MODEL)


def kernel(x_prompt, x_sample, mem_prompt, mem_sample, norm_pre, norm_post, norm_mem, w_mem_kv, w_out,
           a_w_in, a_conv_w, a_conv_b, a_gate_w, a_gate_b, a_lambda,
           b_w_in, b_q_norm, b_kv_norm, b_w_q_up, b_w_kv_up,
           c_w_in, c_w_group, c_scale):
    bf = jnp.bfloat16
    b_parts = [_prep_b(b_w_in[j], b_w_q_up[j], b_w_kv_up[j]) for j in range(b_w_in.shape[0])]
    lru_parts = [_prep_lru(a_gate_w[j], a_gate_b[j]) for j in range(a_gate_w.shape[0])]
    w = dict(
        norm_pre=norm_pre, norm_post=norm_post, norm_mem=norm_mem,
        w_mem_kv=w_mem_kv.astype(bf), w_out=w_out.astype(bf),
        a_w_in=a_w_in.astype(bf), a_conv_w=a_conv_w, a_conv_b=a_conv_b,
        a_wg=[p[0] for p in lru_parts], a_gb=[p[1] for p in lru_parts], a_lambda=a_lambda,
        b_w_in=jnp.stack([p[0] for p in b_parts]), b_q_norm=b_q_norm, b_kv_norm=b_kv_norm,
        b_wa=[p[1] for p in b_parts], b_wb=[p[2] for p in b_parts],
        b_wk=[p[3] for p in b_parts], b_wvt=[p[4] for p in b_parts],
        c_w_in=c_w_in.astype(bf), c_w_group=[c_w_group[j].astype(bf) for j in range(c_w_group.shape[0])],
        c_scale=c_scale,
    )
    return (_trunk(x_prompt, mem_prompt, w), _trunk(x_sample, mem_sample, w))
```

```python
import functools
import math

import jax
import jax.numpy as jnp
from jax import lax
from jax.experimental import pallas as pl
from jax.experimental.pallas import tpu as pltpu

D_MODEL = 2048
MIX_WIDTH = 3 * D_MODEL // 2
XA_HEADS = 4
XA_HEAD_DIM = D_MODEL // 8
XA_WIDTH = XA_HEADS * XA_HEAD_DIM
BRANCH = MIX_WIDTH + XA_WIDTH
MEM_LEN = 256
N_MIXERS = 3
CONV_W = 4
CONV_LEFT = 2
LRU_BLOCKS = 12
LRU_BLOCK = MIX_WIDTH // LRU_BLOCKS
LRU_C = 8.0
QK_NOPE = 128
QK_ROPE = 64
V_DIM = 128
MLA_HEADS = MIX_WIDTH // V_DIM
Q_LORA = D_MODEL // 4
KV_LORA = D_MODEL // 8
ROPE_THETA = 10000.0
POOL_WINDOWS = (2, 4, 8, 16)
POOL_GROUP = MIX_WIDTH // len(POOL_WINDOWS)
EPS = 1e-6

LANES = 128
SUBLANES = 8
BF16_ROWS = 16
QK_PAD = 256
AUG_ROW = QK_NOPE + QK_ROPE
FAST_PATH_LIMIT = 64.0
VT_ROWS = V_DIM + BF16_ROWS
VMEM_LIMIT = 56 * 1024 * 1024

B_GATE_OFF = 0
B_XQ_OFF = BRANCH
B_CQ_OFF = B_XQ_OFF + XA_WIDTH
B_CKV_OFF = B_CQ_OFF + Q_LORA
B_KR_OFF = B_CKV_OFF + KV_LORA
B_IN_WIDTH = B_KR_OFF + 2 * QK_ROPE + LANES

LRU_CH = 1024
LRU_LANE_GROUPS = LRU_CH // LANES
LRU_HALO = BF16_ROWS
LRU_STEP = 4
RSQRT_FLOOR = 1e-30

POOL_HALO = BF16_ROWS


def _params(*sem, flags=None):
    return pltpu.CompilerParams(dimension_semantics=sem, vmem_limit_bytes=VMEM_LIMIT, flags=flags)


def _resident(block_shape, index_map):
    return pl.BlockSpec(block_shape, index_map, pipeline_mode=pl.Buffered(1))


def _norm_matmul_kernel(x_ref, g_ref, w_ref, o_ref, h_ref):
    @pl.when(pl.program_id(1) == 0)
    def _():
        x = x_ref[...].astype(jnp.float32)
        ms = jnp.mean(x * x, axis=-1, keepdims=True)
        h_ref[...] = ((x * lax.rsqrt(ms + EPS)) * g_ref[...]).astype(h_ref.dtype)

    o_ref[...] = jnp.dot(h_ref[...], w_ref[...],
                         preferred_element_type=jnp.float32).astype(o_ref.dtype)


def _norm_matmul(x, g, w, layer, *, tm, tn):
    T, K = x.shape
    N = w.shape[2]
    tm = min(tm, T)
    return pl.pallas_call(
        _norm_matmul_kernel,
        grid=(T // tm, N // tn),
        in_specs=[
            pl.BlockSpec((tm, K), lambda i, j: (i, 0)),
            pl.BlockSpec((1, K), lambda i, j: (0, 0)),
            pl.BlockSpec((None, K, tn), lambda i, j: (layer, 0, j)),
        ],
        out_specs=pl.BlockSpec((tm, tn), lambda i, j: (i, j)),
        out_shape=jax.ShapeDtypeStruct((T, N), jnp.bfloat16),
        scratch_shapes=[pltpu.VMEM((tm, K), jnp.bfloat16)],
        compiler_params=_params("parallel", "arbitrary"),
        name="norm_matmul",
    )(x, g.reshape(1, K), w)


def _out_kernel(*refs, n_mix):
    mix_refs = refs[:n_mix]
    xq_ref, gate_ref, kv_ref, w_ref, x_ref, g_ref, o_ref = refs[n_mix:]

    mix = mix_refs[0][...]
    for r in mix_refs[1:]:
        mix = mix + r[...]

    xq = xq_ref[...]
    scale = XA_HEAD_DIM ** -0.5 * math.log2(math.e)
    xa = []
    for h in range(XA_HEADS):
        lo = h * XA_HEAD_DIM
        q = xq[:, lo:lo + XA_HEAD_DIM]
        k = kv_ref[:, lo:lo + XA_HEAD_DIM]
        v = kv_ref[:, XA_WIDTH + lo:XA_WIDTH + lo + XA_HEAD_DIM]
        s = lax.dot_general(q, k, (((1,), (1,)), ((), ())), preferred_element_type=jnp.float32)
        m = jnp.max(s, axis=-1, keepdims=True)
        p = jnp.exp2((s - m) * scale)
        l = jnp.sum(p, axis=-1, keepdims=True)
        p = (p / l).astype(jnp.bfloat16)
        xa.append(jnp.dot(p, v, preferred_element_type=jnp.float32))

    half_gate = 0.5 * gate_ref[...].astype(jnp.float32)
    act = (half_gate * jnp.tanh(half_gate) + half_gate).astype(jnp.bfloat16)
    acc = jnp.dot(mix * act[:, :MIX_WIDTH], w_ref[:MIX_WIDTH, :], preferred_element_type=jnp.float32)
    xa = jnp.concatenate([a.astype(jnp.bfloat16) for a in xa], axis=-1)
    acc = acc + jnp.dot(xa * act[:, MIX_WIDTH:], w_ref[MIX_WIDTH:, :],
                        preferred_element_type=jnp.float32)

    ms = jnp.mean(acc * acc, axis=-1, keepdims=True)
    o_ref[...] = x_ref[...] + (acc * lax.rsqrt(ms + EPS)) * g_ref[...]


def _out_block(mixes, u, xq_off, gate_off, kv, w_out, layer, x, g_post, *, seq, tm):
    T = x.shape[0]
    tm = min(tm, seq)
    per_seq = seq // tm
    xq_blk = xq_off // XA_WIDTH
    gate_blk = gate_off // BRANCH
    mix_spec = pl.BlockSpec((tm, MIX_WIDTH), lambda i: (i, 0))
    return pl.pallas_call(
        functools.partial(_out_kernel, n_mix=len(mixes)),
        grid=(T // tm,),
        in_specs=[mix_spec] * len(mixes) + [
            pl.BlockSpec((tm, XA_WIDTH), lambda i: (i, xq_blk)),
            pl.BlockSpec((tm, BRANCH), lambda i: (i, gate_blk)),
            pl.BlockSpec((MEM_LEN, 2 * XA_WIDTH), lambda i: (i // per_seq, 0)),
            _resident((None, BRANCH, D_MODEL), lambda i: (layer, 0, 0)),
            pl.BlockSpec((tm, D_MODEL), lambda i: (i, 0)),
            pl.BlockSpec((1, D_MODEL), lambda i: (0, 0)),
        ],
        out_specs=pl.BlockSpec((tm, D_MODEL), lambda i: (i, 0)),
        out_shape=jax.ShapeDtypeStruct((T, D_MODEL), jnp.float32),
        compiler_params=_params("parallel"),
        name="xattn_gate_outproj",
    )(*mixes, u, u, kv, w_out, x, g_post.reshape(1, D_MODEL))


def _lru_kernel(cur_f, prev_f, next_f, cur_r, prev_r, next_r, cw_ref, cb_ref, wg_ref,
                gb_ref, lam_ref, hf_ref, hr_ref, w_f, w_r, a_f, b_f, h_f, a_r, b_r, h_r, carry,
                *, rows, pitch):
    s = pl.program_id(2)
    n = pl.num_programs(2)

    @pl.when(s == 0)
    def _():
        carry[...] = jnp.zeros_like(carry)

    win = rows + 2 * LRU_HALO

    def gates(cur, prev, nxt, chunk, d, w_s, a_s, b_s):
        p = prev[...].astype(jnp.float32) * (chunk > 0).astype(jnp.float32)
        q = nxt[...].astype(jnp.float32) * (chunk < n - 1).astype(jnp.float32)
        w = jnp.concatenate([p, cur[...].astype(jnp.float32), q], axis=0)
        half_cw = 0.5 * cw_ref[...]
        half_cb = 0.5 * cb_ref[...]
        half_u = []
        for grp in range(LRU_LANE_GROUPS):
            lanes = slice(grp * LANES, (grp + 1) * LANES)
            w_s[pl.ds(grp * 2 * win, win, stride=2), :] = w[:, lanes]
            acc = half_cb[:, lanes]
            for k in range(CONV_W):
                first = grp * 2 * win + 2 * (LRU_HALO - CONV_LEFT + k)
                acc = acc + w_s[pl.ds(first, rows, stride=2), :] * half_cw[k:k + 1, lanes]
            half_u.append(acc)
        half_u = jnp.concatenate(half_u, axis=-1)
        lam = lam_ref[d]
        softplus = jnp.maximum(-lam, 0.0) + jnp.log(1.0 + jnp.exp(-jnp.abs(lam)))
        half_decay = (-0.5 * LRU_C * math.log2(math.e)) * softplus
        for blk in range(LRU_CH // LRU_BLOCK):
            lo = blk * LRU_BLOCK
            hu = half_u[:, lo:lo + LRU_BLOCK]
            g = jnp.dot(hu.astype(jnp.bfloat16), wg_ref[d, blk],
                        preferred_element_type=jnp.float32) + gb_ref[d, blk]
            hd = half_decay[:, lo:lo + LRU_BLOCK]
            a = jnp.exp2(hd * jnp.tanh(g[:, :LRU_BLOCK]) + hd)
            iu = hu * jnp.tanh(g[:, LRU_BLOCK:]) + hu
            y = 1.0 - a * a
            b = (y * lax.rsqrt(jnp.maximum(y, RSQRT_FLOOR))) * iu
            for half in range(LRU_BLOCK // LANES):
                slab_rows = pl.ds((blk * (LRU_BLOCK // LANES) + half) * pitch, rows, stride=2)
                a_s[slab_rows, :] = a[:, half * LANES:(half + 1) * LANES]
                b_s[slab_rows, :] = b[:, half * LANES:(half + 1) * LANES]

    gates(cur_f, prev_f, next_f, s, 0, w_f, a_f, b_f)
    gates(cur_r, prev_r, next_r, n - 1 - s, 1, w_r, a_r, b_r)

    def block(a_s, b_s, h_s, h, t0, sign):
        idx = [pl.ds(2 * (t0 + sign * j), LRU_LANE_GROUPS, stride=pitch) for j in range(LRU_STEP)]
        a = [a_s[i, :] for i in idx]
        b = [b_s[i, :] for i in idx]
        a01, b01 = a[1] * a[0], a[1] * b[0] + b[1]
        a23, b23 = a[3] * a[2], a[3] * b[2] + b[3]
        a03, b03 = a23 * a01, a23 * b01 + b23
        h0 = a[0] * h + b[0]
        h1 = a01 * h + b01
        h3 = a03 * h + b03
        h2 = a[2] * h1 + b[2]
        for i, v in zip(idx, (h0, h1, h2, h3)):
            h_s[i, :] = v
        return h3

    blocks_per_iter = 8

    def body(it, hs):
        hf, hr = hs
        for j in range(blocks_per_iter):
            t = (it * blocks_per_iter + j) * LRU_STEP
            hf = block(a_f, b_f, h_f, hf, t, 1)
            hr = block(a_r, b_r, h_r, hr, rows - 1 - t, -1)
        return hf, hr

    hf, hr = lax.fori_loop(0, rows // (LRU_STEP * blocks_per_iter), body, (carry[0], carry[1]))
    carry[0] = hf
    carry[1] = hr

    for grp in range(LRU_LANE_GROUPS):
        slab_rows = pl.ds(grp * pitch, rows, stride=2)
        hf_ref[:, grp * LANES:(grp + 1) * LANES] = h_f[slab_rows, :].astype(hf_ref.dtype)
        hr_ref[:, grp * LANES:(grp + 1) * LANES] = h_r[slab_rows, :].astype(hr_ref.dtype)


def _lru_mixer(u, conv_w, conv_b, wg, gb, lam, *, batch, seq, rows):
    T = u.shape[0]
    rows = min(rows, seq)
    n = seq // rows
    pitch = 2 * rows + 4
    hb = rows // LRU_HALO
    last_halo = T // LRU_HALO - 1

    def cur_map(rev):
        def f(b, c, s):
            chunk = (n - 1 - s) if rev else s
            return (b * n + chunk, c)
        return f

    def prev_map(rev):
        def f(b, c, s):
            chunk = (n - 1 - s) if rev else s
            return (jnp.maximum((b * n + chunk) * hb - 1, 0), c)
        return f

    def next_map(rev):
        def f(b, c, s):
            chunk = (n - 1 - s) if rev else s
            return (jnp.minimum((b * n + chunk + 1) * hb, last_halo), c)
        return f

    blocks = LRU_CH // LRU_BLOCK
    in_specs = []
    for rev in (False, True):
        in_specs += [
            pl.BlockSpec((rows, LRU_CH), cur_map(rev)),
            pl.BlockSpec((LRU_HALO, LRU_CH), prev_map(rev)),
            pl.BlockSpec((LRU_HALO, LRU_CH), next_map(rev)),
        ]
    in_specs += [
        pl.BlockSpec((CONV_W, LRU_CH), lambda b, c, s: (0, c)),
        pl.BlockSpec((1, LRU_CH), lambda b, c, s: (0, c)),
        pl.BlockSpec((2, blocks, LRU_BLOCK, 2 * LRU_BLOCK), lambda b, c, s: (0, c, 0, 0)),
        pl.BlockSpec((2, blocks, 1, 2 * LRU_BLOCK), lambda b, c, s: (0, c, 0, 0)),
        pl.BlockSpec((2, 1, LRU_CH), lambda b, c, s: (0, 0, c)),
    ]
    slab = pltpu.VMEM((LRU_LANE_GROUPS * pitch, LANES), jnp.float32)
    window = pltpu.VMEM((LRU_LANE_GROUPS * 2 * (rows + 2 * LRU_HALO), LANES), jnp.float32)
    out = jax.ShapeDtypeStruct((T, MIX_WIDTH), jnp.bfloat16)
    return pl.pallas_call(
        functools.partial(_lru_kernel, rows=rows, pitch=pitch),
        grid=(batch, MIX_WIDTH // LRU_CH, n),
        in_specs=in_specs,
        out_specs=[pl.BlockSpec((rows, LRU_CH), cur_map(False)),
                   pl.BlockSpec((rows, LRU_CH), cur_map(True))],
        out_shape=[out, out],
        scratch_shapes=[window, window, slab, slab, slab, slab, slab, slab,
                        pltpu.VMEM((2, LRU_LANE_GROUPS, LANES), jnp.float32)],
        compiler_params=_params("parallel", "parallel", "arbitrary"),
        name="rglru",
    )(u, u, u, u, u, u, conv_w, conv_b.reshape(1, MIX_WIDTH), wg, gb,
      lam.reshape(2, 1, MIX_WIDTH))


def _mla_q_kernel(cq_ref, g_ref, wa_ref, wb_ref, ct_ref, st_ref, o_ref, h_ref, *, heads):
    @pl.when(pl.program_id(1) == 0)
    def _():
        x = cq_ref[...].astype(jnp.float32)
        ms = jnp.mean(x * x, axis=-1, keepdims=True)
        h_ref[...] = ((x * lax.rsqrt(ms + EPS)) * g_ref[...]).astype(h_ref.dtype)

    h = h_ref[...]
    nt = (((1,), (1,)), ((), ()))
    qa = lax.dot_general(wa_ref[...], h, nt, preferred_element_type=jnp.float32)
    qb = lax.dot_general(wb_ref[...], h, nt, preferred_element_type=jnp.float32)
    scale = (QK_NOPE + QK_ROPE) ** -0.5 * math.log2(math.e)
    ct = ct_ref[...] * scale
    st = st_ref[...] * scale
    for hh in range(heads):
        lo = hh * QK_PAD
        o_ref[lo:lo + LANES, :] = (qa[lo:lo + LANES] * scale).astype(o_ref.dtype)
        rope = qa[lo + LANES:lo + 2 * LANES] * ct + qb[hh * LANES:(hh + 1) * LANES] * st
        o_ref[lo + LANES:lo + 2 * LANES, :] = rope.astype(o_ref.dtype)


def _mla_q(u, q_norm, wa_t, wb_t, ct_t, st_t, *, seq, tm, heads):
    T = u.shape[0]
    tm = min(tm, seq)
    per_seq = seq // tm
    cq_blk = B_CQ_OFF // Q_LORA
    return pl.pallas_call(
        functools.partial(_mla_q_kernel, heads=heads),
        grid=(T // tm, MLA_HEADS // heads),
        in_specs=[
            pl.BlockSpec((tm, Q_LORA), lambda i, j: (i, cq_blk)),
            pl.BlockSpec((1, Q_LORA), lambda i, j: (0, 0)),
            pl.BlockSpec((heads * QK_PAD, Q_LORA), lambda i, j: (j, 0)),
            pl.BlockSpec((heads * LANES, Q_LORA), lambda i, j: (j, 0)),
            pl.BlockSpec((LANES, tm), lambda i, j: (0, i % per_seq)),
            pl.BlockSpec((LANES, tm), lambda i, j: (0, i % per_seq)),
        ],
        out_specs=pl.BlockSpec((heads * QK_PAD, tm), lambda i, j: (j, i)),
        out_shape=jax.ShapeDtypeStruct((MLA_HEADS * QK_PAD, T), jnp.bfloat16),
        scratch_shapes=[pltpu.VMEM((tm, Q_LORA), jnp.bfloat16)],
        compiler_params=_params("parallel", "arbitrary"),
        name="mla_q_proj",
    )(u, q_norm.reshape(1, Q_LORA), wa_t, wb_t, ct_t, st_t)


def _mla_kv_kernel(ckv_ref, g_ref, wk_ref, wvt_ref, kr_ref, tab_ref, k_ref, vt_ref):
    x = ckv_ref[...].astype(jnp.float32)
    ms = jnp.mean(x * x, axis=-1, keepdims=True)
    h = ((x * lax.rsqrt(ms + EPS)) * g_ref[...]).astype(jnp.bfloat16)
    kn = jnp.dot(h, wk_ref[...], preferred_element_type=jnp.float32)
    vt = lax.dot_general(wvt_ref[...], h, (((1,), (1,)), ((), ())),
                         preferred_element_type=jnp.float32).astype(vt_ref.dtype)
    ones = jnp.ones((VT_ROWS - V_DIM, vt.shape[1]), vt_ref.dtype)
    for hh in range(MLA_HEADS):
        vt_ref[hh * VT_ROWS:hh * VT_ROWS + V_DIM, :] = vt[hh * V_DIM:(hh + 1) * V_DIM, :]
        vt_ref[hh * VT_ROWS + V_DIM:(hh + 1) * VT_ROWS, :] = ones
    t = kr_ref[...].astype(jnp.float32) * tab_ref[...]
    rot = t + pltpu.roll(t, QK_ROPE, 1)
    lane = lax.broadcasted_iota(jnp.int32, rot.shape, 1)
    rot = jnp.where(lane < QK_ROPE, rot, jnp.where(lane == QK_ROPE, 1.0, 0.0)).astype(k_ref.dtype)
    for hh in range(MLA_HEADS):
        k_ref[:, hh * QK_PAD:hh * QK_PAD + LANES] = kn[:, hh * LANES:(hh + 1) * LANES].astype(k_ref.dtype)
        k_ref[:, hh * QK_PAD + LANES:(hh + 1) * QK_PAD] = rot


def _mla_kv(u, kv_norm, wk, wvt, tab, *, seq, tm):
    T = u.shape[0]
    tm = min(tm, seq)
    per_seq = seq // tm
    ckv_blk = B_CKV_OFF // KV_LORA
    kr_blk = B_KR_OFF // LANES
    return pl.pallas_call(
        _mla_kv_kernel,
        grid=(T // tm,),
        in_specs=[
            pl.BlockSpec((tm, KV_LORA), lambda i: (i, ckv_blk)),
            pl.BlockSpec((1, KV_LORA), lambda i: (0, 0)),
            pl.BlockSpec((KV_LORA, MLA_HEADS * QK_NOPE), lambda i: (0, 0)),
            pl.BlockSpec((MLA_HEADS * V_DIM, KV_LORA), lambda i: (0, 0)),
            pl.BlockSpec((tm, LANES), lambda i: (i, kr_blk)),
            pl.BlockSpec((tm, LANES), lambda i: (i % per_seq, 0)),
        ],
        out_specs=[pl.BlockSpec((tm, MLA_HEADS * QK_PAD), lambda i: (i, 0)),
                   pl.BlockSpec((MLA_HEADS * VT_ROWS, tm), lambda i: (0, i))],
        out_shape=[jax.ShapeDtypeStruct((T, MLA_HEADS * QK_PAD), jnp.bfloat16),
                   jax.ShapeDtypeStruct((MLA_HEADS * VT_ROWS, T), jnp.bfloat16)],
        compiler_params=_params("parallel"),
        name="mla_kv_proj",
    )(u, kv_norm.reshape(1, KV_LORA), wk, wvt, u, tab)


def _flash_head(q_ref, k_ref, vt_ref, o_ref, qt_buf, acc_ref, *, tk, n_kv):
    bf = jnp.bfloat16
    qt = q_ref[...]
    qt_buf[0] = qt
    qt_buf[1] = qt
    refs = {}

    def set_reference(t, value):
        ref = value.astype(bf)
        qt_buf[t % 2, AUG_ROW:AUG_ROW + BF16_ROWS, :] = jnp.broadcast_to(-ref, (BF16_ROWS, ref.shape[1]))
        refs[t] = ref.astype(jnp.float32)

    first = jnp.dot(k_ref[:BF16_ROWS, :], qt, preferred_element_type=jnp.float32)
    set_reference(0, jnp.max(first, axis=0, keepdims=True))
    m_true = worst = None
    for j in range(n_kv):
        k = k_ref[j * tk:(j + 1) * tk, :]
        s = jnp.dot(k, qt_buf[j % 2], preferred_element_type=jnp.float32)
        p = jnp.exp2(s.astype(bf))
        pmax = jnp.max(p, axis=0, keepdims=True).astype(jnp.float32)
        seen = refs[j] + jnp.log2(pmax)
        m_true = seen if j == 0 else jnp.maximum(m_true, seen)
        worst = pmax if j == 0 else jnp.maximum(worst, pmax)
        for t in ([1, 2] if j == 0 else [j + 2]):
            if t < n_kv:
                set_reference(t, m_true)
        pv = jnp.dot(vt_ref[:, j * tk:(j + 1) * tk], p, preferred_element_type=jnp.float32)
        if j == 0:
            acc_ref[...] = pv
        else:
            acc_ref[...] = jnp.exp2(refs[j - 1] - refs[j]) * acc_ref[...] + pv

    def finish():
        out = acc_ref[:V_DIM, :] / acc_ref[V_DIM:V_DIM + 1, :]
        o_ref[...] = out.T.astype(o_ref.dtype)

    finish()

    def exact():
        m_run = None
        for j in range(n_kv):
            s = jnp.dot(k_ref[j * tk:(j + 1) * tk, :], qt, preferred_element_type=jnp.float32)
            cmax = jnp.max(s, axis=0, keepdims=True)
            m_new = cmax if j == 0 else jnp.maximum(m_run, cmax)
            p = jnp.exp2((s - m_new).astype(bf))
            pv = jnp.dot(vt_ref[:, j * tk:(j + 1) * tk], p, preferred_element_type=jnp.float32)
            if j == 0:
                acc_ref[...] = pv
            else:
                acc_ref[...] = jnp.exp2(m_run - m_new) * acc_ref[...] + pv
            m_run = m_new
        finish()

    return worst, exact


def _flash_kernel(q_ref, k_ref, vt_ref, o_ref, qt_buf, acc_ref, *, tk, n_kv, heads):
    pending = []
    for h in range(heads):
        pending.append(_flash_head(
            q_ref.at[h * QK_PAD:(h + 1) * QK_PAD, :], k_ref.at[:, h * QK_PAD:(h + 1) * QK_PAD],
            vt_ref.at[h * VT_ROWS:(h + 1) * VT_ROWS, :], o_ref.at[:, h * V_DIM:(h + 1) * V_DIM],
            qt_buf.at[h], acc_ref.at[h], tk=tk, n_kv=n_kv))
    for worst, exact in pending:
        pl.when(jnp.logical_not(jnp.max(worst) <= 2.0 ** FAST_PATH_LIMIT))(exact)


def _flash_attention(qt, k, vt, *, batch, seq, tq, tk, heads):
    T = k.shape[0]
    tq = min(tq, seq)
    tk = min(tk, seq)
    nq = seq // tq
    return pl.pallas_call(
        functools.partial(_flash_kernel, tk=tk, n_kv=seq // tk, heads=heads),
        grid=(batch, MLA_HEADS // heads, nq),
        in_specs=[
            pl.BlockSpec((heads * QK_PAD, tq), lambda b, h, i: (h, b * nq + i)),
            pl.BlockSpec((seq, heads * QK_PAD), lambda b, h, i: (b, h)),
            pl.BlockSpec((heads * VT_ROWS, seq), lambda b, h, i: (h, b)),
        ],
        out_specs=pl.BlockSpec((tq, heads * V_DIM), lambda b, h, i: (b * nq + i, h)),
        out_shape=jax.ShapeDtypeStruct((T, MLA_HEADS * V_DIM), jnp.bfloat16),
        scratch_shapes=[pltpu.VMEM((heads, 2, QK_PAD, tq), jnp.bfloat16),
                        pltpu.VMEM((heads, VT_ROWS, tq), jnp.float32)],
        compiler_params=_params("parallel", "parallel", "arbitrary"),
        name="mla_flash_attention",
    )(qt, k, vt)


def _pool_kernel(cur_ref, prev_ref, next_ref, w_ref, sc_ref, o_ref, *, seq, tm):
    i = pl.program_id(1)
    n = pl.num_programs(1)
    win = tm + 2 * POOL_HALO
    keep_prev = (i > 0).astype(jnp.float32)
    keep_next = (i < n - 1).astype(jnp.float32)
    t = i * tm + lax.broadcasted_iota(jnp.int32, (tm, 1), 0)
    for g, w in enumerate(POOL_WINDOWS):
        lo = g * POOL_GROUP
        x = cur_ref[:, lo:lo + POOL_GROUP].astype(jnp.float32)
        p = prev_ref[:, lo:lo + POOL_GROUP].astype(jnp.float32) * keep_prev
        q = next_ref[:, lo:lo + POOL_GROUP].astype(jnp.float32) * keep_next
        run = jnp.concatenate([p, x, q], axis=0)
        span = 1
        while span < w:
            run = run + pltpu.roll(run, span, 0)
            span *= 2
        ahead = w - w // 2 - 1
        if ahead:
            run = pltpu.roll(run, win - ahead, 0)
        total = run[POOL_HALO:POOL_HALO + tm]
        count = jnp.minimum(t + (w - w // 2), seq) - jnp.maximum(t - w // 2, 0)
        pooled = total / count.astype(jnp.float32) - x
        y = jnp.dot(pooled.astype(jnp.bfloat16), w_ref[g], preferred_element_type=jnp.float32)
        o_ref[:, lo:lo + POOL_GROUP] = (y * sc_ref[:, lo:lo + POOL_GROUP]).astype(o_ref.dtype)


def _pool_mixer(u, w_group, scale, *, batch, seq, tm):
    T = u.shape[0]
    tm = min(tm, seq)
    n = seq // tm
    hb = tm // POOL_HALO
    last_halo = T // POOL_HALO - 1
    return pl.pallas_call(
        functools.partial(_pool_kernel, seq=seq, tm=tm),
        grid=(batch, n),
        in_specs=[
            pl.BlockSpec((tm, MIX_WIDTH), lambda b, i: (b * n + i, 0)),
            pl.BlockSpec((POOL_HALO, MIX_WIDTH),
                         lambda b, i: (jnp.maximum((b * n + i) * hb - 1, 0), 0)),
            pl.BlockSpec((POOL_HALO, MIX_WIDTH),
                         lambda b, i: (jnp.minimum((b * n + i + 1) * hb, last_halo), 0)),
            pl.BlockSpec((len(POOL_WINDOWS), POOL_GROUP, POOL_GROUP), lambda b, i: (0, 0, 0)),
            pl.BlockSpec((1, MIX_WIDTH), lambda b, i: (0, 0)),
        ],
        out_specs=pl.BlockSpec((tm, MIX_WIDTH), lambda b, i: (b * n + i, 0)),
        out_shape=jax.ShapeDtypeStruct((T, MIX_WIDTH), jnp.bfloat16),
        compiler_params=_params("parallel", "parallel"),
        name="pool_mixer",
    )(u, u, u, w_group, scale.reshape(1, MIX_WIDTH))


def _rope_tables(seq):
    inv_freq = 1.0 / (ROPE_THETA ** (jnp.arange(0, QK_ROPE, 2, dtype=jnp.float32) / QK_ROPE))
    ang = jnp.arange(seq, dtype=jnp.float32)[:, None] * inv_freq[None, :]
    cos, sin = jnp.cos(ang), jnp.sin(ang)
    zeros = jnp.zeros((seq, LANES - QK_ROPE), jnp.float32)
    ct = jnp.concatenate([cos, cos, zeros], axis=-1).T
    st = jnp.concatenate([-sin, sin, zeros], axis=-1).T
    tab = jnp.concatenate([cos, cos, -sin, sin], axis=-1)
    return ct, st, tab


def _swap_halves(w):
    half = QK_ROPE // 2
    return jnp.concatenate([w[..., half:], w[..., :half]], axis=-1)


def _prep_b(b_w_in, b_w_q_up, b_w_kv_up):
    bf = jnp.bfloat16
    K = b_w_in.shape[0]
    o1 = Q_LORA
    o2 = o1 + KV_LORA
    o3 = o2 + QK_ROPE
    o4 = o3 + XA_WIDTH
    w_cq, w_ckv, w_kr, w_xq, w_gate = (b_w_in[:, :o1], b_w_in[:, o1:o2], b_w_in[:, o2:o3],
                                       b_w_in[:, o3:o4], b_w_in[:, o4:])
    w_in = jnp.concatenate([w_gate, w_xq, w_cq, w_ckv, w_kr, _swap_halves(w_kr),
                            jnp.zeros((K, LANES), b_w_in.dtype)], axis=-1).astype(bf)
    wq = b_w_q_up.reshape(Q_LORA, MLA_HEADS, QK_NOPE + QK_ROPE)
    q_nope, q_rope = wq[..., :QK_NOPE], wq[..., QK_NOPE:]
    pad = jnp.zeros((Q_LORA, MLA_HEADS, LANES - QK_ROPE), wq.dtype)
    wa = jnp.concatenate([q_nope, q_rope, pad], axis=-1).reshape(Q_LORA, MLA_HEADS * QK_PAD).T.astype(bf)
    wb = jnp.concatenate([_swap_halves(q_rope), pad], axis=-1).reshape(Q_LORA, MLA_HEADS * LANES).T.astype(bf)
    wkv = b_w_kv_up.reshape(KV_LORA, MLA_HEADS, QK_NOPE + V_DIM)
    wk = wkv[..., :QK_NOPE].reshape(KV_LORA, MLA_HEADS * QK_NOPE).astype(bf)
    wvt = wkv[..., QK_NOPE:].reshape(KV_LORA, MLA_HEADS * V_DIM).T.astype(bf)
    return w_in, wa, wb, wk, wvt


def _prep_lru(gate_w, gate_b):
    wg = jnp.concatenate([gate_w[:, 0], gate_w[:, 1]], axis=-1).astype(jnp.bfloat16)
    gb = gate_b.reshape(2, 2, LRU_BLOCKS, 1, LRU_BLOCK)
    gb = 0.5 * jnp.concatenate([gb[:, 0], gb[:, 1]], axis=-1)
    return wg, gb


def _trunk(x, mem, w):
    batch, seq, _ = x.shape
    T = batch * seq
    x = x.reshape(T, D_MODEL)
    mem = mem.reshape(batch * MEM_LEN, D_MODEL)
    ct, st, tab = _rope_tables(seq)
    depth = w["norm_pre"].shape[0]
    for i in range(depth):
        kind, j = i % N_MIXERS, i // N_MIXERS
        kv = _norm_matmul(mem, w["norm_mem"][i], w["w_mem_kv"], i, tm=512, tn=1024)
        if kind == 0:
            u = _norm_matmul(x, w["norm_pre"][i], w["a_w_in"], j, tm=1024, tn=2048)
            mixes = _lru_mixer(u, w["a_conv_w"][j], w["a_conv_b"][j], w["a_wg"][j], w["a_gb"][j],
                               w["a_lambda"][j], batch=batch, seq=seq, rows=512)
            xq_off, gate_off = MIX_WIDTH, BRANCH
        elif kind == 1:
            u = _norm_matmul(x, w["norm_pre"][i], w["b_w_in"], j, tm=1024, tn=2048)
            q = _mla_q(u, w["b_q_norm"][j], w["b_wa"][j], w["b_wb"][j], ct, st,
                       seq=seq, tm=1024, heads=8)
            k, vt = _mla_kv(u, w["b_kv_norm"][j], w["b_wk"][j], w["b_wvt"][j], tab, seq=seq, tm=512)
            mixes = [_flash_attention(q, k, vt, batch=batch, seq=seq, tq=1024, tk=512, heads=1)]
            xq_off, gate_off = B_XQ_OFF, B_GATE_OFF
        else:
            u = _norm_matmul(x, w["norm_pre"][i], w["c_w_in"], j, tm=1024, tn=2048)
            mixes = [_pool_mixer(u, w["c_w_group"][j], w["c_scale"][j], batch=batch, seq=seq, tm=512)]
            xq_off, gate_off = MIX_WIDTH, BRANCH
        x = _out_block(mixes, u, xq_off, gate_off, kv, w["w_out"], i, x, w["norm_post"][i],
                       seq=seq, tm=256)
    return x.reshape(batch, seq, D_MODEL)


def kernel(x_prompt, x_sample, mem_prompt, mem_sample, norm_pre, norm_post, norm_mem, w_mem_kv, w_out,
           a_w_in, a_conv_w, a_conv_b, a_gate_w, a_gate_b, a_lambda,
           b_w_in, b_q_norm, b_kv_norm, b_w_q_up, b_w_kv_up,
           c_w_in, c_w_group, c_scale):
    bf = jnp.bfloat16
    b_parts = [_prep_b(b_w_in[j], b_w_q_up[j], b_w_kv_up[j]) for j in range(b_w_in.shape[0])]
    lru_parts = [_prep_lru(a_gate_w[j], a_gate_b[j]) for j in range(a_gate_w.shape[0])]
    w = dict(
        norm_pre=norm_pre, norm_post=norm_post, norm_mem=norm_mem,
        w_mem_kv=w_mem_kv.astype(bf), w_out=w_out.astype(bf),
        a_w_in=a_w_in.astype(bf), a_conv_w=a_conv_w, a_conv_b=a_conv_b,
        a_wg=[p[0] for p in lru_parts], a_gb=[p[1] for p in lru_parts], a_lambda=a_lambda,
        b_w_in=jnp.stack([p[0] for p in b_parts]), b_q_norm=b_q_norm, b_kv_norm=b_kv_norm,
        b_wa=[p[1] for p in b_parts], b_wb=[p[2] for p in b_parts],
        b_wk=[p[3] for p in b_parts], b_wvt=[p[4] for p in b_parts],
        c_w_in=c_w_in.astype(bf), c_w_group=[c_w_group[j].astype(bf) for j in range(c_w_group.shape[0])],
        c_scale=c_scale,
    )
    return (_trunk(x_prompt, mem_prompt, w), _trunk(x_sample, mem_sample, w))
```

```python
import functools
import math

import jax
import jax.numpy as jnp
from jax import lax
from jax.experimental import pallas as pl
from jax.experimental.pallas import tpu as pltpu

D_MODEL = 2048
MIX_WIDTH = 3 * D_MODEL // 2
XA_HEADS = 4
XA_HEAD_DIM = D_MODEL // 8
XA_WIDTH = XA_HEADS * XA_HEAD_DIM
BRANCH = MIX_WIDTH + XA_WIDTH
MEM_LEN = 256
N_MIXERS = 3
CONV_W = 4
CONV_LEFT = 2
LRU_BLOCKS = 12
LRU_BLOCK = MIX_WIDTH // LRU_BLOCKS
LRU_C = 8.0
QK_NOPE = 128
QK_ROPE = 64
V_DIM = 128
MLA_HEADS = MIX_WIDTH // V_DIM
Q_LORA = D_MODEL // 4
KV_LORA = D_MODEL // 8
ROPE_THETA = 10000.0
POOL_WINDOWS = (2, 4, 8, 16)
POOL_GROUP = MIX_WIDTH // len(POOL_WINDOWS)
EPS = 1e-6

LANES = 128
SUBLANES = 8
BF16_ROWS = 16
QK_PAD = 256
AUG_ROW = QK_NOPE + QK_ROPE
FAST_PATH_LIMIT = 64.0
VT_ROWS = V_DIM + BF16_ROWS
VMEM_LIMIT = 60 * 1024 * 1024

B_GATE_OFF = 0
B_XQ_OFF = BRANCH
B_CQ_OFF = B_XQ_OFF + XA_WIDTH
B_CKV_OFF = B_CQ_OFF + Q_LORA
B_KR_OFF = B_CKV_OFF + KV_LORA
B_IN_WIDTH = B_KR_OFF + 2 * QK_ROPE + LANES

LRU_CH = 1024
LRU_LANE_GROUPS = LRU_CH // LANES
LRU_HALO = BF16_ROWS
LRU_STEP = 4
RSQRT_FLOOR = 1e-30

POOL_HALO = BF16_ROWS


def _params(*sem, flags=None):
    return pltpu.CompilerParams(dimension_semantics=sem, vmem_limit_bytes=VMEM_LIMIT, flags=flags)


def _resident(block_shape, index_map):
    return pl.BlockSpec(block_shape, index_map, pipeline_mode=pl.Buffered(1))


def _norm_matmul_kernel(x_ref, g_ref, w_ref, o_ref, h_ref):
    @pl.when(pl.program_id(1) == 0)
    def _():
        x = x_ref[...].astype(jnp.float32)
        ms = jnp.mean(x * x, axis=-1, keepdims=True)
        h_ref[...] = ((x * lax.rsqrt(ms + EPS)) * g_ref[...]).astype(h_ref.dtype)

    o_ref[...] = jnp.dot(h_ref[...], w_ref[...],
                         preferred_element_type=jnp.float32).astype(o_ref.dtype)


def _norm_matmul(x, g, w, layer, *, tm, tn):
    T, K = x.shape
    N = w.shape[2]
    tm = min(tm, T)
    return pl.pallas_call(
        _norm_matmul_kernel,
        grid=(T // tm, N // tn),
        in_specs=[
            pl.BlockSpec((tm, K), lambda i, j: (i, 0)),
            pl.BlockSpec((1, K), lambda i, j: (0, 0)),
            pl.BlockSpec((None, K, tn), lambda i, j: (layer, 0, j)),
        ],
        out_specs=pl.BlockSpec((tm, tn), lambda i, j: (i, j)),
        out_shape=jax.ShapeDtypeStruct((T, N), jnp.bfloat16),
        scratch_shapes=[pltpu.VMEM((tm, K), jnp.bfloat16)],
        compiler_params=_params("parallel", "arbitrary"),
        name="norm_matmul",
    )(x, g.reshape(1, K), w)


def _out_kernel(*refs, n_mix):
    mix_refs = refs[:n_mix]
    xq_ref, gate_ref, kv_ref, w_ref, x_ref, g_ref, o_ref = refs[n_mix:]

    mix = mix_refs[0][...]
    for r in mix_refs[1:]:
        mix = mix + r[...]

    xq = xq_ref[...]
    scale = XA_HEAD_DIM ** -0.5 * math.log2(math.e)
    xa = []
    for h in range(XA_HEADS):
        lo = h * XA_HEAD_DIM
        q = xq[:, lo:lo + XA_HEAD_DIM]
        k = kv_ref[:, lo:lo + XA_HEAD_DIM]
        v = kv_ref[:, XA_WIDTH + lo:XA_WIDTH + lo + XA_HEAD_DIM]
        s = lax.dot_general(q, k, (((1,), (1,)), ((), ())), preferred_element_type=jnp.float32)
        m = jnp.max(s, axis=-1, keepdims=True)
        p = jnp.exp2((s - m) * scale)
        l = jnp.sum(p, axis=-1, keepdims=True)
        p = (p / l).astype(jnp.bfloat16)
        xa.append(jnp.dot(p, v, preferred_element_type=jnp.float32))

    half_gate = 0.5 * gate_ref[...].astype(jnp.float32)
    act = (half_gate * jnp.tanh(half_gate) + half_gate).astype(jnp.bfloat16)
    acc = jnp.dot(mix * act[:, :MIX_WIDTH], w_ref[:MIX_WIDTH, :], preferred_element_type=jnp.float32)
    xa = jnp.concatenate([a.astype(jnp.bfloat16) for a in xa], axis=-1)
    acc = acc + jnp.dot(xa * act[:, MIX_WIDTH:], w_ref[MIX_WIDTH:, :],
                        preferred_element_type=jnp.float32)

    ms = jnp.mean(acc * acc, axis=-1, keepdims=True)
    o_ref[...] = x_ref[...] + (acc * lax.rsqrt(ms + EPS)) * g_ref[...]


def _out_block(mixes, u, xq_off, gate_off, kv, w_out, layer, x, g_post, *, seq, tm):
    T = x.shape[0]
    tm = min(tm, seq)
    per_seq = seq // tm
    xq_blk = xq_off // XA_WIDTH
    gate_blk = gate_off // BRANCH
    mix_spec = pl.BlockSpec((tm, MIX_WIDTH), lambda i: (i, 0))
    return pl.pallas_call(
        functools.partial(_out_kernel, n_mix=len(mixes)),
        grid=(T // tm,),
        in_specs=[mix_spec] * len(mixes) + [
            pl.BlockSpec((tm, XA_WIDTH), lambda i: (i, xq_blk)),
            pl.BlockSpec((tm, BRANCH), lambda i: (i, gate_blk)),
            pl.BlockSpec((MEM_LEN, 2 * XA_WIDTH), lambda i: (i // per_seq, 0)),
            _resident((None, BRANCH, D_MODEL), lambda i: (layer, 0, 0)),
            pl.BlockSpec((tm, D_MODEL), lambda i: (i, 0)),
            pl.BlockSpec((1, D_MODEL), lambda i: (0, 0)),
        ],
        out_specs=pl.BlockSpec((tm, D_MODEL), lambda i: (i, 0)),
        out_shape=jax.ShapeDtypeStruct((T, D_MODEL), jnp.float32),
        compiler_params=_params("parallel"),
        name="xattn_gate_outproj",
    )(*mixes, u, u, kv, w_out, x, g_post.reshape(1, D_MODEL))


def _lru_kernel(cur_f, prev_f, next_f, cur_r, prev_r, next_r, cw_ref, cb_ref, wg_ref,
                gb_ref, lam_ref, hf_ref, hr_ref, w_f, w_r, a_f, b_f, h_f, a_r, b_r, h_r, carry,
                *, rows, pitch):
    s = pl.program_id(2)
    n = pl.num_programs(2)

    @pl.when(s == 0)
    def _():
        carry[...] = jnp.zeros_like(carry)

    win = rows + 2 * LRU_HALO

    def gates(cur, prev, nxt, chunk, d, w_s, a_s, b_s):
        p = prev[...].astype(jnp.float32) * (chunk > 0).astype(jnp.float32)
        q = nxt[...].astype(jnp.float32) * (chunk < n - 1).astype(jnp.float32)
        w = jnp.concatenate([p, cur[...].astype(jnp.float32), q], axis=0)
        half_cw = 0.5 * cw_ref[...]
        half_cb = 0.5 * cb_ref[...]
        half_u = []
        for grp in range(LRU_LANE_GROUPS):
            lanes = slice(grp * LANES, (grp + 1) * LANES)
            w_s[pl.ds(grp * 2 * win, win, stride=2), :] = w[:, lanes]
            acc = half_cb[:, lanes]
            for k in range(CONV_W):
                first = grp * 2 * win + 2 * (LRU_HALO - CONV_LEFT + k)
                acc = acc + w_s[pl.ds(first, rows, stride=2), :] * half_cw[k:k + 1, lanes]
            half_u.append(acc)
        half_u = jnp.concatenate(half_u, axis=-1)
        lam = lam_ref[d]
        softplus = jnp.maximum(-lam, 0.0) + jnp.log(1.0 + jnp.exp(-jnp.abs(lam)))
        half_decay = (-0.5 * LRU_C * math.log2(math.e)) * softplus
        for blk in range(LRU_CH // LRU_BLOCK):
            lo = blk * LRU_BLOCK
            hu = half_u[:, lo:lo + LRU_BLOCK]
            g = jnp.dot(hu.astype(jnp.bfloat16), wg_ref[d, blk],
                        preferred_element_type=jnp.float32) + gb_ref[d, blk]
            hd = half_decay[:, lo:lo + LRU_BLOCK]
            a = jnp.exp2(hd * jnp.tanh(g[:, :LRU_BLOCK]) + hd)
            iu = hu * jnp.tanh(g[:, LRU_BLOCK:]) + hu
            y = 1.0 - a * a
            b = (y * lax.rsqrt(jnp.maximum(y, RSQRT_FLOOR))) * iu
            for half in range(LRU_BLOCK // LANES):
                slab_rows = pl.ds((blk * (LRU_BLOCK // LANES) + half) * pitch, rows, stride=2)
                a_s[slab_rows, :] = a[:, half * LANES:(half + 1) * LANES]
                b_s[slab_rows, :] = b[:, half * LANES:(half + 1) * LANES]

    gates(cur_f, prev_f, next_f, s, 0, w_f, a_f, b_f)
    gates(cur_r, prev_r, next_r, n - 1 - s, 1, w_r, a_r, b_r)

    def block(a_s, b_s, h_s, h, t0, sign):
        idx = [pl.ds(2 * (t0 + sign * j), LRU_LANE_GROUPS, stride=pitch) for j in range(LRU_STEP)]
        a = [a_s[i, :] for i in idx]
        b = [b_s[i, :] for i in idx]
        a01, b01 = a[1] * a[0], a[1] * b[0] + b[1]
        a23, b23 = a[3] * a[2], a[3] * b[2] + b[3]
        a03, b03 = a23 * a01, a23 * b01 + b23
        h0 = a[0] * h + b[0]
        h1 = a01 * h + b01
        h3 = a03 * h + b03
        h2 = a[2] * h1 + b[2]
        for i, v in zip(idx, (h0, h1, h2, h3)):
            h_s[i, :] = v
        return h3

    blocks_per_iter = 8

    def body(it, hs):
        hf, hr = hs
        for j in range(blocks_per_iter):
            t = (it * blocks_per_iter + j) * LRU_STEP
            hf = block(a_f, b_f, h_f, hf, t, 1)
            hr = block(a_r, b_r, h_r, hr, rows - 1 - t, -1)
        return hf, hr

    hf, hr = lax.fori_loop(0, rows // (LRU_STEP * blocks_per_iter), body, (carry[0], carry[1]))
    carry[0] = hf
    carry[1] = hr

    for grp in range(LRU_LANE_GROUPS):
        slab_rows = pl.ds(grp * pitch, rows, stride=2)
        hf_ref[:, grp * LANES:(grp + 1) * LANES] = h_f[slab_rows, :].astype(hf_ref.dtype)
        hr_ref[:, grp * LANES:(grp + 1) * LANES] = h_r[slab_rows, :].astype(hr_ref.dtype)


def _lru_mixer(u, conv_w, conv_b, wg, gb, lam, *, batch, seq, rows):
    T = u.shape[0]
    rows = min(rows, seq)
    n = seq // rows
    pitch = 2 * rows + 4
    hb = rows // LRU_HALO
    last_halo = T // LRU_HALO - 1

    def cur_map(rev):
        def f(b, c, s):
            chunk = (n - 1 - s) if rev else s
            return (b * n + chunk, c)
        return f

    def prev_map(rev):
        def f(b, c, s):
            chunk = (n - 1 - s) if rev else s
            return (jnp.maximum((b * n + chunk) * hb - 1, 0), c)
        return f

    def next_map(rev):
        def f(b, c, s):
            chunk = (n - 1 - s) if rev else s
            return (jnp.minimum((b * n + chunk + 1) * hb, last_halo), c)
        return f

    blocks = LRU_CH // LRU_BLOCK
    in_specs = []
    for rev in (False, True):
        in_specs += [
            pl.BlockSpec((rows, LRU_CH), cur_map(rev)),
            pl.BlockSpec((LRU_HALO, LRU_CH), prev_map(rev)),
            pl.BlockSpec((LRU_HALO, LRU_CH), next_map(rev)),
        ]
    in_specs += [
        pl.BlockSpec((CONV_W, LRU_CH), lambda b, c, s: (0, c)),
        pl.BlockSpec((1, LRU_CH), lambda b, c, s: (0, c)),
        pl.BlockSpec((2, blocks, LRU_BLOCK, 2 * LRU_BLOCK), lambda b, c, s: (0, c, 0, 0)),
        pl.BlockSpec((2, blocks, 1, 2 * LRU_BLOCK), lambda b, c, s: (0, c, 0, 0)),
        pl.BlockSpec((2, 1, LRU_CH), lambda b, c, s: (0, 0, c)),
    ]
    slab = pltpu.VMEM((LRU_LANE_GROUPS * pitch, LANES), jnp.float32)
    window = pltpu.VMEM((LRU_LANE_GROUPS * 2 * (rows + 2 * LRU_HALO), LANES), jnp.float32)
    out = jax.ShapeDtypeStruct((T, MIX_WIDTH), jnp.bfloat16)
    return pl.pallas_call(
        functools.partial(_lru_kernel, rows=rows, pitch=pitch),
        grid=(batch, MIX_WIDTH // LRU_CH, n),
        in_specs=in_specs,
        out_specs=[pl.BlockSpec((rows, LRU_CH), cur_map(False)),
                   pl.BlockSpec((rows, LRU_CH), cur_map(True))],
        out_shape=[out, out],
        scratch_shapes=[window, window, slab, slab, slab, slab, slab, slab,
                        pltpu.VMEM((2, LRU_LANE_GROUPS, LANES), jnp.float32)],
        compiler_params=_params("parallel", "parallel", "arbitrary"),
        name="rglru",
    )(u, u, u, u, u, u, conv_w, conv_b.reshape(1, MIX_WIDTH), wg, gb,
      lam.reshape(2, 1, MIX_WIDTH))


def _mla_q_kernel(cq_ref, g_ref, wa_ref, wb_ref, ct_ref, st_ref, o_ref, h_ref, *, heads):
    @pl.when(pl.program_id(1) == 0)
    def _():
        x = cq_ref[...].astype(jnp.float32)
        ms = jnp.mean(x * x, axis=-1, keepdims=True)
        h_ref[...] = ((x * lax.rsqrt(ms + EPS)) * g_ref[...]).astype(h_ref.dtype)

    h = h_ref[...]
    nt = (((1,), (1,)), ((), ()))
    qa = lax.dot_general(wa_ref[...], h, nt, preferred_element_type=jnp.float32)
    qb = lax.dot_general(wb_ref[...], h, nt, preferred_element_type=jnp.float32)
    scale = (QK_NOPE + QK_ROPE) ** -0.5 * math.log2(math.e)
    ct = ct_ref[...] * scale
    st = st_ref[...] * scale
    for hh in range(heads):
        lo = hh * QK_PAD
        o_ref[lo:lo + LANES, :] = (qa[lo:lo + LANES] * scale).astype(o_ref.dtype)
        rope = qa[lo + LANES:lo + 2 * LANES] * ct + qb[hh * LANES:(hh + 1) * LANES] * st
        o_ref[lo + LANES:lo + 2 * LANES, :] = rope.astype(o_ref.dtype)


def _mla_q(u, q_norm, wa_t, wb_t, ct_t, st_t, *, seq, tm, heads):
    T = u.shape[0]
    tm = min(tm, seq)
    per_seq = seq // tm
    cq_blk = B_CQ_OFF // Q_LORA
    return pl.pallas_call(
        functools.partial(_mla_q_kernel, heads=heads),
        grid=(T // tm, MLA_HEADS // heads),
        in_specs=[
            pl.BlockSpec((tm, Q_LORA), lambda i, j: (i, cq_blk)),
            pl.BlockSpec((1, Q_LORA), lambda i, j: (0, 0)),
            pl.BlockSpec((heads * QK_PAD, Q_LORA), lambda i, j: (j, 0)),
            pl.BlockSpec((heads * LANES, Q_LORA), lambda i, j: (j, 0)),
            pl.BlockSpec((LANES, tm), lambda i, j: (0, i % per_seq)),
            pl.BlockSpec((LANES, tm), lambda i, j: (0, i % per_seq)),
        ],
        out_specs=pl.BlockSpec((heads * QK_PAD, tm), lambda i, j: (j, i)),
        out_shape=jax.ShapeDtypeStruct((MLA_HEADS * QK_PAD, T), jnp.bfloat16),
        scratch_shapes=[pltpu.VMEM((tm, Q_LORA), jnp.bfloat16)],
        compiler_params=_params("parallel", "arbitrary"),
        name="mla_q_proj",
    )(u, q_norm.reshape(1, Q_LORA), wa_t, wb_t, ct_t, st_t)


def _mla_kv_kernel(ckv_ref, g_ref, wk_ref, wvt_ref, kr_ref, tab_ref, k_ref, vt_ref):
    x = ckv_ref[...].astype(jnp.float32)
    ms = jnp.mean(x * x, axis=-1, keepdims=True)
    h = ((x * lax.rsqrt(ms + EPS)) * g_ref[...]).astype(jnp.bfloat16)
    kn = jnp.dot(h, wk_ref[...], preferred_element_type=jnp.float32)
    vt = lax.dot_general(wvt_ref[...], h, (((1,), (1,)), ((), ())),
                         preferred_element_type=jnp.float32).astype(vt_ref.dtype)
    ones = jnp.ones((VT_ROWS - V_DIM, vt.shape[1]), vt_ref.dtype)
    for hh in range(MLA_HEADS):
        vt_ref[hh * VT_ROWS:hh * VT_ROWS + V_DIM, :] = vt[hh * V_DIM:(hh + 1) * V_DIM, :]
        vt_ref[hh * VT_ROWS + V_DIM:(hh + 1) * VT_ROWS, :] = ones
    t = kr_ref[...].astype(jnp.float32) * tab_ref[...]
    rot = t + pltpu.roll(t, QK_ROPE, 1)
    lane = lax.broadcasted_iota(jnp.int32, rot.shape, 1)
    rot = jnp.where(lane < QK_ROPE, rot, jnp.where(lane == QK_ROPE, 1.0, 0.0)).astype(k_ref.dtype)
    for hh in range(MLA_HEADS):
        k_ref[:, hh * QK_PAD:hh * QK_PAD + LANES] = kn[:, hh * LANES:(hh + 1) * LANES].astype(k_ref.dtype)
        k_ref[:, hh * QK_PAD + LANES:(hh + 1) * QK_PAD] = rot


def _mla_kv(u, kv_norm, wk, wvt, tab, *, seq, tm):
    T = u.shape[0]
    tm = min(tm, seq)
    per_seq = seq // tm
    ckv_blk = B_CKV_OFF // KV_LORA
    kr_blk = B_KR_OFF // LANES
    return pl.pallas_call(
        _mla_kv_kernel,
        grid=(T // tm,),
        in_specs=[
            pl.BlockSpec((tm, KV_LORA), lambda i: (i, ckv_blk)),
            pl.BlockSpec((1, KV_LORA), lambda i: (0, 0)),
            pl.BlockSpec((KV_LORA, MLA_HEADS * QK_NOPE), lambda i: (0, 0)),
            pl.BlockSpec((MLA_HEADS * V_DIM, KV_LORA), lambda i: (0, 0)),
            pl.BlockSpec((tm, LANES), lambda i: (i, kr_blk)),
            pl.BlockSpec((tm, LANES), lambda i: (i % per_seq, 0)),
        ],
        out_specs=[pl.BlockSpec((tm, MLA_HEADS * QK_PAD), lambda i: (i, 0)),
                   pl.BlockSpec((MLA_HEADS * VT_ROWS, tm), lambda i: (0, i))],
        out_shape=[jax.ShapeDtypeStruct((T, MLA_HEADS * QK_PAD), jnp.bfloat16),
                   jax.ShapeDtypeStruct((MLA_HEADS * VT_ROWS, T), jnp.bfloat16)],
        compiler_params=_params("parallel"),
        name="mla_kv_proj",
    )(u, kv_norm.reshape(1, KV_LORA), wk, wvt, u, tab)


def _flash_head(q_ref, k_ref, vt_ref, o_ref, qt_buf, acc_ref, *, tk, n_kv):
    bf = jnp.bfloat16
    qt = q_ref[...]
    qt_buf[0] = qt
    qt_buf[1] = qt
    refs = {}

    def set_reference(t, value):
        ref = value.astype(bf)
        qt_buf[t % 2, AUG_ROW:AUG_ROW + BF16_ROWS, :] = jnp.broadcast_to(-ref, (BF16_ROWS, ref.shape[1]))
        refs[t] = ref.astype(jnp.float32)

    first = jnp.dot(k_ref[:BF16_ROWS, :], qt, preferred_element_type=jnp.float32)
    set_reference(0, jnp.max(first, axis=0, keepdims=True))
    m_true = worst = None
    for j in range(n_kv):
        k = k_ref[j * tk:(j + 1) * tk, :]
        s = jnp.dot(k, qt_buf[j % 2], preferred_element_type=jnp.float32)
        p = jnp.exp2(s.astype(bf))
        pmax = jnp.max(p, axis=0, keepdims=True).astype(jnp.float32)
        seen = refs[j] + jnp.log2(pmax)
        m_true = seen if j == 0 else jnp.maximum(m_true, seen)
        worst = pmax if j == 0 else jnp.maximum(worst, pmax)
        for t in ([1, 2] if j == 0 else [j + 2]):
            if t < n_kv:
                set_reference(t, m_true)
        pv = jnp.dot(vt_ref[:, j * tk:(j + 1) * tk], p, preferred_element_type=jnp.float32)
        if j == 0:
            acc_ref[...] = pv
        else:
            acc_ref[...] = jnp.exp2(refs[j - 1] - refs[j]) * acc_ref[...] + pv

    def finish():
        out = acc_ref[:V_DIM, :] / acc_ref[V_DIM:V_DIM + 1, :]
        o_ref[...] = out.T.astype(o_ref.dtype)

    finish()

    def exact():
        m_run = None
        for j in range(n_kv):
            s = jnp.dot(k_ref[j * tk:(j + 1) * tk, :], qt, preferred_element_type=jnp.float32)
            cmax = jnp.max(s, axis=0, keepdims=True)
            m_new = cmax if j == 0 else jnp.maximum(m_run, cmax)
            p = jnp.exp2((s - m_new).astype(bf))
            pv = jnp.dot(vt_ref[:, j * tk:(j + 1) * tk], p, preferred_element_type=jnp.float32)
            if j == 0:
                acc_ref[...] = pv
            else:
                acc_ref[...] = jnp.exp2(m_run - m_new) * acc_ref[...] + pv
            m_run = m_new
        finish()

    return worst, exact


def _flash_kernel(q_ref, k_ref, vt_ref, o_ref, qt_buf, acc_ref, *, tk, n_kv, heads):
    pending = []
    for h in range(heads):
        pending.append(_flash_head(
            q_ref.at[h * QK_PAD:(h + 1) * QK_PAD, :], k_ref.at[:, h * QK_PAD:(h + 1) * QK_PAD],
            vt_ref.at[h * VT_ROWS:(h + 1) * VT_ROWS, :], o_ref.at[:, h * V_DIM:(h + 1) * V_DIM],
            qt_buf.at[h], acc_ref.at[h], tk=tk, n_kv=n_kv))
    for worst, exact in pending:
        pl.when(jnp.logical_not(jnp.max(worst) <= 2.0 ** FAST_PATH_LIMIT))(exact)


def _flash_attention(qt, k, vt, *, batch, seq, tq, tk, heads):
    T = k.shape[0]
    tq = min(tq, seq)
    tk = min(tk, seq)
    nq = seq // tq
    return pl.pallas_call(
        functools.partial(_flash_kernel, tk=tk, n_kv=seq // tk, heads=heads),
        grid=(batch, MLA_HEADS // heads, nq),
        in_specs=[
            pl.BlockSpec((heads * QK_PAD, tq), lambda b, h, i: (h, b * nq + i)),
            pl.BlockSpec((seq, heads * QK_PAD), lambda b, h, i: (b, h)),
            pl.BlockSpec((heads * VT_ROWS, seq), lambda b, h, i: (h, b)),
        ],
        out_specs=pl.BlockSpec((tq, heads * V_DIM), lambda b, h, i: (b * nq + i, h)),
        out_shape=jax.ShapeDtypeStruct((T, MLA_HEADS * V_DIM), jnp.bfloat16),
        scratch_shapes=[pltpu.VMEM((heads, 2, QK_PAD, tq), jnp.bfloat16),
                        pltpu.VMEM((heads, VT_ROWS, tq), jnp.float32)],
        compiler_params=_params("parallel", "parallel", "arbitrary"),
        name="mla_flash_attention",
    )(qt, k, vt)


def _pool_kernel(cur_ref, prev_ref, next_ref, w_ref, sc_ref, o_ref, *, seq, tm):
    i = pl.program_id(1)
    n = pl.num_programs(1)
    win = tm + 2 * POOL_HALO
    keep_prev = (i > 0).astype(jnp.float32)
    keep_next = (i < n - 1).astype(jnp.float32)
    t = i * tm + lax.broadcasted_iota(jnp.int32, (tm, 1), 0)
    for g, w in enumerate(POOL_WINDOWS):
        lo = g * POOL_GROUP
        x = cur_ref[:, lo:lo + POOL_GROUP].astype(jnp.float32)
        p = prev_ref[:, lo:lo + POOL_GROUP].astype(jnp.float32) * keep_prev
        q = next_ref[:, lo:lo + POOL_GROUP].astype(jnp.float32) * keep_next
        run = jnp.concatenate([p, x, q], axis=0)
        span = 1
        while span < w:
            run = run + pltpu.roll(run, span, 0)
            span *= 2
        ahead = w - w // 2 - 1
        if ahead:
            run = pltpu.roll(run, win - ahead, 0)
        total = run[POOL_HALO:POOL_HALO + tm]
        count = jnp.minimum(t + (w - w // 2), seq) - jnp.maximum(t - w // 2, 0)
        pooled = total / count.astype(jnp.float32) - x
        y = jnp.dot(pooled.astype(jnp.bfloat16), w_ref[g], preferred_element_type=jnp.float32)
        o_ref[:, lo:lo + POOL_GROUP] = (y * sc_ref[:, lo:lo + POOL_GROUP]).astype(o_ref.dtype)


def _pool_mixer(u, w_group, scale, *, batch, seq, tm):
    T = u.shape[0]
    tm = min(tm, seq)
    n = seq // tm
    hb = tm // POOL_HALO
    last_halo = T // POOL_HALO - 1
    return pl.pallas_call(
        functools.partial(_pool_kernel, seq=seq, tm=tm),
        grid=(batch, n),
        in_specs=[
            pl.BlockSpec((tm, MIX_WIDTH), lambda b, i: (b * n + i, 0)),
            pl.BlockSpec((POOL_HALO, MIX_WIDTH),
                         lambda b, i: (jnp.maximum((b * n + i) * hb - 1, 0), 0)),
            pl.BlockSpec((POOL_HALO, MIX_WIDTH),
                         lambda b, i: (jnp.minimum((b * n + i + 1) * hb, last_halo), 0)),
            pl.BlockSpec((len(POOL_WINDOWS), POOL_GROUP, POOL_GROUP), lambda b, i: (0, 0, 0)),
            pl.BlockSpec((1, MIX_WIDTH), lambda b, i: (0, 0)),
        ],
        out_specs=pl.BlockSpec((tm, MIX_WIDTH), lambda b, i: (b * n + i, 0)),
        out_shape=jax.ShapeDtypeStruct((T, MIX_WIDTH), jnp.bfloat16),
        compiler_params=_params("parallel", "parallel"),
        name="pool_mixer",
    )(u, u, u, w_group, scale.reshape(1, MIX_WIDTH))


def _rope_tables(seq):
    inv_freq = 1.0 / (ROPE_THETA ** (jnp.arange(0, QK_ROPE, 2, dtype=jnp.float32) / QK_ROPE))
    ang = jnp.arange(seq, dtype=jnp.float32)[:, None] * inv_freq[None, :]
    cos, sin = jnp.cos(ang), jnp.sin(ang)
    zeros = jnp.zeros((seq, LANES - QK_ROPE), jnp.float32)
    ct = jnp.concatenate([cos, cos, zeros], axis=-1).T
    st = jnp.concatenate([-sin, sin, zeros], axis=-1).T
    tab = jnp.concatenate([cos, cos, -sin, sin], axis=-1)
    return ct, st, tab


def _swap_halves(w):
    half = QK_ROPE // 2
    return jnp.concatenate([w[..., half:], w[..., :half]], axis=-1)


def _prep_b(b_w_in, b_w_q_up, b_w_kv_up):
    bf = jnp.bfloat16
    K = b_w_in.shape[0]
    o1 = Q_LORA
    o2 = o1 + KV_LORA
    o3 = o2 + QK_ROPE
    o4 = o3 + XA_WIDTH
    w_cq, w_ckv, w_kr, w_xq, w_gate = (b_w_in[:, :o1], b_w_in[:, o1:o2], b_w_in[:, o2:o3],
                                       b_w_in[:, o3:o4], b_w_in[:, o4:])
    w_in = jnp.concatenate([w_gate, w_xq, w_cq, w_ckv, w_kr, _swap_halves(w_kr),
                            jnp.zeros((K, LANES), b_w_in.dtype)], axis=-1).astype(bf)
    wq = b_w_q_up.reshape(Q_LORA, MLA_HEADS, QK_NOPE + QK_ROPE)
    q_nope, q_rope = wq[..., :QK_NOPE], wq[..., QK_NOPE:]
    pad = jnp.zeros((Q_LORA, MLA_HEADS, LANES - QK_ROPE), wq.dtype)
    wa = jnp.concatenate([q_nope, q_rope, pad], axis=-1).reshape(Q_LORA, MLA_HEADS * QK_PAD).T.astype(bf)
    wb = jnp.concatenate([_swap_halves(q_rope), pad], axis=-1).reshape(Q_LORA, MLA_HEADS * LANES).T.astype(bf)
    wkv = b_w_kv_up.reshape(KV_LORA, MLA_HEADS, QK_NOPE + V_DIM)
    wk = wkv[..., :QK_NOPE].reshape(KV_LORA, MLA_HEADS * QK_NOPE).astype(bf)
    wvt = wkv[..., QK_NOPE:].reshape(KV_LORA, MLA_HEADS * V_DIM).T.astype(bf)
    return w_in, wa, wb, wk, wvt


def _prep_lru(gate_w, gate_b):
    wg = jnp.concatenate([gate_w[:, 0], gate_w[:, 1]], axis=-1).astype(jnp.bfloat16)
    gb = gate_b.reshape(2, 2, LRU_BLOCKS, 1, LRU_BLOCK)
    gb = 0.5 * jnp.concatenate([gb[:, 0], gb[:, 1]], axis=-1)
    return wg, gb


def _trunk(x, mem, w):
    batch, seq, _ = x.shape
    T = batch * seq
    x = x.reshape(T, D_MODEL)
    mem = mem.reshape(batch * MEM_LEN, D_MODEL)
    ct, st, tab = _rope_tables(seq)
    depth = w["norm_pre"].shape[0]
    for i in range(depth):
        kind, j = i % N_MIXERS, i // N_MIXERS
        kv = _norm_matmul(mem, w["norm_mem"][i], w["w_mem_kv"], i, tm=512, tn=1024)
        if kind == 0:
            u = _norm_matmul(x, w["norm_pre"][i], w["a_w_in"], j, tm=1024, tn=2048)
            mixes = _lru_mixer(u, w["a_conv_w"][j], w["a_conv_b"][j], w["a_wg"][j], w["a_gb"][j],
                               w["a_lambda"][j], batch=batch, seq=seq, rows=512)
            xq_off, gate_off = MIX_WIDTH, BRANCH
        elif kind == 1:
            u = _norm_matmul(x, w["norm_pre"][i], w["b_w_in"], j, tm=1024, tn=2048)
            q = _mla_q(u, w["b_q_norm"][j], w["b_wa"][j], w["b_wb"][j], ct, st,
                       seq=seq, tm=1024, heads=8)
            k, vt = _mla_kv(u, w["b_kv_norm"][j], w["b_wk"][j], w["b_wvt"][j], tab, seq=seq, tm=512)
            mixes = [_flash_attention(q, k, vt, batch=batch, seq=seq, tq=1024, tk=512, heads=1)]
            xq_off, gate_off = B_XQ_OFF, B_GATE_OFF
        else:
            u = _norm_matmul(x, w["norm_pre"][i], w["c_w_in"], j, tm=1024, tn=2048)
            mixes = [_pool_mixer(u, w["c_w_group"][j], w["c_scale"][j], batch=batch, seq=seq, tm=512)]
            xq_off, gate_off = MIX_WIDTH, BRANCH
        x = _out_block(mixes, u, xq_off, gate_off, kv, w["w_out"], i, x, w["norm_post"][i],
                       seq=seq, tm=512 if len(mixes) == 1 else 256)
    return x.reshape(batch, seq, D_MODEL)


def kernel(x_prompt, x_sample, mem_prompt, mem_sample, norm_pre, norm_post, norm_mem, w_mem_kv, w_out,
           a_w_in, a_conv_w, a_conv_b, a_gate_w, a_gate_b, a_lambda,
           b_w_in, b_q_norm, b_kv_norm, b_w_q_up, b_w_kv_up,
           c_w_in, c_w_group, c_scale):
    bf = jnp.bfloat16
    b_parts = [_prep_b(b_w_in[j], b_w_q_up[j], b_w_kv_up[j]) for j in range(b_w_in.shape[0])]
    lru_parts = [_prep_lru(a_gate_w[j], a_gate_b[j]) for j in range(a_gate_w.shape[0])]
    w = dict(
        norm_pre=norm_pre, norm_post=norm_post, norm_mem=norm_mem,
        w_mem_kv=w_mem_kv.astype(bf), w_out=w_out.astype(bf),
        a_w_in=a_w_in.astype(bf), a_conv_w=a_conv_w, a_conv_b=a_conv_b,
        a_wg=[p[0] for p in lru_parts], a_gb=[p[1] for p in lru_parts], a_lambda=a_lambda,
        b_w_in=jnp.stack([p[0] for p in b_parts]), b_q_norm=b_q_norm, b_kv_norm=b_kv_norm,
        b_wa=[p[1] for p in b_parts], b_wb=[p[2] for p in b_parts],
        b_wk=[p[3] for p in b_parts], b_wvt=[p[4] for p in b_parts],
        c_w_in=c_w_in.astype(bf), c_w_group=[c_w_group[j].astype(bf) for j in range(c_w_group.shape[0])],
        c_scale=c_scale,
    )
    return (_trunk(x_prompt, mem_prompt, w), _trunk(x_sample, mem_sample, w))
```

```python
import functools
import math

import jax
import jax.numpy as jnp
from jax import lax
from jax.experimental import pallas as pl
from jax.experimental.pallas import tpu as pltpu

D_MODEL = 2048
MIX_WIDTH = 3 * D_MODEL // 2
XA_HEADS = 4
XA_HEAD_DIM = D_MODEL // 8
XA_WIDTH = XA_HEADS * XA_HEAD_DIM
BRANCH = MIX_WIDTH + XA_WIDTH
MEM_LEN = 256
N_MIXERS = 3
CONV_W = 4
CONV_LEFT = 2
LRU_BLOCKS = 12
LRU_BLOCK = MIX_WIDTH // LRU_BLOCKS
LRU_C = 8.0
QK_NOPE = 128
QK_ROPE = 64
V_DIM = 128
MLA_HEADS = MIX_WIDTH // V_DIM
Q_LORA = D_MODEL // 4
KV_LORA = D_MODEL // 8
ROPE_THETA = 10000.0
POOL_WINDOWS = (2, 4, 8, 16)
POOL_GROUP = MIX_WIDTH // len(POOL_WINDOWS)
EPS = 1e-6

LANES = 128
SUBLANES = 8
BF16_ROWS = 16
QK_PAD = 256
AUG_ROW = QK_NOPE + QK_ROPE
FAST_PATH_LIMIT = 64.0
VT_ROWS = V_DIM + BF16_ROWS
VMEM_LIMIT = 63 * 1024 * 1024

B_GATE_OFF = 0
B_XQ_OFF = BRANCH
B_CQ_OFF = B_XQ_OFF + XA_WIDTH
B_CKV_OFF = B_CQ_OFF + Q_LORA
B_KR_OFF = B_CKV_OFF + KV_LORA
B_IN_WIDTH = B_KR_OFF + 2 * QK_ROPE + LANES

LRU_CH = 1024
LRU_LANE_GROUPS = LRU_CH // LANES
LRU_HALO = BF16_ROWS
LRU_STEP = 4
RSQRT_FLOOR = 1e-30

POOL_HALO = BF16_ROWS


def _params(*sem, flags=None):
    return pltpu.CompilerParams(dimension_semantics=sem, vmem_limit_bytes=VMEM_LIMIT, flags=flags)


def _resident(block_shape, index_map):
    return pl.BlockSpec(block_shape, index_map, pipeline_mode=pl.Buffered(1))


def _norm_matmul_kernel(x_ref, g_ref, w_ref, o_ref, h_ref):
    @pl.when(pl.program_id(1) == 0)
    def _():
        x = x_ref[...].astype(jnp.float32)
        ms = jnp.mean(x * x, axis=-1, keepdims=True)
        h_ref[...] = ((x * lax.rsqrt(ms + EPS)) * g_ref[...]).astype(h_ref.dtype)

    o_ref[...] = jnp.dot(h_ref[...], w_ref[...],
                         preferred_element_type=jnp.float32).astype(o_ref.dtype)


def _norm_matmul(x, g, w, layer, *, tm, tn):
    T, K = x.shape
    N = w.shape[2]
    tm = min(tm, T)
    return pl.pallas_call(
        _norm_matmul_kernel,
        grid=(T // tm, N // tn),
        in_specs=[
            pl.BlockSpec((tm, K), lambda i, j: (i, 0)),
            pl.BlockSpec((1, K), lambda i, j: (0, 0)),
            pl.BlockSpec((None, K, tn), lambda i, j: (layer, 0, j)),
        ],
        out_specs=pl.BlockSpec((tm, tn), lambda i, j: (i, j)),
        out_shape=jax.ShapeDtypeStruct((T, N), jnp.bfloat16),
        scratch_shapes=[pltpu.VMEM((tm, K), jnp.bfloat16)],
        compiler_params=_params("parallel", "arbitrary"),
        name="norm_matmul",
    )(x, g.reshape(1, K), w)


def _out_kernel(*refs, n_mix):
    mix_refs = refs[:n_mix]
    xq_ref, gate_ref, kv_ref, w_ref, x_ref, g_ref, o_ref = refs[n_mix:]

    mix = mix_refs[0][...]
    for r in mix_refs[1:]:
        mix = mix + r[...]

    xq = xq_ref[...]
    scale = XA_HEAD_DIM ** -0.5 * math.log2(math.e)
    xa = []
    for h in range(XA_HEADS):
        lo = h * XA_HEAD_DIM
        q = xq[:, lo:lo + XA_HEAD_DIM]
        k = kv_ref[:, lo:lo + XA_HEAD_DIM]
        v = kv_ref[:, XA_WIDTH + lo:XA_WIDTH + lo + XA_HEAD_DIM]
        s = lax.dot_general(q, k, (((1,), (1,)), ((), ())), preferred_element_type=jnp.float32)
        m = jnp.max(s, axis=-1, keepdims=True)
        p = jnp.exp2((s - m) * scale)
        l = jnp.sum(p, axis=-1, keepdims=True)
        p = (p / l).astype(jnp.bfloat16)
        xa.append(jnp.dot(p, v, preferred_element_type=jnp.float32))

    half_gate = 0.5 * gate_ref[...].astype(jnp.float32)
    act = (half_gate * jnp.tanh(half_gate) + half_gate).astype(jnp.bfloat16)
    acc = jnp.dot(mix * act[:, :MIX_WIDTH], w_ref[:MIX_WIDTH, :], preferred_element_type=jnp.float32)
    xa = jnp.concatenate([a.astype(jnp.bfloat16) for a in xa], axis=-1)
    acc = acc + jnp.dot(xa * act[:, MIX_WIDTH:], w_ref[MIX_WIDTH:, :],
                        preferred_element_type=jnp.float32)

    ms = jnp.mean(acc * acc, axis=-1, keepdims=True)
    o_ref[...] = x_ref[...] + (acc * lax.rsqrt(ms + EPS)) * g_ref[...]


def _out_block(mixes, u, xq_off, gate_off, kv, w_out, layer, x, g_post, *, seq, tm):
    T = x.shape[0]
    tm = min(tm, seq)
    per_seq = seq // tm
    xq_blk = xq_off // XA_WIDTH
    gate_blk = gate_off // BRANCH
    mix_spec = pl.BlockSpec((tm, MIX_WIDTH), lambda i: (i, 0))
    return pl.pallas_call(
        functools.partial(_out_kernel, n_mix=len(mixes)),
        grid=(T // tm,),
        in_specs=[mix_spec] * len(mixes) + [
            pl.BlockSpec((tm, XA_WIDTH), lambda i: (i, xq_blk)),
            pl.BlockSpec((tm, BRANCH), lambda i: (i, gate_blk)),
            pl.BlockSpec((MEM_LEN, 2 * XA_WIDTH), lambda i: (i // per_seq, 0)),
            _resident((None, BRANCH, D_MODEL), lambda i: (layer, 0, 0)),
            pl.BlockSpec((tm, D_MODEL), lambda i: (i, 0)),
            pl.BlockSpec((1, D_MODEL), lambda i: (0, 0)),
        ],
        out_specs=pl.BlockSpec((tm, D_MODEL), lambda i: (i, 0)),
        out_shape=jax.ShapeDtypeStruct((T, D_MODEL), jnp.float32),
        compiler_params=_params("parallel"),
        name="xattn_gate_outproj",
    )(*mixes, u, u, kv, w_out, x, g_post.reshape(1, D_MODEL))


def _lru_kernel(cur_f, prev_f, next_f, cur_r, prev_r, next_r, cw_ref, cb_ref, wg_ref,
                gb_ref, lam_ref, hf_ref, hr_ref, w_f, w_r, a_f, b_f, h_f, a_r, b_r, h_r, carry,
                *, rows, pitch):
    s = pl.program_id(2)
    n = pl.num_programs(2)

    @pl.when(s == 0)
    def _():
        carry[...] = jnp.zeros_like(carry)

    win = rows + 2 * LRU_HALO

    def gates(cur, prev, nxt, chunk, d, w_s, a_s, b_s):
        p = prev[...].astype(jnp.float32) * (chunk > 0).astype(jnp.float32)
        q = nxt[...].astype(jnp.float32) * (chunk < n - 1).astype(jnp.float32)
        w = jnp.concatenate([p, cur[...].astype(jnp.float32), q], axis=0)
        half_cw = 0.5 * cw_ref[...]
        half_cb = 0.5 * cb_ref[...]
        half_u = []
        for grp in range(LRU_LANE_GROUPS):
            lanes = slice(grp * LANES, (grp + 1) * LANES)
            w_s[pl.ds(grp * 2 * win, win, stride=2), :] = w[:, lanes]
            acc = half_cb[:, lanes]
            for k in range(CONV_W):
                first = grp * 2 * win + 2 * (LRU_HALO - CONV_LEFT + k)
                acc = acc + w_s[pl.ds(first, rows, stride=2), :] * half_cw[k:k + 1, lanes]
            half_u.append(acc)
        half_u = jnp.concatenate(half_u, axis=-1)
        lam = lam_ref[d]
        softplus = jnp.maximum(-lam, 0.0) + jnp.log(1.0 + jnp.exp(-jnp.abs(lam)))
        half_decay = (-0.5 * LRU_C * math.log2(math.e)) * softplus
        for blk in range(LRU_CH // LRU_BLOCK):
            lo = blk * LRU_BLOCK
            hu = half_u[:, lo:lo + LRU_BLOCK]
            g = jnp.dot(hu.astype(jnp.bfloat16), wg_ref[d, blk],
                        preferred_element_type=jnp.float32) + gb_ref[d, blk]
            hd = half_decay[:, lo:lo + LRU_BLOCK]
            a = jnp.exp2(hd * jnp.tanh(g[:, :LRU_BLOCK]) + hd)
            iu = hu * jnp.tanh(g[:, LRU_BLOCK:]) + hu
            y = 1.0 - a * a
            b = (y * lax.rsqrt(jnp.maximum(y, RSQRT_FLOOR))) * iu
            for half in range(LRU_BLOCK // LANES):
                slab_rows = pl.ds((blk * (LRU_BLOCK // LANES) + half) * pitch, rows, stride=2)
                a_s[slab_rows, :] = a[:, half * LANES:(half + 1) * LANES]
                b_s[slab_rows, :] = b[:, half * LANES:(half + 1) * LANES]

    gates(cur_f, prev_f, next_f, s, 0, w_f, a_f, b_f)
    gates(cur_r, prev_r, next_r, n - 1 - s, 1, w_r, a_r, b_r)

    def block(a_s, b_s, h_s, h, t0, sign):
        idx = [pl.ds(2 * (t0 + sign * j), LRU_LANE_GROUPS, stride=pitch) for j in range(LRU_STEP)]
        a = [a_s[i, :] for i in idx]
        b = [b_s[i, :] for i in idx]
        a01, b01 = a[1] * a[0], a[1] * b[0] + b[1]
        a23, b23 = a[3] * a[2], a[3] * b[2] + b[3]
        a03, b03 = a23 * a01, a23 * b01 + b23
        h0 = a[0] * h + b[0]
        h1 = a01 * h + b01
        h3 = a03 * h + b03
        h2 = a[2] * h1 + b[2]
        for i, v in zip(idx, (h0, h1, h2, h3)):
            h_s[i, :] = v
        return h3

    blocks_per_iter = 8

    def body(it, hs):
        hf, hr = hs
        for j in range(blocks_per_iter):
            t = (it * blocks_per_iter + j) * LRU_STEP
            hf = block(a_f, b_f, h_f, hf, t, 1)
            hr = block(a_r, b_r, h_r, hr, rows - 1 - t, -1)
        return hf, hr

    hf, hr = lax.fori_loop(0, rows // (LRU_STEP * blocks_per_iter), body, (carry[0], carry[1]))
    carry[0] = hf
    carry[1] = hr

    for grp in range(LRU_LANE_GROUPS):
        slab_rows = pl.ds(grp * pitch, rows, stride=2)
        hf_ref[:, grp * LANES:(grp + 1) * LANES] = h_f[slab_rows, :].astype(hf_ref.dtype)
        hr_ref[:, grp * LANES:(grp + 1) * LANES] = h_r[slab_rows, :].astype(hr_ref.dtype)


def _lru_mixer(u, conv_w, conv_b, wg, gb, lam, *, batch, seq, rows):
    T = u.shape[0]
    rows = min(rows, seq)
    n = seq // rows
    pitch = 2 * rows + 4
    hb = rows // LRU_HALO
    last_halo = T // LRU_HALO - 1

    def cur_map(rev):
        def f(b, c, s):
            chunk = (n - 1 - s) if rev else s
            return (b * n + chunk, c)
        return f

    def prev_map(rev):
        def f(b, c, s):
            chunk = (n - 1 - s) if rev else s
            return (jnp.maximum((b * n + chunk) * hb - 1, 0), c)
        return f

    def next_map(rev):
        def f(b, c, s):
            chunk = (n - 1 - s) if rev else s
            return (jnp.minimum((b * n + chunk + 1) * hb, last_halo), c)
        return f

    blocks = LRU_CH // LRU_BLOCK
    in_specs = []
    for rev in (False, True):
        in_specs += [
            pl.BlockSpec((rows, LRU_CH), cur_map(rev)),
            pl.BlockSpec((LRU_HALO, LRU_CH), prev_map(rev)),
            pl.BlockSpec((LRU_HALO, LRU_CH), next_map(rev)),
        ]
    in_specs += [
        pl.BlockSpec((CONV_W, LRU_CH), lambda b, c, s: (0, c)),
        pl.BlockSpec((1, LRU_CH), lambda b, c, s: (0, c)),
        pl.BlockSpec((2, blocks, LRU_BLOCK, 2 * LRU_BLOCK), lambda b, c, s: (0, c, 0, 0)),
        pl.BlockSpec((2, blocks, 1, 2 * LRU_BLOCK), lambda b, c, s: (0, c, 0, 0)),
        pl.BlockSpec((2, 1, LRU_CH), lambda b, c, s: (0, 0, c)),
    ]
    slab = pltpu.VMEM((LRU_LANE_GROUPS * pitch, LANES), jnp.float32)
    window = pltpu.VMEM((LRU_LANE_GROUPS * 2 * (rows + 2 * LRU_HALO), LANES), jnp.float32)
    out = jax.ShapeDtypeStruct((T, MIX_WIDTH), jnp.bfloat16)
    return pl.pallas_call(
        functools.partial(_lru_kernel, rows=rows, pitch=pitch),
        grid=(batch, MIX_WIDTH // LRU_CH, n),
        in_specs=in_specs,
        out_specs=[pl.BlockSpec((rows, LRU_CH), cur_map(False)),
                   pl.BlockSpec((rows, LRU_CH), cur_map(True))],
        out_shape=[out, out],
        scratch_shapes=[window, window, slab, slab, slab, slab, slab, slab,
                        pltpu.VMEM((2, LRU_LANE_GROUPS, LANES), jnp.float32)],
        compiler_params=_params("parallel", "parallel", "arbitrary"),
        name="rglru",
    )(u, u, u, u, u, u, conv_w, conv_b.reshape(1, MIX_WIDTH), wg, gb,
      lam.reshape(2, 1, MIX_WIDTH))


def _mla_q_kernel(cq_ref, g_ref, wa_ref, wb_ref, ct_ref, st_ref, o_ref, h_ref, *, heads):
    @pl.when(pl.program_id(1) == 0)
    def _():
        x = cq_ref[...].astype(jnp.float32)
        ms = jnp.mean(x * x, axis=-1, keepdims=True)
        h_ref[...] = ((x * lax.rsqrt(ms + EPS)) * g_ref[...]).astype(h_ref.dtype)

    h = h_ref[...]
    nt = (((1,), (1,)), ((), ()))
    qa = lax.dot_general(wa_ref[...], h, nt, preferred_element_type=jnp.float32)
    qb = lax.dot_general(wb_ref[...], h, nt, preferred_element_type=jnp.float32)
    scale = (QK_NOPE + QK_ROPE) ** -0.5 * math.log2(math.e)
    ct = ct_ref[...] * scale
    st = st_ref[...] * scale
    for hh in range(heads):
        lo = hh * QK_PAD
        o_ref[lo:lo + LANES, :] = (qa[lo:lo + LANES] * scale).astype(o_ref.dtype)
        rope = qa[lo + LANES:lo + 2 * LANES] * ct + qb[hh * LANES:(hh + 1) * LANES] * st
        o_ref[lo + LANES:lo + 2 * LANES, :] = rope.astype(o_ref.dtype)


def _mla_q(u, q_norm, wa_t, wb_t, ct_t, st_t, *, seq, tm, heads):
    T = u.shape[0]
    tm = min(tm, seq)
    per_seq = seq // tm
    cq_blk = B_CQ_OFF // Q_LORA
    return pl.pallas_call(
        functools.partial(_mla_q_kernel, heads=heads),
        grid=(T // tm, MLA_HEADS // heads),
        in_specs=[
            pl.BlockSpec((tm, Q_LORA), lambda i, j: (i, cq_blk)),
            pl.BlockSpec((1, Q_LORA), lambda i, j: (0, 0)),
            pl.BlockSpec((heads * QK_PAD, Q_LORA), lambda i, j: (j, 0)),
            pl.BlockSpec((heads * LANES, Q_LORA), lambda i, j: (j, 0)),
            pl.BlockSpec((LANES, tm), lambda i, j: (0, i % per_seq)),
            pl.BlockSpec((LANES, tm), lambda i, j: (0, i % per_seq)),
        ],
        out_specs=pl.BlockSpec((heads * QK_PAD, tm), lambda i, j: (j, i)),
        out_shape=jax.ShapeDtypeStruct((MLA_HEADS * QK_PAD, T), jnp.bfloat16),
        scratch_shapes=[pltpu.VMEM((tm, Q_LORA), jnp.bfloat16)],
        compiler_params=_params("parallel", "arbitrary"),
        name="mla_q_proj",
    )(u, q_norm.reshape(1, Q_LORA), wa_t, wb_t, ct_t, st_t)


def _mla_kv_kernel(ckv_ref, g_ref, wk_ref, wvt_ref, kr_ref, tab_ref, k_ref, vt_ref):
    x = ckv_ref[...].astype(jnp.float32)
    ms = jnp.mean(x * x, axis=-1, keepdims=True)
    h = ((x * lax.rsqrt(ms + EPS)) * g_ref[...]).astype(jnp.bfloat16)
    kn = jnp.dot(h, wk_ref[...], preferred_element_type=jnp.float32)
    vt = lax.dot_general(wvt_ref[...], h, (((1,), (1,)), ((), ())),
                         preferred_element_type=jnp.float32).astype(vt_ref.dtype)
    ones = jnp.ones((VT_ROWS - V_DIM, vt.shape[1]), vt_ref.dtype)
    for hh in range(MLA_HEADS):
        vt_ref[hh * VT_ROWS:hh * VT_ROWS + V_DIM, :] = vt[hh * V_DIM:(hh + 1) * V_DIM, :]
        vt_ref[hh * VT_ROWS + V_DIM:(hh + 1) * VT_ROWS, :] = ones
    t = kr_ref[...].astype(jnp.float32) * tab_ref[...]
    rot = t + pltpu.roll(t, QK_ROPE, 1)
    lane = lax.broadcasted_iota(jnp.int32, rot.shape, 1)
    rot = jnp.where(lane < QK_ROPE, rot, jnp.where(lane == QK_ROPE, 1.0, 0.0)).astype(k_ref.dtype)
    for hh in range(MLA_HEADS):
        k_ref[:, hh * QK_PAD:hh * QK_PAD + LANES] = kn[:, hh * LANES:(hh + 1) * LANES].astype(k_ref.dtype)
        k_ref[:, hh * QK_PAD + LANES:(hh + 1) * QK_PAD] = rot


def _mla_kv(u, kv_norm, wk, wvt, tab, *, seq, tm):
    T = u.shape[0]
    tm = min(tm, seq)
    per_seq = seq // tm
    ckv_blk = B_CKV_OFF // KV_LORA
    kr_blk = B_KR_OFF // LANES
    return pl.pallas_call(
        _mla_kv_kernel,
        grid=(T // tm,),
        in_specs=[
            pl.BlockSpec((tm, KV_LORA), lambda i: (i, ckv_blk)),
            pl.BlockSpec((1, KV_LORA), lambda i: (0, 0)),
            pl.BlockSpec((KV_LORA, MLA_HEADS * QK_NOPE), lambda i: (0, 0)),
            pl.BlockSpec((MLA_HEADS * V_DIM, KV_LORA), lambda i: (0, 0)),
            pl.BlockSpec((tm, LANES), lambda i: (i, kr_blk)),
            pl.BlockSpec((tm, LANES), lambda i: (i % per_seq, 0)),
        ],
        out_specs=[pl.BlockSpec((tm, MLA_HEADS * QK_PAD), lambda i: (i, 0)),
                   pl.BlockSpec((MLA_HEADS * VT_ROWS, tm), lambda i: (0, i))],
        out_shape=[jax.ShapeDtypeStruct((T, MLA_HEADS * QK_PAD), jnp.bfloat16),
                   jax.ShapeDtypeStruct((MLA_HEADS * VT_ROWS, T), jnp.bfloat16)],
        compiler_params=_params("parallel"),
        name="mla_kv_proj",
    )(u, kv_norm.reshape(1, KV_LORA), wk, wvt, u, tab)


def _flash_head(q_ref, k_ref, vt_ref, o_ref, qt_buf, acc_ref, *, tk, n_kv):
    bf = jnp.bfloat16
    qt = q_ref[...]
    qt_buf[0] = qt
    qt_buf[1] = qt
    refs = {}

    def set_reference(t, value):
        ref = value.astype(bf)
        qt_buf[t % 2, AUG_ROW:AUG_ROW + BF16_ROWS, :] = jnp.broadcast_to(-ref, (BF16_ROWS, ref.shape[1]))
        refs[t] = ref.astype(jnp.float32)

    first = jnp.dot(k_ref[:BF16_ROWS, :], qt, preferred_element_type=jnp.float32)
    set_reference(0, jnp.max(first, axis=0, keepdims=True))
    m_true = worst = None
    for j in range(n_kv):
        k = k_ref[j * tk:(j + 1) * tk, :]
        s = jnp.dot(k, qt_buf[j % 2], preferred_element_type=jnp.float32)
        p = jnp.exp2(s.astype(bf))
        pmax = jnp.max(p, axis=0, keepdims=True).astype(jnp.float32)
        seen = refs[j] + jnp.log2(pmax)
        m_true = seen if j == 0 else jnp.maximum(m_true, seen)
        worst = pmax if j == 0 else jnp.maximum(worst, pmax)
        for t in ([1, 2] if j == 0 else [j + 2]):
            if t < n_kv:
                set_reference(t, m_true)
        pv = jnp.dot(vt_ref[:, j * tk:(j + 1) * tk], p, preferred_element_type=jnp.float32)
        if j == 0:
            acc_ref[...] = pv
        else:
            acc_ref[...] = jnp.exp2(refs[j - 1] - refs[j]) * acc_ref[...] + pv

    def finish():
        out = acc_ref[:V_DIM, :] / acc_ref[V_DIM:V_DIM + 1, :]
        o_ref[...] = out.T.astype(o_ref.dtype)

    finish()

    def exact():
        m_run = None
        for j in range(n_kv):
            s = jnp.dot(k_ref[j * tk:(j + 1) * tk, :], qt, preferred_element_type=jnp.float32)
            cmax = jnp.max(s, axis=0, keepdims=True)
            m_new = cmax if j == 0 else jnp.maximum(m_run, cmax)
            p = jnp.exp2((s - m_new).astype(bf))
            pv = jnp.dot(vt_ref[:, j * tk:(j + 1) * tk], p, preferred_element_type=jnp.float32)
            if j == 0:
                acc_ref[...] = pv
            else:
                acc_ref[...] = jnp.exp2(m_run - m_new) * acc_ref[...] + pv
            m_run = m_new
        finish()

    return worst, exact


def _flash_kernel(q_ref, k_ref, vt_ref, o_ref, qt_buf, acc_ref, *, tk, n_kv, heads):
    pending = []
    for h in range(heads):
        pending.append(_flash_head(
            q_ref.at[h * QK_PAD:(h + 1) * QK_PAD, :], k_ref.at[:, h * QK_PAD:(h + 1) * QK_PAD],
            vt_ref.at[h * VT_ROWS:(h + 1) * VT_ROWS, :], o_ref.at[:, h * V_DIM:(h + 1) * V_DIM],
            qt_buf.at[h], acc_ref.at[h], tk=tk, n_kv=n_kv))
    for worst, exact in pending:
        pl.when(jnp.logical_not(jnp.max(worst) <= 2.0 ** FAST_PATH_LIMIT))(exact)


def _flash_attention(qt, k, vt, *, batch, seq, tq, tk, heads):
    T = k.shape[0]
    tq = min(tq, seq)
    tk = min(tk, seq)
    nq = seq // tq
    return pl.pallas_call(
        functools.partial(_flash_kernel, tk=tk, n_kv=seq // tk, heads=heads),
        grid=(batch, MLA_HEADS // heads, nq),
        in_specs=[
            pl.BlockSpec((heads * QK_PAD, tq), lambda b, h, i: (h, b * nq + i)),
            pl.BlockSpec((seq, heads * QK_PAD), lambda b, h, i: (b, h)),
            pl.BlockSpec((heads * VT_ROWS, seq), lambda b, h, i: (h, b)),
        ],
        out_specs=pl.BlockSpec((tq, heads * V_DIM), lambda b, h, i: (b * nq + i, h)),
        out_shape=jax.ShapeDtypeStruct((T, MLA_HEADS * V_DIM), jnp.bfloat16),
        scratch_shapes=[pltpu.VMEM((heads, 2, QK_PAD, tq), jnp.bfloat16),
                        pltpu.VMEM((heads, VT_ROWS, tq), jnp.float32)],
        compiler_params=_params("parallel", "parallel", "arbitrary"),
        name="mla_flash_attention",
    )(qt, k, vt)


def _pool_kernel(cur_ref, prev_ref, next_ref, w_ref, sc_ref, o_ref, *, seq, tm):
    i = pl.program_id(1)
    n = pl.num_programs(1)
    win = tm + 2 * POOL_HALO
    keep_prev = (i > 0).astype(jnp.float32)
    keep_next = (i < n - 1).astype(jnp.float32)
    t = i * tm + lax.broadcasted_iota(jnp.int32, (tm, 1), 0)
    for g, w in enumerate(POOL_WINDOWS):
        lo = g * POOL_GROUP
        x = cur_ref[:, lo:lo + POOL_GROUP].astype(jnp.float32)
        p = prev_ref[:, lo:lo + POOL_GROUP].astype(jnp.float32) * keep_prev
        q = next_ref[:, lo:lo + POOL_GROUP].astype(jnp.float32) * keep_next
        run = jnp.concatenate([p, x, q], axis=0)
        span = 1
        while span < w:
            run = run + pltpu.roll(run, span, 0)
            span *= 2
        ahead = w - w // 2 - 1
        if ahead:
            run = pltpu.roll(run, win - ahead, 0)
        total = run[POOL_HALO:POOL_HALO + tm]
        count = jnp.minimum(t + (w - w // 2), seq) - jnp.maximum(t - w // 2, 0)
        pooled = total / count.astype(jnp.float32) - x
        y = jnp.dot(pooled.astype(jnp.bfloat16), w_ref[g], preferred_element_type=jnp.float32)
        o_ref[:, lo:lo + POOL_GROUP] = (y * sc_ref[:, lo:lo + POOL_GROUP]).astype(o_ref.dtype)


def _pool_mixer(u, w_group, scale, *, batch, seq, tm):
    T = u.shape[0]
    tm = min(tm, seq)
    n = seq // tm
    hb = tm // POOL_HALO
    last_halo = T // POOL_HALO - 1
    return pl.pallas_call(
        functools.partial(_pool_kernel, seq=seq, tm=tm),
        grid=(batch, n),
        in_specs=[
            pl.BlockSpec((tm, MIX_WIDTH), lambda b, i: (b * n + i, 0)),
            pl.BlockSpec((POOL_HALO, MIX_WIDTH),
                         lambda b, i: (jnp.maximum((b * n + i) * hb - 1, 0), 0)),
            pl.BlockSpec((POOL_HALO, MIX_WIDTH),
                         lambda b, i: (jnp.minimum((b * n + i + 1) * hb, last_halo), 0)),
            pl.BlockSpec((len(POOL_WINDOWS), POOL_GROUP, POOL_GROUP), lambda b, i: (0, 0, 0)),
            pl.BlockSpec((1, MIX_WIDTH), lambda b, i: (0, 0)),
        ],
        out_specs=pl.BlockSpec((tm, MIX_WIDTH), lambda b, i: (b * n + i, 0)),
        out_shape=jax.ShapeDtypeStruct((T, MIX_WIDTH), jnp.bfloat16),
        compiler_params=_params("parallel", "parallel"),
        name="pool_mixer",
    )(u, u, u, w_group, scale.reshape(1, MIX_WIDTH))


def _rope_tables(seq):
    inv_freq = 1.0 / (ROPE_THETA ** (jnp.arange(0, QK_ROPE, 2, dtype=jnp.float32) / QK_ROPE))
    ang = jnp.arange(seq, dtype=jnp.float32)[:, None] * inv_freq[None, :]
    cos, sin = jnp.cos(ang), jnp.sin(ang)
    zeros = jnp.zeros((seq, LANES - QK_ROPE), jnp.float32)
    ct = jnp.concatenate([cos, cos, zeros], axis=-1).T
    st = jnp.concatenate([-sin, sin, zeros], axis=-1).T
    tab = jnp.concatenate([cos, cos, -sin, sin], axis=-1)
    return ct, st, tab


def _swap_halves(w):
    half = QK_ROPE // 2
    return jnp.concatenate([w[..., half:], w[..., :half]], axis=-1)


def _prep_b(b_w_in, b_w_q_up, b_w_kv_up):
    bf = jnp.bfloat16
    K = b_w_in.shape[0]
    o1 = Q_LORA
    o2 = o1 + KV_LORA
    o3 = o2 + QK_ROPE
    o4 = o3 + XA_WIDTH
    w_cq, w_ckv, w_kr, w_xq, w_gate = (b_w_in[:, :o1], b_w_in[:, o1:o2], b_w_in[:, o2:o3],
                                       b_w_in[:, o3:o4], b_w_in[:, o4:])
    w_in = jnp.concatenate([w_gate, w_xq, w_cq, w_ckv, w_kr, _swap_halves(w_kr),
                            jnp.zeros((K, LANES), b_w_in.dtype)], axis=-1).astype(bf)
    wq = b_w_q_up.reshape(Q_LORA, MLA_HEADS, QK_NOPE + QK_ROPE)
    q_nope, q_rope = wq[..., :QK_NOPE], wq[..., QK_NOPE:]
    pad = jnp.zeros((Q_LORA, MLA_HEADS, LANES - QK_ROPE), wq.dtype)
    wa = jnp.concatenate([q_nope, q_rope, pad], axis=-1).reshape(Q_LORA, MLA_HEADS * QK_PAD).T.astype(bf)
    wb = jnp.concatenate([_swap_halves(q_rope), pad], axis=-1).reshape(Q_LORA, MLA_HEADS * LANES).T.astype(bf)
    wkv = b_w_kv_up.reshape(KV_LORA, MLA_HEADS, QK_NOPE + V_DIM)
    wk = wkv[..., :QK_NOPE].reshape(KV_LORA, MLA_HEADS * QK_NOPE).astype(bf)
    wvt = wkv[..., QK_NOPE:].reshape(KV_LORA, MLA_HEADS * V_DIM).T.astype(bf)
    return w_in, wa, wb, wk, wvt


def _prep_lru(gate_w, gate_b):
    wg = jnp.concatenate([gate_w[:, 0], gate_w[:, 1]], axis=-1).astype(jnp.bfloat16)
    gb = gate_b.reshape(2, 2, LRU_BLOCKS, 1, LRU_BLOCK)
    gb = 0.5 * jnp.concatenate([gb[:, 0], gb[:, 1]], axis=-1)
    return wg, gb


def _trunk(x, mem, w):
    batch, seq, _ = x.shape
    T = batch * seq
    x = x.reshape(T, D_MODEL)
    mem = mem.reshape(batch * MEM_LEN, D_MODEL)
    ct, st, tab = _rope_tables(seq)
    depth = w["norm_pre"].shape[0]
    for i in range(depth):
        kind, j = i % N_MIXERS, i // N_MIXERS
        kv = _norm_matmul(mem, w["norm_mem"][i], w["w_mem_kv"], i, tm=512, tn=1024)
        if kind == 0:
            u = _norm_matmul(x, w["norm_pre"][i], w["a_w_in"], j, tm=1024, tn=2048)
            mixes = _lru_mixer(u, w["a_conv_w"][j], w["a_conv_b"][j], w["a_wg"][j], w["a_gb"][j],
                               w["a_lambda"][j], batch=batch, seq=seq, rows=512)
            xq_off, gate_off = MIX_WIDTH, BRANCH
        elif kind == 1:
            u = _norm_matmul(x, w["norm_pre"][i], w["b_w_in"], j, tm=1024, tn=2048)
            q = _mla_q(u, w["b_q_norm"][j], w["b_wa"][j], w["b_wb"][j], ct, st,
                       seq=seq, tm=1024, heads=8)
            k, vt = _mla_kv(u, w["b_kv_norm"][j], w["b_wk"][j], w["b_wvt"][j], tab, seq=seq, tm=512)
            mixes = [_flash_attention(q, k, vt, batch=batch, seq=seq, tq=1024, tk=512, heads=1)]
            xq_off, gate_off = B_XQ_OFF, B_GATE_OFF
        else:
            u = _norm_matmul(x, w["norm_pre"][i], w["c_w_in"], j, tm=1024, tn=2048)
            mixes = [_pool_mixer(u, w["c_w_group"][j], w["c_scale"][j], batch=batch, seq=seq, tm=512)]
            xq_off, gate_off = MIX_WIDTH, BRANCH
        x = _out_block(mixes, u, xq_off, gate_off, kv, w["w_out"], i, x, w["norm_post"][i],
                       seq=seq, tm=512)
    return x.reshape(batch, seq, D_MODEL)


def kernel(x_prompt, x_sample, mem_prompt, mem_sample, norm_pre, norm_post, norm_mem, w_mem_kv, w_out,
           a_w_in, a_conv_w, a_conv_b, a_gate_w, a_gate_b, a_lambda,
           b_w_in, b_q_norm, b_kv_norm, b_w_q_up, b_w_kv_up,
           c_w_in, c_w_group, c_scale):
    bf = jnp.bfloat16
    b_parts = [_prep_b(b_w_in[j], b_w_q_up[j], b_w_kv_up[j]) for j in range(b_w_in.shape[0])]
    lru_parts = [_prep_lru(a_gate_w[j], a_gate_b[j]) for j in range(a_gate_w.shape[0])]
    w = dict(
        norm_pre=norm_pre, norm_post=norm_post, norm_mem=norm_mem,
        w_mem_kv=w_mem_kv.astype(bf), w_out=w_out.astype(bf),
        a_w_in=a_w_in.astype(bf), a_conv_w=a_conv_w, a_conv_b=a_conv_b,
        a_wg=[p[0] for p in lru_parts], a_gb=[p[1] for p in lru_parts], a_lambda=a_lambda,
        b_w_in=jnp.stack([p[0] for p in b_parts]), b_q_norm=b_q_norm, b_kv_norm=b_kv_norm,
        b_wa=[p[1] for p in b_parts], b_wb=[p[2] for p in b_parts],
        b_wk=[p[3] for p in b_parts], b_wvt=[p[4] for p in b_parts],
        c_w_in=c_w_in.astype(bf), c_w_group=[c_w_group[j].astype(bf) for j in range(c_w_group.shape[0])],
        c_scale=c_scale,
    )
    return (_trunk(x_prompt, mem_prompt, w), _trunk(x_sample, mem_sample, w))
```
